```python
import jax, jax.numpy as jnp
from jax import lax
import numpy as np

D_MODEL = 1024
BATCH = 4
SEQ = 4096
DEPTH = 1

CHUNK = 64
Q_BLOCK = 128
D_MIX = D_MODEL
A_HEADS = 8
A_HEAD_DIM = 64
A_WIDTH = A_HEADS * A_HEAD_DIM
IDX_HEADS = 4
IDX_DIM = 64
TOPK_MAX = 256
B_HEADS = 4
B_KEY_DIM = 64
B_VAL_DIM = 128
B_WIDTH = B_HEADS * B_VAL_DIM
ROPE_THETA = 500000.0
ROT_DIM = A_HEAD_DIM // 4
N_EXPERTS = 32
TOP_K_EXPERTS = 4
D_EXPERT = D_MODEL
SWIGLU_LIMIT = 7.0
SWIGLU_ALPHA = 1.702
EXPERT_BLOCK = 256
RMS_EPS = 1e-6

SPLITS = (A_WIDTH, A_WIDTH, A_WIDTH,
          IDX_HEADS * IDX_DIM, IDX_DIM, IDX_HEADS,
          B_HEADS * B_KEY_DIM, B_HEADS * B_KEY_DIM,
          B_WIDTH, B_WIDTH)
D_IN = sum(SPLITS)
SPLIT_POINTS = tuple(int(v) for v in np.cumsum(SPLITS)[:-1])

kernel_name = "hybrid_dsa_hgrn2_moe_adaln_block"

F32 = jnp.float32


def rmsnorm(x, g):
    xf = x.astype(F32)
    y = xf * lax.rsqrt(jnp.mean(xf * xf, axis=-1, keepdims=True) + RMS_EPS)
    return (y * g.astype(F32)).astype(x.dtype)


def rope_tables(positions):
    inv_freq = ROPE_THETA ** (-(jnp.arange(0, ROT_DIM, 2, dtype=F32) / ROT_DIM))
    ang = positions.astype(F32)[..., None] * inv_freq
    return jnp.cos(ang)[:, :, None, :], jnp.sin(ang)[:, :, None, :]


def apply_partial_rope(x, cos, sin):
    r = cos.shape[-1]
    x1, x2, rest = x[..., :r], x[..., r:2 * r], x[..., 2 * r:]
    xf1, xf2 = x1.astype(F32), x2.astype(F32)
    rot = jnp.concatenate([xf1 * cos - xf2 * sin, xf2 * cos + xf1 * sin], axis=-1)
    return jnp.concatenate([rot.astype(x.dtype), rest], axis=-1)


def dsa_attention(q, k, v, qi, ki, wi):
    B, S = q.shape[0], q.shape[1]
    topk = min(TOPK_MAX, S // 4)
    nqb = S // Q_BLOCK
    key_chunk = jnp.arange(S) // CHUNK
    scale = A_HEAD_DIM ** -0.5

    def blocks(a):
        return jnp.swapaxes(a.reshape((B, nqb, Q_BLOCK) + a.shape[2:]), 0, 1)

    def one_block(args):
        n, qb, qib, wib = args
        q_chunk = (n * Q_BLOCK + jnp.arange(Q_BLOCK)) // CHUNK
        logits = jnp.einsum('bqhd,bsd->bqhs', qib, ki) * (IDX_DIM ** -0.5)
        score = jnp.einsum('bqh,bqhs->bqs', wib, jax.nn.relu(logits)).astype(F32)
        admissible = key_chunk[None, :] <= q_chunk[:, None]
        score = jnp.where(admissible[None], score, -jnp.inf)
        _, idx = lax.top_k(score, topk)
        valid = (idx // CHUNK) <= q_chunk[None, :, None]
        k_sel = jax.vmap(lambda kb, ib: kb[ib])(k, idx)
        v_sel = jax.vmap(lambda vb, ib: vb[ib])(v, idx)
        s = jnp.einsum('bqhd,bqkhd->bqhk', qb, k_sel).astype(F32) * scale
        s = jnp.where(valid[:, :, None, :], s, -jnp.inf)
        p = jax.nn.softmax(s, axis=-1).astype(v.dtype)
        return jnp.einsum('bqhk,bqkhd->bqhd', p, v_sel)

    out = lax.map(one_block, (jnp.arange(nqb), blocks(q), blocks(qi), blocks(wi)))
    return jnp.swapaxes(out, 0, 1).reshape(B, S, -1)


def hgrn2_chunkwise(q, log_f, k, v):
    B, S, H, dk = q.shape
    dv = v.shape[-1]
    nc = S // CHUNK

    def to_chunks(a):
        return a.astype(F32).reshape(B, nc, CHUNK, H, a.shape[-1]).transpose(1, 0, 3, 2, 4)

    tri = jnp.tril(jnp.ones((CHUNK, CHUNK), bool))[:, :, None]

    def step(state, inp):
        qc, lfc, kc, vc = inp
        b = jnp.cumsum(lfc, axis=2)
        o_inter = jnp.einsum('bhtd,bhde->bhte', qc * jnp.exp(b), state)
        diff = b[:, :, :, None, :] - b[:, :, None, :, :]
        decay = jnp.exp(jnp.where(tri, diff, -jnp.inf))
        attn = jnp.einsum('bhtd,bhsd,bhtsd->bhts', qc, kc, decay)
        o_intra = jnp.einsum('bhts,bhse->bhte', attn, vc)
        b_last = b[:, :, -1:, :]
        new_state = (jnp.exp(b_last[:, :, 0, :])[..., None] * state
                     + jnp.einsum('bhsd,bhse->bhde', kc * jnp.exp(b_last - b), vc))
        return new_state, o_inter + o_intra

    s0 = jnp.zeros((B, H, dk, dv), F32)
    _, o = lax.scan(step, s0, (to_chunks(q), to_chunks(log_f), to_chunks(k), to_chunks(v)))
    return o.transpose(1, 0, 3, 2, 4).reshape(B, S, H, dv)


def moe_ffn(h, router_w, router_b, w1, b1, w2, b2):
    B, S, D = h.shape
    T = B * S
    xt = h.reshape(T, D)
    logits = (xt @ router_w + router_b).astype(F32)
    top_vals, top_idx = lax.top_k(logits, TOP_K_EXPERTS)
    gates = jax.nn.softmax(top_vals, axis=-1)

    n_assign = T * TOP_K_EXPERTS
    e_flat = top_idx.reshape(-1).astype(jnp.int32)
    tok_flat = jnp.arange(n_assign, dtype=jnp.int32) // TOP_K_EXPERTS
    order = jnp.argsort(e_flat)
    e_sorted = e_flat[order]
    tok_sorted = tok_flat[order]
    counts = jnp.bincount(e_flat, length=N_EXPERTS).astype(jnp.int32)
    padded = (counts + EXPERT_BLOCK - 1) // EXPERT_BLOCK * EXPERT_BLOCK
    start = jnp.cumsum(counts) - counts
    pad_end = jnp.cumsum(padded)
    pad_start = pad_end - padded
    row = pad_start[e_sorted] + jnp.arange(n_assign, dtype=jnp.int32) - start[e_sorted]
    n_blocks = -(-n_assign // EXPERT_BLOCK) + N_EXPERTS
    n_rows = n_blocks * EXPERT_BLOCK
    row_tok = jnp.full((n_rows,), T, jnp.int32).at[row].set(tok_sorted)
    x_rows = jnp.concatenate([xt, jnp.zeros((1, D), xt.dtype)], axis=0)[row_tok]
    x_rows = x_rows.reshape(n_blocks, EXPERT_BLOCK, D)
    block_start = jnp.arange(n_blocks, dtype=jnp.int32) * EXPERT_BLOCK
    block_expert = jnp.minimum(jnp.searchsorted(pad_end, block_start, side='right'),
                               N_EXPERTS - 1).astype(jnp.int32)

    def expert_block(args):
        xb, e = args
        hg = xb @ w1[e] + b1[e]
        glu, lin = jnp.split(hg, 2, axis=-1)
        glu = jnp.minimum(glu, SWIGLU_LIMIT)
        lin = jnp.clip(lin, -SWIGLU_LIMIT, SWIGLU_LIMIT)
        act = glu * jax.nn.sigmoid(SWIGLU_ALPHA * glu) * (lin + 1)
        return act @ w2[e] + b2[e]

    y_rows = lax.map(expert_block, (x_rows, block_expert)).reshape(n_rows, D)
    row_orig = jnp.zeros((n_assign,), jnp.int32).at[order].set(row)
    y = y_rows[row_orig].reshape(T, TOP_K_EXPERTS, D)
    out = jnp.einsum('tk,tkd->td', gates.astype(y.dtype), y)
    return out.reshape(B, S, D)


def setup_inputs(seed: int = 0) -> dict:
    key = jax.random.key(seed)
    ks = jax.random.split(key, 20)
    D, F = D_MODEL, D_EXPERT
    nrm = jax.random.normal
    x = nrm(ks[0], (BATCH, SEQ, D), F32)
    c = nrm(ks[1], (BATCH, D), F32)
    offset = jax.random.randint(ks[2], (BATCH, 1), 0, 65536, dtype=jnp.int32)
    positions = (offset + jnp.arange(SEQ, dtype=jnp.int32)[None, :]).astype(jnp.int32)
    ada_w = nrm(ks[3], (DEPTH, D, 6 * D), F32) * (0.5 * D ** -0.5)
    ada_b = nrm(ks[4], (DEPTH, 6 * D), F32) * 0.02
    norm1_g = 1.0 + 0.02 * nrm(ks[5], (DEPTH, D), F32)
    w_in = nrm(ks[6], (DEPTH, D, D_IN), F32) * D ** -0.5
    hg_norm_g = 1.0 + 0.02 * nrm(ks[7], (DEPTH, B_VAL_DIM), F32)
    lb_logits = 0.5 * nrm(ks[8], (DEPTH + 1, B_HEADS * B_KEY_DIM), F32)
    w_out = nrm(ks[9], (DEPTH, D_MIX, D), F32) * D_MIX ** -0.5
    norm2_g = 1.0 + 0.02 * nrm(ks[10], (DEPTH, D), F32)
    router_w = nrm(ks[11], (DEPTH, D, N_EXPERTS), F32) * D ** -0.5
    router_b = 0.01 * nrm(ks[12], (DEPTH, N_EXPERTS), F32)
    moe_w1 = nrm(ks[13], (DEPTH, N_EXPERTS, D, 2 * F), F32) * D ** -0.5
    moe_b1 = 0.01 * nrm(ks[14], (DEPTH, N_EXPERTS, 2 * F), F32)
    moe_w2 = nrm(ks[15], (DEPTH, N_EXPERTS, F, D), F32) * F ** -0.5
    moe_b2 = 0.01 * nrm(ks[16], (DEPTH, N_EXPERTS, D), F32)
    final_g = 1.0 + 0.02 * nrm(ks[17], (D,), F32)
    return {"x": x, "c": c, "positions": positions, "ada_w": ada_w, "ada_b": ada_b,
            "norm1_g": norm1_g, "w_in": w_in, "hg_norm_g": hg_norm_g, "lb_logits": lb_logits,
            "w_out": w_out, "norm2_g": norm2_g, "router_w": router_w, "router_b": router_b,
            "moe_w1": moe_w1, "moe_b1": moe_b1, "moe_w2": moe_w2, "moe_b2": moe_b2,
            "final_g": final_g}


def reference(x, c, positions, ada_w, ada_b, norm1_g, w_in, hg_norm_g, lb_logits,
              w_out, norm2_g, router_w, router_b, moe_w1, moe_b1, moe_w2, moe_b2, final_g):
    B, S, _ = x.shape
    cos, sin = rope_tables(positions)
    lower_bounds = jnp.cumsum(jax.nn.softmax(lb_logits.astype(F32), axis=0), axis=0)

    for layer in range(DEPTH):
        mod = jax.nn.silu(c) @ ada_w[layer] + ada_b[layer]
        shift1, scale1, gate1, shift2, scale2, gate2 = [m[:, None, :] for m in jnp.split(mod, 6, axis=-1)]

        h = rmsnorm(x, norm1_g[layer]) * (1 + scale1) + shift1
        proj = h @ w_in[layer]
        qa, ka, va, qi, ki, wi, qb, fb, ib, gb = jnp.split(proj, SPLIT_POINTS, axis=-1)

        qa = apply_partial_rope(qa.reshape(B, S, A_HEADS, A_HEAD_DIM), cos, sin)
        ka = apply_partial_rope(ka.reshape(B, S, A_HEADS, A_HEAD_DIM), cos, sin)
        va = va.reshape(B, S, A_HEADS, A_HEAD_DIM)
        qi = apply_partial_rope(qi.reshape(B, S, IDX_HEADS, IDX_DIM), cos, sin)
        ki = apply_partial_rope(ki[:, :, None, :], cos, sin)[:, :, 0, :]
        wi = wi * (IDX_HEADS ** -0.5)
        out_a = dsa_attention(qa, ka, va, qi, ki, wi)

        lb = lower_bounds[layer]
        f = lb + (1.0 - lb) * jax.nn.sigmoid(fb.astype(F32))
        log_f = jnp.log(f).reshape(B, S, B_HEADS, B_KEY_DIM)
        k_in = (1.0 - f).reshape(B, S, B_HEADS, B_KEY_DIM)
        o_b = hgrn2_chunkwise(qb.reshape(B, S, B_HEADS, B_KEY_DIM), log_f, k_in,
                              ib.reshape(B, S, B_HEADS, B_VAL_DIM))
        o_b = rmsnorm(o_b, hg_norm_g[layer]).astype(x.dtype)
        out_b = (o_b * jax.nn.silu(gb.reshape(B, S, B_HEADS, B_VAL_DIM))).reshape(B, S, B_WIDTH)

        mix = jnp.concatenate([out_a.astype(x.dtype), out_b], axis=-1)
        x = x + gate1 * (mix @ w_out[layer])

        h2 = rmsnorm(x, norm2_g[layer]) * (1 + scale2) + shift2
        x = x + gate2 * moe_ffn(h2, router_w[layer], router_b[layer], moe_w1[layer],
                                moe_b1[layer], moe_w2[layer], moe_b2[layer])

    return rmsnorm(x, final_g)
```

```python
import functools

import numpy as np
import jax
import jax.numpy as jnp
from jax import lax
from jax.experimental import pallas as pl
from jax.experimental.pallas import tpu as pltpu

F32 = jnp.float32
BF16 = jnp.bfloat16
I32 = jnp.int32
HIGHEST = lax.Precision.HIGHEST

D_MODEL = 1024
CHUNK = 64
A_HEADS = 8
A_HEAD_DIM = 64
A_WIDTH = A_HEADS * A_HEAD_DIM
IDX_HEADS = 4
IDX_DIM = 64
TOPK_MAX = 256
B_HEADS = 4
B_KEY_DIM = 64
B_VAL_DIM = 128
B_WIDTH = B_HEADS * B_VAL_DIM
ROPE_THETA = 500000.0
ROT_DIM = A_HEAD_DIM // 4
ROT_HALF = ROT_DIM // 2
N_EXPERTS = 32
TOP_K_EXPERTS = 4
SWIGLU_LIMIT = 7.0
SWIGLU_ALPHA = 1.702
EXPERT_BLOCK = 256
RMS_EPS = 1e-6

LANES = 128
SUBLANES = 8
VMEM_LIMIT = 56 * 1024 * 1024

NEG_BIG = -1e30
INT_MIN = -(2 ** 31)

DSA_TQ = 256
DSA_KB = 256
HG_STEP_CHUNKS = 8
HG_LEVELS = (32, 16, 8, 4, 2, 1)


def _dot(a, b, precision=None):
    return jnp.dot(a, b, preferred_element_type=F32, precision=precision)


def _dot_nt(a, b, precision=None):
    return lax.dot_general(a, b, (((1,), (1,)), ((), ())),
                           preferred_element_type=F32, precision=precision)


def _sigmoid(x):
    return 1.0 / (1.0 + jnp.exp(-x))


def _adaln_kernel(c_ref, w_ref, b_ref, o_ref):
    c = c_ref[...]
    o_ref[...] = _dot(c * _sigmoid(c), w_ref[...], HIGHEST) + b_ref[...]


def _adaln(c_pad, ada_w, ada_b):
    d = c_pad.shape[1]
    n = ada_w.shape[1]
    bn = 1024
    return pl.pallas_call(
        _adaln_kernel,
        out_shape=jax.ShapeDtypeStruct((c_pad.shape[0], n), F32),
        grid=(n // bn,),
        in_specs=[pl.BlockSpec((c_pad.shape[0], d), lambda j: (0, 0)),
                  pl.BlockSpec((d, bn), lambda j: (0, j)),
                  pl.BlockSpec((1, bn), lambda j: (0, j))],
        out_specs=pl.BlockSpec((c_pad.shape[0], bn), lambda j: (0, j)),
        name="adaln",
    )(c_pad, ada_w, ada_b)


def _trig_kernel(pos_ref, freq_ref, cos_ref, sin_ref):
    ang = pos_ref[...] * freq_ref[...]
    cos_ref[...] = jnp.cos(ang)
    sin_ref[...] = jnp.sin(ang)


def _trig(pos_row, freq_col):
    t = pos_row.shape[1]
    bt = 2048
    return pl.pallas_call(
        _trig_kernel,
        out_shape=(jax.ShapeDtypeStruct((ROT_HALF, t), F32),) * 2,
        grid=(t // bt,),
        in_specs=[pl.BlockSpec((1, bt), lambda i: (0, i)),
                  pl.BlockSpec((ROT_HALF, 1), lambda i: (0, 0))],
        out_specs=(pl.BlockSpec((ROT_HALF, bt), lambda i: (0, i)),) * 2,
        name="trig",
    )(pos_row, freq_col)


def _rope_rows(p, c, s):
    x1 = p[0:ROT_HALF]
    x2 = p[ROT_HALF:ROT_DIM]
    return jnp.concatenate([x1 * c - x2 * s, x2 * c + x1 * s, p[ROT_DIM:]], axis=0)


def _inproj_t_kernel(xt_ref, g_ref, sc_ref, sh_ref, w_ref, cos_ref, sin_ref,
                     qa_ref, v_ref, qi_ref, wt_ref):
    xt = xt_ref[...]
    ms = jnp.mean(xt * xt, axis=0, keepdims=True)
    h = xt * lax.rsqrt(ms + RMS_EPS) * g_ref[...]
    h = h * (1.0 + sc_ref[0]) + sh_ref[0]
    hb = h.astype(BF16)
    c = cos_ref[...]
    s = sin_ref[...]
    pq = _dot(w_ref[0:A_WIDTH, :], hb)
    for hh in range(A_HEADS):
        r = _rope_rows(pq[hh * 64:(hh + 1) * 64], c, s) * (A_HEAD_DIM ** -0.5)
        qa_ref[hh * 64:(hh + 1) * 64, :] = r.astype(BF16)
    pv = _dot(w_ref[A_WIDTH:2 * A_WIDTH, :], hb)
    for cb in range(v_ref.shape[0]):
        v_ref[cb] = pv[:, cb * DSA_KB:(cb + 1) * DSA_KB].astype(BF16)
    pi = _dot(w_ref[2 * A_WIDTH:2 * A_WIDTH + 256, :], hb)
    for hh in range(IDX_HEADS):
        qi_ref[hh * 64:(hh + 1) * 64, :] = _rope_rows(pi[hh * 64:(hh + 1) * 64], c, s) * (IDX_DIM ** -0.5)
    pw = _dot(w_ref[2 * A_WIDTH + 256:2 * A_WIDTH + 264, :], hb)
    wt_ref[...] = pw * (IDX_HEADS ** -0.5)


def _inproj_t(xt, g_col, sc_col, sh_col, w_t, cos_t, sin_t, seq):
    d, t = xt.shape
    tn = 512
    per_b = seq // tn
    rows = w_t.shape[0]
    return pl.pallas_call(
        _inproj_t_kernel,
        out_shape=(jax.ShapeDtypeStruct((A_WIDTH, t), BF16),
                   jax.ShapeDtypeStruct((t // DSA_KB, A_WIDTH, DSA_KB), BF16),
                   jax.ShapeDtypeStruct((IDX_HEADS * IDX_DIM, t), F32),
                   jax.ShapeDtypeStruct((SUBLANES, t), F32)),
        grid=(t // tn,),
        in_specs=[pl.BlockSpec((d, tn), lambda i: (0, i)),
                  pl.BlockSpec((d, 1), lambda i: (0, 0)),
                  pl.BlockSpec((1, d, 1), lambda i: (i // per_b, 0, 0)),
                  pl.BlockSpec((1, d, 1), lambda i: (i // per_b, 0, 0)),
                  pl.BlockSpec((rows, d), lambda i: (0, 0)),
                  pl.BlockSpec((ROT_HALF, tn), lambda i: (0, i)),
                  pl.BlockSpec((ROT_HALF, tn), lambda i: (0, i))],
        out_specs=(pl.BlockSpec((A_WIDTH, tn), lambda i: (0, i)),
                   pl.BlockSpec((tn // DSA_KB, A_WIDTH, DSA_KB), lambda i: (i, 0, 0)),
                   pl.BlockSpec((IDX_HEADS * IDX_DIM, tn), lambda i: (0, i)),
                   pl.BlockSpec((SUBLANES, tn), lambda i: (0, i))),
        compiler_params=pltpu.CompilerParams(vmem_limit_bytes=VMEM_LIMIT),
        name="inproj_t",
    )(xt, g_col, sc_col, sh_col, w_t, cos_t, sin_t)


def _inproj_r_kernel(x_ref, g_ref, sc_ref, sh_ref, w_ref, c_ref, s_ref,
                     ka_ref, ki_ref, qb_ref, fb_ref, ib_ref, gb_ref):
    x = x_ref[...]
    ms = jnp.mean(x * x, axis=-1, keepdims=True)
    h = x * lax.rsqrt(ms + RMS_EPS) * g_ref[...]
    h = h * (1.0 + sc_ref[0]) + sh_ref[0]
    hb = h.astype(BF16)
    c = c_ref[...]
    s = s_ref[...]
    lane = lax.broadcasted_iota(I32, c.shape, 1)
    first = (lane % A_HEAD_DIM) < ROT_HALF

    def rope(p):
        partner = jnp.where(first, pltpu.roll(p, LANES - ROT_HALF, 1), pltpu.roll(p, ROT_HALF, 1))
        return p * c + partner * s

    for j in range(A_WIDTH // LANES):
        p = _dot(hb, w_ref[:, j * LANES:(j + 1) * LANES])
        ka_ref[:, j * LANES:(j + 1) * LANES] = rope(p).astype(BF16)
    o = A_WIDTH
    ki_ref[...] = rope(_dot(hb, w_ref[:, o:o + LANES]))
    o += LANES
    qb_ref[...] = _dot(hb, w_ref[:, o:o + 256])
    o += 256
    fb_ref[...] = _dot(hb, w_ref[:, o:o + 256])
    o += 256
    ib_ref[...] = _dot(hb, w_ref[:, o:o + 512])
    o += 512
    gb_ref[...] = _dot(hb, w_ref[:, o:o + 512])


def _inproj_r(x2, g_row, sc_row, sh_row, w_r, c_tab, s_tab, seq):
    t, d = x2.shape
    tm = 512
    per_b = seq // tm
    cols = w_r.shape[1]
    widths = (A_WIDTH, LANES, 256, 256, 512, 512)
    dts = (BF16, F32, F32, F32, F32, F32)
    return pl.pallas_call(
        _inproj_r_kernel,
        out_shape=tuple(jax.ShapeDtypeStruct((t, w), dt) for w, dt in zip(widths, dts)),
        grid=(t // tm,),
        in_specs=[pl.BlockSpec((tm, d), lambda i: (i, 0)),
                  pl.BlockSpec((1, d), lambda i: (0, 0)),
                  pl.BlockSpec((1, 1, d), lambda i: (i // per_b, 0, 0)),
                  pl.BlockSpec((1, 1, d), lambda i: (i // per_b, 0, 0)),
                  pl.BlockSpec((d, cols), lambda i: (0, 0)),
                  pl.BlockSpec((tm, LANES), lambda i: (i, 0)),
                  pl.BlockSpec((tm, LANES), lambda i: (i, 0))],
        out_specs=tuple(pl.BlockSpec((tm, w), lambda i: (i, 0)) for w in widths),
        compiler_params=pltpu.CompilerParams(vmem_limit_bytes=VMEM_LIMIT),
        name="inproj_r",
    )(x2, g_row, sc_row, sh_row, w_r, c_tab, s_tab)


def _dsa_kernel(qi_ref, wt_ref, ki_ref, qa_ref, ka_ref, v_ref, o_ref,
                keys_ref, m_ref, l_ref, acc_ref):
    j = pl.program_id(1)
    nkb = j + 1
    tq = DSA_TQ
    kb_rows = DSA_KB
    row = lax.broadcasted_iota(I32, (kb_rows, tq), 0)
    col = lax.broadcasted_iota(I32, (kb_rows, tq), 1)
    q_chunk = (j * tq + col) // CHUNK

    def score_block(kb, carry):
        r0 = pl.multiple_of(kb * kb_rows, kb_rows)
        ki = ki_ref[pl.ds(r0, kb_rows), 0:IDX_DIM]
        sc = jnp.zeros((kb_rows, tq), F32)
        for h in range(IDX_HEADS):
            lg = _dot(ki, qi_ref[h * IDX_DIM:(h + 1) * IDX_DIM, :], HIGHEST)
            sc = sc + wt_ref[h:h + 1, :] * jnp.maximum(lg, 0.0)
        sc = jnp.where(sc == 0.0, 0.0, sc)
        bits = pltpu.bitcast(sc, I32)
        key = jnp.where(bits < 0, bits ^ 0x7FFFFFFF, bits)
        adm = ((kb * kb_rows + row) // CHUNK) <= q_chunk
        keys_ref[kb] = jnp.where(adm, key, INT_MIN)
        return carry

    lax.fori_loop(0, nkb, score_block, 0)

    def count(pred_fn):
        def body(kb, acc):
            return acc + jnp.sum(jnp.where(pred_fn(keys_ref[kb], kb), 1, 0), axis=0, keepdims=True)
        return lax.fori_loop(0, nkb, body, jnp.zeros((1, tq), I32))

    def bit_step(i, res):
        cand = res ^ lax.shift_left(jnp.int32(1), 31 - i)
        cnt = count(lambda k, kb: k >= cand)
        return jnp.where(cnt >= TOPK_MAX, cand, res)

    thr = lax.fori_loop(0, 32, bit_step, jnp.full((1, tq), INT_MIN, I32))
    n_gt = count(lambda k, kb: k > thr)
    n_eq = count(lambda k, kb: k == thr)
    need = TOPK_MAX - n_gt
    live = thr != INT_MIN
    excess = jnp.logical_and(n_gt + n_eq > TOPK_MAX, live)
    any_excess = jnp.max(jnp.where(excess, 1, 0)) > 0

    def idx_search():
        def step(i, p):
            cand = p | lax.shift_left(jnp.int32(1), 11 - i)
            cnt = count(lambda k, kb: jnp.logical_and(k == thr, kb * kb_rows + row < cand))
            return jnp.where(cnt < need, cand, p)
        return lax.fori_loop(0, 12, step, jnp.zeros((1, tq), I32))

    p_full = jnp.full((1, tq), 4095, I32)
    p_idx = lax.cond(any_excess, idx_search, lambda: p_full)
    p_idx = jnp.where(live, p_idx, -1)

    m_ref[...] = jnp.full(m_ref.shape, NEG_BIG, F32)
    l_ref[...] = jnp.zeros(l_ref.shape, F32)
    acc_ref[...] = jnp.zeros(acc_ref.shape, F32)

    def attn_block(kb, carry):
        r0 = pl.multiple_of(kb * kb_rows, kb_rows)
        key = keys_ref[kb]
        sel = jnp.logical_or(key > thr,
                             jnp.logical_and(key == thr, kb * kb_rows + row <= p_idx))
        bias = jnp.where(sel, 0.0, NEG_BIG)
        for h in range(A_HEADS):
            hs = slice(h * A_HEAD_DIM, (h + 1) * A_HEAD_DIM)
            s = _dot(ka_ref[pl.ds(r0, kb_rows), hs], qa_ref[hs, :]) + bias
            m_old = m_ref[h:h + 1, :]
            m_new = jnp.maximum(m_old, jnp.max(s, axis=0, keepdims=True))
            alpha = jnp.exp(m_old - m_new)
            p = jnp.exp(s - m_new)
            l_ref[h:h + 1, :] = alpha * l_ref[h:h + 1, :] + jnp.sum(p, axis=0, keepdims=True)
            acc_ref[hs, :] = alpha * acc_ref[hs, :] + _dot(v_ref[kb, hs, :], p.astype(BF16))
            m_ref[h:h + 1, :] = m_new
        return carry

    lax.fori_loop(0, nkb, attn_block, 0)

    for h in range(A_HEADS):
        hs = slice(h * A_HEAD_DIM, (h + 1) * A_HEAD_DIM)
        acc_ref[hs, :] = acc_ref[hs, :] / l_ref[h:h + 1, :]
    o_ref[...] = acc_ref[...].T


def _dsa(qi_t, w_t, ki, qa_t, ka, v_blk, batch, seq):
    t = ka.shape[0]
    nqb = seq // DSA_TQ
    nkb = seq // DSA_KB
    return pl.pallas_call(
        _dsa_kernel,
        out_shape=jax.ShapeDtypeStruct((t, A_WIDTH), F32),
        grid=(batch, nqb),
        in_specs=[pl.BlockSpec((IDX_HEADS * IDX_DIM, DSA_TQ), lambda b, j: (0, b * nqb + j)),
                  pl.BlockSpec((SUBLANES, DSA_TQ), lambda b, j: (0, b * nqb + j)),
                  pl.BlockSpec((seq, LANES), lambda b, j: (b, 0)),
                  pl.BlockSpec((A_WIDTH, DSA_TQ), lambda b, j: (0, b * nqb + j)),
                  pl.BlockSpec((seq, A_WIDTH), lambda b, j: (b, 0)),
                  pl.BlockSpec((nkb, A_WIDTH, DSA_KB), lambda b, j: (b, 0, 0))],
        out_specs=pl.BlockSpec((DSA_TQ, A_WIDTH), lambda b, j: (b * nqb + j, 0)),
        scratch_shapes=[pltpu.VMEM((nkb, DSA_KB, DSA_TQ), I32),
                        pltpu.VMEM((A_HEADS, DSA_TQ), F32),
                        pltpu.VMEM((A_HEADS, DSA_TQ), F32),
                        pltpu.VMEM((A_WIDTH, DSA_TQ), F32)],
        compiler_params=pltpu.CompilerParams(vmem_limit_bytes=VMEM_LIMIT),
        name="dsa",
    )(qi_t, w_t, ki, qa_t, ka, v_blk)


def _hgrn_constants():
    c = CHUNK
    t = np.arange(c)[:, None]
    u = np.arange(c)[None, :]
    blocks = [(u <= t), (u > t)]
    masks = []
    for half in HG_LEVELS:
        mid = (t // (2 * half)) * (2 * half) + half - 1
        right = ((t // half) % 2) == 1
        blocks.append(right & (u > mid) & (u <= t))
        blocks.append((~right) & (u > t) & (u <= mid))
        tt, ss = t, u
        masks.append(((tt // (2 * half)) == (ss // (2 * half)))
                     & ((((tt // half) % 2) == 1) & (((ss // half) % 2) == 0)))
    masks.append(t == u)
    cm = np.concatenate(blocks, axis=0).astype(np.float32)
    cm3 = np.concatenate([cm, cm, cm], axis=1)
    return cm3, np.stack(masks).astype(np.float32)


def _hgrn_kernel(qb_ref, fb_ref, ib_ref, gb_ref, lb_ref, hg_ref, cm_ref, mask_ref, eye_ref,
                 o_ref, state_ref):
    @pl.when(pl.program_id(1) == 0)
    def _():
        state_ref[...] = jnp.zeros(state_ref.shape, F32)

    lb = lb_ref[...]
    c = CHUNK
    nlev = len(HG_LEVELS)

    def chunk_step(ci, carry):
        r0 = pl.multiple_of(ci * c, c)
        f = lb + (1.0 - lb) * _sigmoid(fb_ref[pl.ds(r0, c), :])
        lf = jnp.log(f)
        kin = 1.0 - f
        hi = lf.astype(BF16)
        r1 = lf - hi.astype(F32)
        mid = r1.astype(BF16)
        lo = (r1 - mid.astype(F32)).astype(BF16)
        e = jnp.exp(_dot(cm_ref[...], jnp.concatenate([hi, mid, lo], axis=0)))
        q = qb_ref[pl.ds(r0, c), :]
        q_in = (q * e[0:c]).astype(BF16)
        k_out = (kin * e[c:2 * c]).astype(BF16)
        e_last = e[c - 1:c, :]
        q_lv = [(q * e[(2 + 2 * i) * c:(3 + 2 * i) * c]).astype(BF16) for i in range(nlev)]
        k_lv = [(kin * e[(3 + 2 * i) * c:(4 + 2 * i) * c]).astype(BF16) for i in range(nlev)]
        q_lv.append(q.astype(BF16))
        k_lv.append(kin.astype(BF16))
        for h in range(B_HEADS):
            ks = slice(h * B_KEY_DIM, (h + 1) * B_KEY_DIM)
            vs = slice(h * B_VAL_DIM, (h + 1) * B_VAL_DIM)
            attn = jnp.zeros((c, c), F32)
            for i in range(nlev + 1):
                attn = attn + mask_ref[i] * _dot_nt(q_lv[i][:, ks], k_lv[i][:, ks])
            v = ib_ref[pl.ds(r0, c), vs]
            vb = v.astype(BF16)
            st = state_ref[h]
            o = _dot_nt(q_in[:, ks], st.astype(BF16)) + _dot(attn.astype(BF16), vb)
            v_t = _dot_nt(eye_ref[...], vb).astype(BF16)
            state_ref[h] = st * e_last[:, ks] + _dot(v_t, k_out[:, ks])
            ms = jnp.mean(o * o, axis=-1, keepdims=True)
            y = o * lax.rsqrt(ms + RMS_EPS) * hg_ref[...]
            g = gb_ref[pl.ds(r0, c), vs]
            o_ref[pl.ds(r0, c), vs] = y * (g * _sigmoid(g))
        return carry

    lax.fori_loop(0, HG_STEP_CHUNKS, chunk_step, 0)


def _hgrn(qb, fb, ib, gb, lb_row, hg_row, batch, seq):
    t = qb.shape[0]
    tm = HG_STEP_CHUNKS * CHUNK
    per_b = seq // tm
    cm3, masks = _hgrn_constants()
    cm3 = jnp.asarray(cm3, BF16)
    masks = jnp.asarray(masks, F32)
    eye = jnp.eye(B_VAL_DIM, dtype=BF16)
    return pl.pallas_call(
        _hgrn_kernel,
        out_shape=jax.ShapeDtypeStruct((t, B_WIDTH), F32),
        grid=(batch, per_b),
        in_specs=[pl.BlockSpec((tm, 256), lambda b, i: (b * per_b + i, 0)),
                  pl.BlockSpec((tm, 256), lambda b, i: (b * per_b + i, 0)),
                  pl.BlockSpec((tm, B_WIDTH), lambda b, i: (b * per_b + i, 0)),
                  pl.BlockSpec((tm, B_WIDTH), lambda b, i: (b * per_b + i, 0)),
                  pl.BlockSpec((1, 256), lambda b, i: (0, 0)),
                  pl.BlockSpec((1, B_VAL_DIM), lambda b, i: (0, 0)),
                  pl.BlockSpec(cm3.shape, lambda b, i: (0, 0)),
                  pl.BlockSpec(masks.shape, lambda b, i: (0, 0, 0)),
                  pl.BlockSpec(eye.shape, lambda b, i: (0, 0))],
        out_specs=pl.BlockSpec((tm, B_WIDTH), lambda b, i: (b * per_b + i, 0)),
        scratch_shapes=[pltpu.VMEM((B_HEADS, B_VAL_DIM, B_KEY_DIM), F32)],
        compiler_params=pltpu.CompilerParams(vmem_limit_bytes=VMEM_LIMIT),
        name="hgrn",
    )(qb, fb, ib, gb, lb_row, hg_row, cm3, masks, eye)


def _outproj_kernel(oa_ref, ob_ref, x_ref, g1_ref, wo_ref, n2_ref, sc_ref, sh_ref,
                    rw_ref, rb_ref, tri_ref,
                    x1_ref, h2_ref, idx_ref, gate_ref, rank_ref, cnt_ref, run_ref):
    @pl.when(pl.program_id(0) == 0)
    def _():
        run_ref[...] = jnp.zeros(run_ref.shape, F32)

    half = A_WIDTH
    mix = _dot(oa_ref[...].astype(BF16), wo_ref[0:half, :]) + _dot(ob_ref[...].astype(BF16), wo_ref[half:, :])
    x1 = x_ref[...] + g1_ref[0] * mix
    x1_ref[...] = x1
    ms = jnp.mean(x1 * x1, axis=-1, keepdims=True)
    h2 = x1 * lax.rsqrt(ms + RMS_EPS) * n2_ref[...]
    h2 = h2 * (1.0 + sc_ref[0]) + sh_ref[0]
    h2_ref[...] = h2
    logits = _dot(h2, rw_ref[...], HIGHEST) + rb_ref[...]
    tm = logits.shape[0]
    lane = lax.broadcasted_iota(I32, (tm, LANES), 1)
    work = logits
    vals, idxs = [], []
    for _ in range(TOP_K_EXPERTS):
        m = jnp.max(work, axis=-1, keepdims=True)
        ix = jnp.min(jnp.where(work == m, lane, LANES), axis=-1, keepdims=True)
        vals.append(m)
        idxs.append(ix)
        work = jnp.where(lane == ix, -jnp.inf, work)
    es = [jnp.exp(v - vals[0]) for v in vals]
    tot = es[0] + es[1] + es[2] + es[3]
    onehot = jnp.zeros((tm, LANES), F32)
    idx_out = jnp.zeros((tm, LANES), I32)
    gate_out = jnp.zeros((tm, LANES), F32)
    for k in range(TOP_K_EXPERTS):
        onehot = onehot + jnp.where(lane == idxs[k], 1.0, 0.0)
        idx_out = jnp.where(lane == k, idxs[k], idx_out)
        gate_out = jnp.where(lane == k, es[k] / tot, gate_out)
    idx_ref[...] = idx_out
    gate_ref[...] = gate_out
    before = _dot(tri_ref[...], onehot.astype(BF16)) + run_ref[...]
    rank_out = jnp.zeros((tm, LANES), F32)
    for k in range(TOP_K_EXPERTS):
        rk = jnp.sum(jnp.where(lane == idxs[k], before, 0.0), axis=-1, keepdims=True)
        rank_out = jnp.where(lane == k, rk, rank_out)
    rank_ref[...] = rank_out.astype(I32)
    run = run_ref[...] + jnp.sum(onehot, axis=0, keepdims=True)
    run_ref[...] = run
    cnt_ref[...] = run.astype(I32)


def _outproj(oa, ob, x2, g1, wo, n2_row, sc2, sh2, rw, rb, seq):
    t, d = x2.shape
    tm = 512
    per_b = seq // tm
    tri = jnp.asarray(np.tril(np.ones((tm, tm), np.float32), -1), BF16)
    row_spec = lambda w: pl.BlockSpec((tm, w), lambda i: (i, 0))
    mod_spec = pl.BlockSpec((1, 1, d), lambda i: (i // per_b, 0, 0))
    full = lambda a: pl.BlockSpec(a.shape, lambda i: (0,) * a.ndim)
    return pl.pallas_call(
        _outproj_kernel,
        out_shape=(jax.ShapeDtypeStruct((t, d), F32), jax.ShapeDtypeStruct((t, d), F32),
                   jax.ShapeDtypeStruct((t, LANES), I32), jax.ShapeDtypeStruct((t, LANES), F32),
                   jax.ShapeDtypeStruct((t, LANES), I32), jax.ShapeDtypeStruct((1, LANES), I32)),
        grid=(t // tm,),
        in_specs=[row_spec(A_WIDTH), row_spec(B_WIDTH), row_spec(d), mod_spec, full(wo),
                  full(n2_row), mod_spec, mod_spec, full(rw), full(rb), full(tri)],
        out_specs=(row_spec(d), row_spec(d), row_spec(LANES), row_spec(LANES), row_spec(LANES),
                   pl.BlockSpec((1, LANES), lambda i: (0, 0))),
        scratch_shapes=[pltpu.VMEM((1, LANES), F32)],
        compiler_params=pltpu.CompilerParams(dimension_semantics=("arbitrary",),
                                             vmem_limit_bytes=VMEM_LIMIT),
        name="outproj",
    )(oa, ob, x2, g1, wo, n2_row, sc2, sh2, rw, rb, tri)


DISPATCH_TOKENS = 256


def _row_copy(src_hbm, dst_hbm, src_row, dst_row, sem):
    return pltpu.make_async_copy(src_hbm.at[pl.ds(src_row, 1), :], dst_hbm.at[pl.ds(dst_row, 1), :], sem)


def _dispatch_kernel(rows_ref, h2_hbm, zero_hbm, out_hbm, sem):
    del zero_hbm
    base = pl.program_id(0) * DISPATCH_TOKENS
    n = DISPATCH_TOKENS * TOP_K_EXPERTS

    def issue(a, carry):
        _row_copy(h2_hbm, out_hbm, base + a // TOP_K_EXPERTS, rows_ref[a], sem).start()
        return carry

    lax.fori_loop(0, n, issue, 0)

    def drain(a, carry):
        _row_copy(h2_hbm, out_hbm, 0, 0, sem).wait()
        return carry

    lax.fori_loop(0, n, drain, 0)


def _dispatch(rows_flat, h2, n_rows):
    t, d = h2.shape
    n = DISPATCH_TOKENS * TOP_K_EXPERTS
    zeros = jnp.zeros((n_rows, d), F32)
    return pl.pallas_call(
        _dispatch_kernel,
        out_shape=jax.ShapeDtypeStruct((n_rows, d), F32),
        grid=(t // DISPATCH_TOKENS,),
        in_specs=[pl.BlockSpec((n,), lambda i: (i,), memory_space=pltpu.SMEM),
                  pl.BlockSpec(memory_space=pl.ANY),
                  pl.BlockSpec(memory_space=pl.ANY)],
        out_specs=pl.BlockSpec(memory_space=pl.ANY),
        scratch_shapes=[pltpu.SemaphoreType.DMA(())],
        input_output_aliases={2: 0},
        name="dispatch",
    )(rows_flat, h2, zeros)


def _experts_kernel(be_ref, nu_ref, x_ref, w1_ref, b1_ref, w2_ref, b2_ref, y_ref, w1b_ref, w2b_ref):
    i = pl.program_id(0)
    prev = be_ref[jnp.maximum(i - 1, 0)]
    changed = jnp.logical_or(i == 0, be_ref[i] != prev)
    used = i < nu_ref[0]

    @pl.when(jnp.logical_and(changed, used))
    def _():
        w1b_ref[...] = w1_ref[0].astype(BF16)
        w2b_ref[...] = w2_ref[0].astype(BF16)

    @pl.when(used)
    def _():
        f = w2b_ref.shape[0]
        hg = _dot(x_ref[...].astype(BF16), w1b_ref[...]) + b1_ref[0]
        glu = jnp.minimum(hg[:, :f], SWIGLU_LIMIT)
        lin = jnp.clip(hg[:, f:], -SWIGLU_LIMIT, SWIGLU_LIMIT)
        act = glu * _sigmoid(SWIGLU_ALPHA * glu) * (lin + 1.0)
        y_ref[...] = _dot(act.astype(BF16), w2b_ref[...]) + b2_ref[0]

    @pl.when(jnp.logical_not(used))
    def _():
        y_ref[...] = jnp.zeros(y_ref.shape, F32)


def _experts(block_expert, n_used, x_rows, w1, b1, w2, b2):
    n_rows, d = x_rows.shape
    e, _, f2 = w1.shape
    f = w2.shape[1]
    nb = n_rows // EXPERT_BLOCK
    return pl.pallas_call(
        _experts_kernel,
        out_shape=jax.ShapeDtypeStruct((n_rows, d), F32),
        grid_spec=pltpu.PrefetchScalarGridSpec(
            num_scalar_prefetch=2,
            grid=(nb,),
            in_specs=[pl.BlockSpec((EXPERT_BLOCK, d), lambda i, be, nu: (i, 0)),
                      pl.BlockSpec((1, d, f2), lambda i, be, nu: (be[i], 0, 0)),
                      pl.BlockSpec((1, 1, f2), lambda i, be, nu: (be[i], 0, 0)),
                      pl.BlockSpec((1, f, d), lambda i, be, nu: (be[i], 0, 0)),
                      pl.BlockSpec((1, 1, d), lambda i, be, nu: (be[i], 0, 0))],
            out_specs=pl.BlockSpec((EXPERT_BLOCK, d), lambda i, be, nu: (i, 0)),
            scratch_shapes=[pltpu.VMEM((d, f2), BF16), pltpu.VMEM((f, d), BF16)]),
        compiler_params=pltpu.CompilerParams(dimension_semantics=("arbitrary",),
                                             vmem_limit_bytes=VMEM_LIMIT),
        name="experts",
    )(block_expert, n_used, x_rows, w1, b1.reshape(e, 1, f2), w2, b2.reshape(e, 1, d))


COMBINE_TOKENS = 256


def _combine_kernel(rows_ref, y_hbm, gate_ref, x1_ref, g2_ref, fg_ref, o_ref, buf_ref, sem):
    n = COMBINE_TOKENS * TOP_K_EXPERTS

    def dst(a):
        return buf_ref.at[a % TOP_K_EXPERTS, pl.ds(a // TOP_K_EXPERTS, 1), :]

    def issue(a, carry):
        pltpu.make_async_copy(y_hbm.at[pl.ds(rows_ref[a], 1), :], dst(a), sem).start()
        return carry

    lax.fori_loop(0, n, issue, 0)

    def drain(a, carry):
        pltpu.make_async_copy(y_hbm.at[pl.ds(0, 1), :], dst(a), sem).wait()
        return carry

    lax.fori_loop(0, n, drain, 0)

    gates = gate_ref[...]
    moe = gates[:, 0:1] * buf_ref[0]
    for k in range(1, TOP_K_EXPERTS):
        moe = moe + gates[:, k:k + 1] * buf_ref[k]
    x2 = x1_ref[...] + g2_ref[0] * moe
    ms = jnp.mean(x2 * x2, axis=-1, keepdims=True)
    o_ref[...] = x2 * lax.rsqrt(ms + RMS_EPS) * fg_ref[...]


def _combine(rows_flat, y_rows, gates, x1, g2, fg_row, seq):
    t, d = x1.shape
    tm = COMBINE_TOKENS
    per_b = seq // tm
    n = tm * TOP_K_EXPERTS
    return pl.pallas_call(
        _combine_kernel,
        out_shape=jax.ShapeDtypeStruct((t, d), F32),
        grid=(t // tm,),
        in_specs=[pl.BlockSpec((n,), lambda i: (i,), memory_space=pltpu.SMEM),
                  pl.BlockSpec(memory_space=pl.ANY),
                  pl.BlockSpec((tm, LANES), lambda i: (i, 0)),
                  pl.BlockSpec((tm, d), lambda i: (i, 0)),
                  pl.BlockSpec((1, 1, d), lambda i: (i // per_b, 0, 0)),
                  pl.BlockSpec((1, d), lambda i: (0, 0))],
        out_specs=pl.BlockSpec((tm, d), lambda i: (i, 0)),
        scratch_shapes=[pltpu.VMEM((TOP_K_EXPERTS, tm, d), F32), pltpu.SemaphoreType.DMA(())],
        compiler_params=pltpu.CompilerParams(vmem_limit_bytes=VMEM_LIMIT),
        name="combine",
    )(rows_flat, y_rows, gates, x1, g2, fg_row)


def kernel(x, c, positions, ada_w, ada_b, norm1_g, w_in, hg_norm_g, lb_logits, w_out, norm2_g,
           router_w, router_b, moe_w1, moe_b1, moe_w2, moe_b2, final_g):
    batch, seq, d = x.shape
    t = batch * seq
    layer = 0
    x2 = x.reshape(t, d)

    c_pad = jnp.zeros((SUBLANES, d), F32).at[:batch].set(c)
    mod = _adaln(c_pad, ada_w[layer], ada_b[layer][None, :])[:batch]
    shift1, scale1, gate1, shift2, scale2, gate2 = jnp.split(mod, 6, axis=-1)
    row3 = lambda m: m[:, None, :]
    col3 = lambda m: m[:, :, None]

    inv_freq = ROPE_THETA ** (-(jnp.arange(0, ROT_DIM, 2, dtype=F32) / ROT_DIM))
    cos_t, sin_t = _trig(positions.reshape(1, t).astype(F32), inv_freq[:, None])
    ones = jnp.ones((t, A_HEAD_DIM - ROT_DIM), F32)
    c64 = jnp.concatenate([cos_t.T, cos_t.T, ones], axis=1)
    s64 = jnp.concatenate([-sin_t.T, sin_t.T, 0.0 * ones], axis=1)
    c_tab = jnp.concatenate([c64, c64], axis=1)
    s_tab = jnp.concatenate([s64, s64], axis=1)

    wl = w_in[layer]
    sp = np.cumsum((A_WIDTH, A_WIDTH, A_WIDTH, IDX_HEADS * IDX_DIM, IDX_DIM, IDX_HEADS,
                    B_HEADS * B_KEY_DIM, B_HEADS * B_KEY_DIM, B_WIDTH))
    w_qa, w_ka, w_va, w_qi, w_ki, w_wi, w_qb, w_fb, w_ib, w_gb = jnp.split(wl, [int(v) for v in sp], axis=1)
    w_t = jnp.concatenate([w_qa, w_va, w_qi, w_wi, jnp.zeros((d, SUBLANES - IDX_HEADS), F32)], axis=1)
    w_t = w_t.T.astype(BF16)
    w_r = jnp.concatenate([w_ka, w_ki, jnp.zeros((d, LANES - IDX_DIM), F32), w_qb, w_fb, w_ib, w_gb],
                          axis=1).astype(BF16)

    g1n = norm1_g[layer]
    qa_t, v_blk, qi_t, wi_t = _inproj_t(x2.T, g1n[:, None], col3(scale1), col3(shift1),
                                        w_t, cos_t, sin_t, seq)
    ka, ki, qb, fb, ib, gb = _inproj_r(x2, g1n[None, :], row3(scale1), row3(shift1),
                                       w_r, c_tab, s_tab, seq)

    out_a = _dsa(qi_t, wi_t, ki, qa_t, ka, v_blk, batch, seq)

    lower = jnp.cumsum(jax.nn.softmax(lb_logits.astype(F32), axis=0), axis=0)[layer]
    out_b = _hgrn(qb, fb, ib, gb, lower[None, :], hg_norm_g[layer][None, :], batch, seq)

    rw = jnp.concatenate([router_w[layer], jnp.zeros((d, LANES - N_EXPERTS), F32)], axis=1)
    rb = jnp.concatenate([router_b[layer], jnp.full((LANES - N_EXPERTS,), NEG_BIG, F32)])[None, :]
    x1, h2, idx, gates, rank, counts = _outproj(
        out_a, out_b, x2, row3(gate1), w_out[layer].astype(BF16), norm2_g[layer][None, :],
        row3(scale2), row3(shift2), rw, rb, seq)

    counts = counts[0, :N_EXPERTS]
    padded = (counts + EXPERT_BLOCK - 1) // EXPERT_BLOCK * EXPERT_BLOCK
    pad_end = jnp.cumsum(padded)
    pad_start = pad_end - padded
    n_assign = t * TOP_K_EXPERTS
    n_blocks = -(-n_assign // EXPERT_BLOCK) + N_EXPERTS
    n_rows = n_blocks * EXPERT_BLOCK
    block_start = jnp.arange(n_blocks, dtype=I32) * EXPERT_BLOCK
    block_expert = jnp.minimum(jnp.searchsorted(pad_end, block_start, side='right'),
                               N_EXPERTS - 1).astype(I32)
    n_used = (pad_end[-1:] // EXPERT_BLOCK).astype(I32)
    rows_flat = (pad_start[idx[:, :TOP_K_EXPERTS]] + rank[:, :TOP_K_EXPERTS]).astype(I32).reshape(-1)

    x_rows = _dispatch(rows_flat, h2, n_rows)
    y_rows = _experts(block_expert, n_used, x_rows, moe_w1[layer], moe_b1[layer],
                      moe_w2[layer], moe_b2[layer])
    out = _combine(rows_flat, y_rows, gates, x1, row3(gate2), final_g[None, :], seq)
    return out.reshape(batch, seq, d)
```

```python
import functools

import numpy as np
import jax
import jax.numpy as jnp
from jax import lax
from jax.experimental import pallas as pl
from jax.experimental.pallas import tpu as pltpu

F32 = jnp.float32
BF16 = jnp.bfloat16
I32 = jnp.int32
HIGHEST = lax.Precision.HIGHEST

D_MODEL = 1024
CHUNK = 64
A_HEADS = 8
A_HEAD_DIM = 64
A_WIDTH = A_HEADS * A_HEAD_DIM
IDX_HEADS = 4
IDX_DIM = 64
TOPK_MAX = 256
B_HEADS = 4
B_KEY_DIM = 64
B_VAL_DIM = 128
B_WIDTH = B_HEADS * B_VAL_DIM
ROPE_THETA = 500000.0
ROT_DIM = A_HEAD_DIM // 4
ROT_HALF = ROT_DIM // 2
N_EXPERTS = 32
TOP_K_EXPERTS = 4
SWIGLU_LIMIT = 7.0
SWIGLU_ALPHA = 1.702
EXPERT_BLOCK = 256
RMS_EPS = 1e-6

LANES = 128
SUBLANES = 8
VMEM_LIMIT = 56 * 1024 * 1024

NEG_BIG = -1e30
INT_MIN = -(2 ** 31)

DSA_TQ = 256
DSA_KB = 256
HG_STEP_CHUNKS = 8
HG_LEVELS = (32, 16, 8, 4, 2, 1)


def _dot(a, b, precision=None):
    return jnp.dot(a, b, preferred_element_type=F32, precision=precision)


def _dot_nt(a, b, precision=None):
    return lax.dot_general(a, b, (((1,), (1,)), ((), ())),
                           preferred_element_type=F32, precision=precision)


def _sigmoid(x):
    return 1.0 / (1.0 + jnp.exp(-x))


ROW_TILES = D_MODEL // LANES


def _load_token_tiles(ref3):
    return jnp.concatenate([ref3[:, s, :] for s in range(ROW_TILES)], axis=1)


def _store_token_tiles(ref3, val):
    for s in range(ROW_TILES):
        ref3[:, s, :] = val[:, s * LANES:(s + 1) * LANES]


def _adaln_kernel(c_ref, w_ref, b_ref, o_ref):
    c = c_ref[...]
    o_ref[...] = _dot(c * _sigmoid(c), w_ref[...], HIGHEST) + b_ref[...]


def _adaln(c_pad, ada_w, ada_b):
    d = c_pad.shape[1]
    n = ada_w.shape[1]
    bn = 1024
    return pl.pallas_call(
        _adaln_kernel,
        out_shape=jax.ShapeDtypeStruct((c_pad.shape[0], n), F32),
        grid=(n // bn,),
        in_specs=[pl.BlockSpec((c_pad.shape[0], d), lambda j: (0, 0)),
                  pl.BlockSpec((d, bn), lambda j: (0, j)),
                  pl.BlockSpec((1, bn), lambda j: (0, j))],
        out_specs=pl.BlockSpec((c_pad.shape[0], bn), lambda j: (0, j)),
        name="adaln",
    )(c_pad, ada_w, ada_b)


def _trig_kernel(pos_ref, freq_ref, cos_ref, sin_ref):
    ang = pos_ref[...] * freq_ref[...]
    cos_ref[...] = jnp.cos(ang)
    sin_ref[...] = jnp.sin(ang)


def _trig(pos_row, freq_col):
    t = pos_row.shape[1]
    bt = 2048
    return pl.pallas_call(
        _trig_kernel,
        out_shape=(jax.ShapeDtypeStruct((ROT_HALF, t), F32),) * 2,
        grid=(t // bt,),
        in_specs=[pl.BlockSpec((1, bt), lambda i: (0, i)),
                  pl.BlockSpec((ROT_HALF, 1), lambda i: (0, 0))],
        out_specs=(pl.BlockSpec((ROT_HALF, bt), lambda i: (0, i)),) * 2,
        name="trig",
    )(pos_row, freq_col)


def _rope_rows(p, c, s):
    x1 = p[0:ROT_HALF]
    x2 = p[ROT_HALF:ROT_DIM]
    return jnp.concatenate([x1 * c - x2 * s, x2 * c + x1 * s, p[ROT_DIM:]], axis=0)


def _inproj_t_kernel(xt_ref, g_ref, sc_ref, sh_ref, w_ref, cos_ref, sin_ref,
                     qa_ref, v_ref, qi_ref, wt_ref):
    xt = xt_ref[...]
    ms = jnp.mean(xt * xt, axis=0, keepdims=True)
    h = xt * lax.rsqrt(ms + RMS_EPS) * g_ref[...]
    h = h * (1.0 + sc_ref[0]) + sh_ref[0]
    hb = h.astype(BF16)
    c = cos_ref[...]
    s = sin_ref[...]
    pq = _dot(w_ref[0:A_WIDTH, :], hb)
    for hh in range(A_HEADS):
        r = _rope_rows(pq[hh * 64:(hh + 1) * 64], c, s) * (A_HEAD_DIM ** -0.5)
        qa_ref[hh * 64:(hh + 1) * 64, :] = r.astype(BF16)
    pv = _dot(w_ref[A_WIDTH:2 * A_WIDTH, :], hb)
    for cb in range(v_ref.shape[0]):
        v_ref[cb] = pv[:, cb * DSA_KB:(cb + 1) * DSA_KB].astype(BF16)
    pi = _dot(w_ref[2 * A_WIDTH:2 * A_WIDTH + 256, :], hb)
    for hh in range(IDX_HEADS):
        qi_ref[hh * 64:(hh + 1) * 64, :] = _rope_rows(pi[hh * 64:(hh + 1) * 64], c, s) * (IDX_DIM ** -0.5)
    pw = _dot(w_ref[2 * A_WIDTH + 256:2 * A_WIDTH + 264, :], hb)
    wt_ref[...] = pw * (IDX_HEADS ** -0.5)


def _inproj_t(xt, g_col, sc_col, sh_col, w_t, cos_t, sin_t, seq):
    d, t = xt.shape
    tn = 512
    per_b = seq // tn
    rows = w_t.shape[0]
    return pl.pallas_call(
        _inproj_t_kernel,
        out_shape=(jax.ShapeDtypeStruct((A_WIDTH, t), BF16),
                   jax.ShapeDtypeStruct((t // DSA_KB, A_WIDTH, DSA_KB), BF16),
                   jax.ShapeDtypeStruct((IDX_HEADS * IDX_DIM, t), F32),
                   jax.ShapeDtypeStruct((SUBLANES, t), F32)),
        grid=(t // tn,),
        in_specs=[pl.BlockSpec((d, tn), lambda i: (0, i)),
                  pl.BlockSpec((d, 1), lambda i: (0, 0)),
                  pl.BlockSpec((1, d, 1), lambda i: (i // per_b, 0, 0)),
                  pl.BlockSpec((1, d, 1), lambda i: (i // per_b, 0, 0)),
                  pl.BlockSpec((rows, d), lambda i: (0, 0)),
                  pl.BlockSpec((ROT_HALF, tn), lambda i: (0, i)),
                  pl.BlockSpec((ROT_HALF, tn), lambda i: (0, i))],
        out_specs=(pl.BlockSpec((A_WIDTH, tn), lambda i: (0, i)),
                   pl.BlockSpec((tn // DSA_KB, A_WIDTH, DSA_KB), lambda i: (i, 0, 0)),
                   pl.BlockSpec((IDX_HEADS * IDX_DIM, tn), lambda i: (0, i)),
                   pl.BlockSpec((SUBLANES, tn), lambda i: (0, i))),
        compiler_params=pltpu.CompilerParams(vmem_limit_bytes=VMEM_LIMIT),
        name="inproj_t",
    )(xt, g_col, sc_col, sh_col, w_t, cos_t, sin_t)


def _inproj_r_kernel(x_ref, g_ref, sc_ref, sh_ref, w_ref, c_ref, s_ref,
                     ka_ref, ki_ref, qb_ref, fb_ref, ib_ref, gb_ref):
    x = x_ref[...]
    ms = jnp.mean(x * x, axis=-1, keepdims=True)
    h = x * lax.rsqrt(ms + RMS_EPS) * g_ref[...]
    h = h * (1.0 + sc_ref[0]) + sh_ref[0]
    hb = h.astype(BF16)
    c = c_ref[...]
    s = s_ref[...]
    lane = lax.broadcasted_iota(I32, c.shape, 1)
    first = (lane % A_HEAD_DIM) < ROT_HALF

    def rope(p):
        partner = jnp.where(first, pltpu.roll(p, LANES - ROT_HALF, 1), pltpu.roll(p, ROT_HALF, 1))
        return p * c + partner * s

    for j in range(A_WIDTH // LANES):
        p = _dot(hb, w_ref[:, j * LANES:(j + 1) * LANES])
        ka_ref[:, j * LANES:(j + 1) * LANES] = rope(p).astype(BF16)
    o = A_WIDTH
    ki_ref[...] = rope(_dot(hb, w_ref[:, o:o + LANES]))
    o += LANES
    qb_ref[...] = _dot(hb, w_ref[:, o:o + 256])
    o += 256
    fb_ref[...] = _dot(hb, w_ref[:, o:o + 256])
    o += 256
    ib_ref[...] = _dot(hb, w_ref[:, o:o + 512])
    o += 512
    gb_ref[...] = _dot(hb, w_ref[:, o:o + 512])


def _inproj_r(x2, g_row, sc_row, sh_row, w_r, c_tab, s_tab, seq):
    t, d = x2.shape
    tm = 512
    per_b = seq // tm
    cols = w_r.shape[1]
    widths = (A_WIDTH, LANES, 256, 256, 512, 512)
    dts = (BF16, F32, F32, F32, F32, F32)
    return pl.pallas_call(
        _inproj_r_kernel,
        out_shape=tuple(jax.ShapeDtypeStruct((t, w), dt) for w, dt in zip(widths, dts)),
        grid=(t // tm,),
        in_specs=[pl.BlockSpec((tm, d), lambda i: (i, 0)),
                  pl.BlockSpec((1, d), lambda i: (0, 0)),
                  pl.BlockSpec((1, 1, d), lambda i: (i // per_b, 0, 0)),
                  pl.BlockSpec((1, 1, d), lambda i: (i // per_b, 0, 0)),
                  pl.BlockSpec((d, cols), lambda i: (0, 0)),
                  pl.BlockSpec((tm, LANES), lambda i: (i, 0)),
                  pl.BlockSpec((tm, LANES), lambda i: (i, 0))],
        out_specs=tuple(pl.BlockSpec((tm, w), lambda i: (i, 0)) for w in widths),
        compiler_params=pltpu.CompilerParams(vmem_limit_bytes=VMEM_LIMIT),
        name="inproj_r",
    )(x2, g_row, sc_row, sh_row, w_r, c_tab, s_tab)


def _dsa_kernel(qi_ref, wt_ref, ki_ref, qa_ref, ka_ref, v_ref, o_ref,
                keys_ref, m_ref, l_ref, acc_ref):
    j = pl.program_id(1)
    nkb = j + 1
    tq = DSA_TQ
    kb_rows = DSA_KB
    row = lax.broadcasted_iota(I32, (kb_rows, tq), 0)
    col = lax.broadcasted_iota(I32, (kb_rows, tq), 1)
    q_chunk = (j * tq + col) // CHUNK

    def score_block(kb, carry):
        r0 = pl.multiple_of(kb * kb_rows, kb_rows)
        ki = ki_ref[pl.ds(r0, kb_rows), 0:IDX_DIM]
        sc = jnp.zeros((kb_rows, tq), F32)
        for h in range(IDX_HEADS):
            lg = _dot(ki, qi_ref[h * IDX_DIM:(h + 1) * IDX_DIM, :], HIGHEST)
            sc = sc + wt_ref[h:h + 1, :] * jnp.maximum(lg, 0.0)
        sc = jnp.where(sc == 0.0, 0.0, sc)
        bits = pltpu.bitcast(sc, I32)
        key = jnp.where(bits < 0, bits ^ 0x7FFFFFFF, bits)
        adm = ((kb * kb_rows + row) // CHUNK) <= q_chunk
        keys_ref[kb] = jnp.where(adm, key, INT_MIN)
        return carry

    lax.fori_loop(0, nkb, score_block, 0)

    def count(pred_fn):
        def body(kb, acc):
            return acc + jnp.sum(jnp.where(pred_fn(keys_ref[kb], kb), 1, 0), axis=0, keepdims=True)
        return lax.fori_loop(0, nkb, body, jnp.zeros((1, tq), I32))

    def bit_step(i, res):
        cand = res ^ lax.shift_left(jnp.int32(1), 31 - i)
        cnt = count(lambda k, kb: k >= cand)
        return jnp.where(cnt >= TOPK_MAX, cand, res)

    thr = lax.fori_loop(0, 32, bit_step, jnp.full((1, tq), INT_MIN, I32))
    n_gt = count(lambda k, kb: k > thr)
    n_eq = count(lambda k, kb: k == thr)
    need = TOPK_MAX - n_gt
    live = thr != INT_MIN
    excess = jnp.logical_and(n_gt + n_eq > TOPK_MAX, live)
    any_excess = jnp.max(jnp.where(excess, 1, 0)) > 0

    def idx_search():
        def step(i, p):
            cand = p | lax.shift_left(jnp.int32(1), 11 - i)
            cnt = count(lambda k, kb: jnp.logical_and(k == thr, kb * kb_rows + row < cand))
            return jnp.where(cnt < need, cand, p)
        return lax.fori_loop(0, 12, step, jnp.zeros((1, tq), I32))

    p_full = jnp.full((1, tq), 4095, I32)
    p_idx = lax.cond(any_excess, idx_search, lambda: p_full)
    p_idx = jnp.where(live, p_idx, -1)

    m_ref[...] = jnp.full(m_ref.shape, NEG_BIG, F32)
    l_ref[...] = jnp.zeros(l_ref.shape, F32)
    acc_ref[...] = jnp.zeros(acc_ref.shape, F32)

    def attn_block(kb, carry):
        r0 = pl.multiple_of(kb * kb_rows, kb_rows)
        key = keys_ref[kb]
        sel = jnp.logical_or(key > thr,
                             jnp.logical_and(key == thr, kb * kb_rows + row <= p_idx))
        bias = jnp.where(sel, 0.0, NEG_BIG)
        for h in range(A_HEADS):
            hs = slice(h * A_HEAD_DIM, (h + 1) * A_HEAD_DIM)
            s = _dot(ka_ref[pl.ds(r0, kb_rows), hs], qa_ref[hs, :]) + bias
            m_old = m_ref[h:h + 1, :]
            m_new = jnp.maximum(m_old, jnp.max(s, axis=0, keepdims=True))
            alpha = jnp.exp(m_old - m_new)
            p = jnp.exp(s - m_new)
            l_ref[h:h + 1, :] = alpha * l_ref[h:h + 1, :] + jnp.sum(p, axis=0, keepdims=True)
            acc_ref[hs, :] = alpha * acc_ref[hs, :] + _dot(v_ref[kb, hs, :], p.astype(BF16))
            m_ref[h:h + 1, :] = m_new
        return carry

    lax.fori_loop(0, nkb, attn_block, 0)

    for h in range(A_HEADS):
        hs = slice(h * A_HEAD_DIM, (h + 1) * A_HEAD_DIM)
        acc_ref[hs, :] = acc_ref[hs, :] / l_ref[h:h + 1, :]
    o_ref[...] = acc_ref[...].T


def _dsa(qi_t, w_t, ki, qa_t, ka, v_blk, batch, seq):
    t = ka.shape[0]
    nqb = seq // DSA_TQ
    nkb = seq // DSA_KB
    return pl.pallas_call(
        _dsa_kernel,
        out_shape=jax.ShapeDtypeStruct((t, A_WIDTH), F32),
        grid=(batch, nqb),
        in_specs=[pl.BlockSpec((IDX_HEADS * IDX_DIM, DSA_TQ), lambda b, j: (0, b * nqb + j)),
                  pl.BlockSpec((SUBLANES, DSA_TQ), lambda b, j: (0, b * nqb + j)),
                  pl.BlockSpec((seq, LANES), lambda b, j: (b, 0)),
                  pl.BlockSpec((A_WIDTH, DSA_TQ), lambda b, j: (0, b * nqb + j)),
                  pl.BlockSpec((seq, A_WIDTH), lambda b, j: (b, 0)),
                  pl.BlockSpec((nkb, A_WIDTH, DSA_KB), lambda b, j: (b, 0, 0))],
        out_specs=pl.BlockSpec((DSA_TQ, A_WIDTH), lambda b, j: (b * nqb + j, 0)),
        scratch_shapes=[pltpu.VMEM((nkb, DSA_KB, DSA_TQ), I32),
                        pltpu.VMEM((A_HEADS, DSA_TQ), F32),
                        pltpu.VMEM((A_HEADS, DSA_TQ), F32),
                        pltpu.VMEM((A_WIDTH, DSA_TQ), F32)],
        compiler_params=pltpu.CompilerParams(vmem_limit_bytes=VMEM_LIMIT),
        name="dsa",
    )(qi_t, w_t, ki, qa_t, ka, v_blk)


def _hgrn_constants():
    c = CHUNK
    t = np.arange(c)[:, None]
    u = np.arange(c)[None, :]
    blocks = [(u <= t), (u > t)]
    masks = []
    for half in HG_LEVELS:
        mid = (t // (2 * half)) * (2 * half) + half - 1
        right = ((t // half) % 2) == 1
        blocks.append(right & (u > mid) & (u <= t))
        blocks.append((~right) & (u > t) & (u <= mid))
        tt, ss = t, u
        masks.append(((tt // (2 * half)) == (ss // (2 * half)))
                     & ((((tt // half) % 2) == 1) & (((ss // half) % 2) == 0)))
    masks.append(t == u)
    cm = np.concatenate(blocks, axis=0).astype(np.float32)
    cm3 = np.concatenate([cm, cm, cm], axis=1)
    return cm3, np.stack(masks).astype(np.float32)


def _hgrn_kernel(qb_ref, fb_ref, ib_ref, gb_ref, lb_ref, hg_ref, cm_ref, mask_ref, eye_ref,
                 o_ref, state_ref):
    @pl.when(pl.program_id(1) == 0)
    def _():
        state_ref[...] = jnp.zeros(state_ref.shape, F32)

    lb = lb_ref[...]
    c = CHUNK
    nlev = len(HG_LEVELS)

    def chunk_step(ci, carry):
        r0 = pl.multiple_of(ci * c, c)
        f = lb + (1.0 - lb) * _sigmoid(fb_ref[pl.ds(r0, c), :])
        lf = jnp.log(f)
        kin = 1.0 - f
        hi = lf.astype(BF16)
        r1 = lf - hi.astype(F32)
        mid = r1.astype(BF16)
        lo = (r1 - mid.astype(F32)).astype(BF16)
        e = jnp.exp(_dot(cm_ref[...], jnp.concatenate([hi, mid, lo], axis=0)))
        q = qb_ref[pl.ds(r0, c), :]
        q_in = (q * e[0:c]).astype(BF16)
        k_out = (kin * e[c:2 * c]).astype(BF16)
        e_last = e[c - 1:c, :]
        q_lv = [(q * e[(2 + 2 * i) * c:(3 + 2 * i) * c]).astype(BF16) for i in range(nlev)]
        k_lv = [(kin * e[(3 + 2 * i) * c:(4 + 2 * i) * c]).astype(BF16) for i in range(nlev)]
        q_lv.append(q.astype(BF16))
        k_lv.append(kin.astype(BF16))
        for h in range(B_HEADS):
            ks = slice(h * B_KEY_DIM, (h + 1) * B_KEY_DIM)
            vs = slice(h * B_VAL_DIM, (h + 1) * B_VAL_DIM)
            attn = jnp.zeros((c, c), F32)
            for i in range(nlev + 1):
                attn = attn + mask_ref[i] * _dot_nt(q_lv[i][:, ks], k_lv[i][:, ks])
            v = ib_ref[pl.ds(r0, c), vs]
            vb = v.astype(BF16)
            st = state_ref[h]
            o = _dot_nt(q_in[:, ks], st.astype(BF16)) + _dot(attn.astype(BF16), vb)
            v_t = _dot_nt(eye_ref[...], vb).astype(BF16)
            state_ref[h] = st * e_last[:, ks] + _dot(v_t, k_out[:, ks])
            ms = jnp.mean(o * o, axis=-1, keepdims=True)
            y = o * lax.rsqrt(ms + RMS_EPS) * hg_ref[...]
            g = gb_ref[pl.ds(r0, c), vs]
            o_ref[pl.ds(r0, c), vs] = y * (g * _sigmoid(g))
        return carry

    lax.fori_loop(0, HG_STEP_CHUNKS, chunk_step, 0)


def _hgrn(qb, fb, ib, gb, lb_row, hg_row, batch, seq):
    t = qb.shape[0]
    tm = HG_STEP_CHUNKS * CHUNK
    per_b = seq // tm
    cm3, masks = _hgrn_constants()
    cm3 = jnp.asarray(cm3, BF16)
    masks = jnp.asarray(masks, F32)
    eye = jnp.eye(B_VAL_DIM, dtype=BF16)
    return pl.pallas_call(
        _hgrn_kernel,
        out_shape=jax.ShapeDtypeStruct((t, B_WIDTH), F32),
        grid=(batch, per_b),
        in_specs=[pl.BlockSpec((tm, 256), lambda b, i: (b * per_b + i, 0)),
                  pl.BlockSpec((tm, 256), lambda b, i: (b * per_b + i, 0)),
                  pl.BlockSpec((tm, B_WIDTH), lambda b, i: (b * per_b + i, 0)),
                  pl.BlockSpec((tm, B_WIDTH), lambda b, i: (b * per_b + i, 0)),
                  pl.BlockSpec((1, 256), lambda b, i: (0, 0)),
                  pl.BlockSpec((1, B_VAL_DIM), lambda b, i: (0, 0)),
                  pl.BlockSpec(cm3.shape, lambda b, i: (0, 0)),
                  pl.BlockSpec(masks.shape, lambda b, i: (0, 0, 0)),
                  pl.BlockSpec(eye.shape, lambda b, i: (0, 0))],
        out_specs=pl.BlockSpec((tm, B_WIDTH), lambda b, i: (b * per_b + i, 0)),
        scratch_shapes=[pltpu.VMEM((B_HEADS, B_VAL_DIM, B_KEY_DIM), F32)],
        compiler_params=pltpu.CompilerParams(vmem_limit_bytes=VMEM_LIMIT),
        name="hgrn",
    )(qb, fb, ib, gb, lb_row, hg_row, cm3, masks, eye)


def _outproj_kernel(oa_ref, ob_ref, x_ref, g1_ref, wo_ref, n2_ref, sc_ref, sh_ref,
                    rw_ref, rb_ref, tri_ref,
                    x1_ref, h2_ref, idx_ref, gate_ref, rank_ref, cnt_ref, run_ref):
    @pl.when(pl.program_id(0) == 0)
    def _():
        run_ref[...] = jnp.zeros(run_ref.shape, F32)

    half = A_WIDTH
    mix = _dot(oa_ref[...].astype(BF16), wo_ref[0:half, :]) + _dot(ob_ref[...].astype(BF16), wo_ref[half:, :])
    x1 = x_ref[...] + g1_ref[0] * mix
    x1_ref[...] = x1
    ms = jnp.mean(x1 * x1, axis=-1, keepdims=True)
    h2 = x1 * lax.rsqrt(ms + RMS_EPS) * n2_ref[...]
    h2 = h2 * (1.0 + sc_ref[0]) + sh_ref[0]
    _store_token_tiles(h2_ref, h2)
    logits = _dot(h2, rw_ref[...], HIGHEST) + rb_ref[...]
    tm = logits.shape[0]
    lane = lax.broadcasted_iota(I32, (tm, LANES), 1)
    work = logits
    vals, idxs = [], []
    for _ in range(TOP_K_EXPERTS):
        m = jnp.max(work, axis=-1, keepdims=True)
        ix = jnp.min(jnp.where(work == m, lane, LANES), axis=-1, keepdims=True)
        vals.append(m)
        idxs.append(ix)
        work = jnp.where(lane == ix, -jnp.inf, work)
    es = [jnp.exp(v - vals[0]) for v in vals]
    tot = es[0] + es[1] + es[2] + es[3]
    onehot = jnp.zeros((tm, LANES), F32)
    idx_out = jnp.zeros((tm, LANES), I32)
    gate_out = jnp.zeros((tm, LANES), F32)
    for k in range(TOP_K_EXPERTS):
        onehot = onehot + jnp.where(lane == idxs[k], 1.0, 0.0)
        idx_out = jnp.where(lane == k, idxs[k], idx_out)
        gate_out = jnp.where(lane == k, es[k] / tot, gate_out)
    idx_ref[...] = idx_out
    gate_ref[...] = gate_out
    before = _dot(tri_ref[...], onehot.astype(BF16)) + run_ref[...]
    rank_out = jnp.zeros((tm, LANES), F32)
    for k in range(TOP_K_EXPERTS):
        rk = jnp.sum(jnp.where(lane == idxs[k], before, 0.0), axis=-1, keepdims=True)
        rank_out = jnp.where(lane == k, rk, rank_out)
    rank_ref[...] = rank_out.astype(I32)
    run = run_ref[...] + jnp.sum(onehot, axis=0, keepdims=True)
    run_ref[...] = run
    cnt_ref[...] = run.astype(I32)


def _outproj(oa, ob, x2, g1, wo, n2_row, sc2, sh2, rw, rb, seq):
    t, d = x2.shape
    tm = 512
    per_b = seq // tm
    tri = jnp.asarray(np.tril(np.ones((tm, tm), np.float32), -1), BF16)
    row_spec = lambda w: pl.BlockSpec((tm, w), lambda i: (i, 0))
    mod_spec = pl.BlockSpec((1, 1, d), lambda i: (i // per_b, 0, 0))
    full = lambda a: pl.BlockSpec(a.shape, lambda i: (0,) * a.ndim)
    return pl.pallas_call(
        _outproj_kernel,
        out_shape=(jax.ShapeDtypeStruct((t, d), F32), jax.ShapeDtypeStruct((t, ROW_TILES, LANES), F32),
                   jax.ShapeDtypeStruct((t, LANES), I32), jax.ShapeDtypeStruct((t, LANES), F32),
                   jax.ShapeDtypeStruct((t, LANES), I32), jax.ShapeDtypeStruct((1, LANES), I32)),
        grid=(t // tm,),
        in_specs=[row_spec(A_WIDTH), row_spec(B_WIDTH), row_spec(d), mod_spec, full(wo),
                  full(n2_row), mod_spec, mod_spec, full(rw), full(rb), full(tri)],
        out_specs=(row_spec(d), pl.BlockSpec((tm, ROW_TILES, LANES), lambda i: (i, 0, 0)),
                   row_spec(LANES), row_spec(LANES), row_spec(LANES),
                   pl.BlockSpec((1, LANES), lambda i: (0, 0))),
        scratch_shapes=[pltpu.VMEM((1, LANES), F32)],
        compiler_params=pltpu.CompilerParams(dimension_semantics=("arbitrary",),
                                             vmem_limit_bytes=VMEM_LIMIT),
        name="outproj",
    )(oa, ob, x2, g1, wo, n2_row, sc2, sh2, rw, rb, tri)


DISPATCH_TOKENS = 256
WAIT_UNROLL = 16


def _drain(make_copy, n):
    def body(g, carry):
        for _ in range(WAIT_UNROLL):
            make_copy().wait()
        return carry

    lax.fori_loop(0, n // WAIT_UNROLL, body, 0)


def _dispatch_kernel(rows_ref, h2_ref, zero_hbm, out_hbm, sem):
    del zero_hbm
    n = DISPATCH_TOKENS * TOP_K_EXPERTS

    def issue(a, carry):
        pltpu.make_async_copy(h2_ref.at[a // TOP_K_EXPERTS], out_hbm.at[rows_ref[a]], sem).start()
        return carry

    lax.fori_loop(0, n, issue, 0)
    _drain(lambda: pltpu.make_async_copy(h2_ref.at[0], out_hbm.at[0], sem), n)


def _dispatch(rows_flat, h2, n_rows):
    t = h2.shape[0]
    n = DISPATCH_TOKENS * TOP_K_EXPERTS
    zeros = jnp.zeros((n_rows, ROW_TILES, LANES), F32)
    return pl.pallas_call(
        _dispatch_kernel,
        out_shape=jax.ShapeDtypeStruct((n_rows, ROW_TILES, LANES), F32),
        grid=(t // DISPATCH_TOKENS,),
        in_specs=[pl.BlockSpec((n,), lambda i: (i,), memory_space=pltpu.SMEM),
                  pl.BlockSpec((DISPATCH_TOKENS, ROW_TILES, LANES), lambda i: (i, 0, 0)),
                  pl.BlockSpec(memory_space=pl.ANY)],
        out_specs=pl.BlockSpec(memory_space=pl.ANY),
        scratch_shapes=[pltpu.SemaphoreType.DMA(())],
        input_output_aliases={2: 0},
        name="dispatch",
    )(rows_flat, h2, zeros)


def _experts_kernel(be_ref, nu_ref, x_ref, w1_ref, b1_ref, w2_ref, b2_ref, y_ref, w1b_ref, w2b_ref):
    i = pl.program_id(0)
    prev = be_ref[jnp.maximum(i - 1, 0)]
    changed = jnp.logical_or(i == 0, be_ref[i] != prev)
    used = i < nu_ref[0]

    @pl.when(jnp.logical_and(changed, used))
    def _():
        w1b_ref[...] = w1_ref[0].astype(BF16)
        w2b_ref[...] = w2_ref[0].astype(BF16)

    @pl.when(used)
    def _():
        f = w2b_ref.shape[0]
        hg = _dot(_load_token_tiles(x_ref).astype(BF16), w1b_ref[...]) + b1_ref[0]
        glu = jnp.minimum(hg[:, :f], SWIGLU_LIMIT)
        lin = jnp.clip(hg[:, f:], -SWIGLU_LIMIT, SWIGLU_LIMIT)
        act = glu * _sigmoid(SWIGLU_ALPHA * glu) * (lin + 1.0)
        _store_token_tiles(y_ref, _dot(act.astype(BF16), w2b_ref[...]) + b2_ref[0])

    @pl.when(jnp.logical_not(used))
    def _():
        y_ref[...] = jnp.zeros(y_ref.shape, F32)


def _experts(block_expert, n_used, x_rows, w1, b1, w2, b2):
    n_rows = x_rows.shape[0]
    e, d, f2 = w1.shape
    f = w2.shape[1]
    nb = n_rows // EXPERT_BLOCK
    tok_spec = pl.BlockSpec((EXPERT_BLOCK, ROW_TILES, LANES), lambda i, be, nu: (i, 0, 0))
    return pl.pallas_call(
        _experts_kernel,
        out_shape=jax.ShapeDtypeStruct((n_rows, ROW_TILES, LANES), F32),
        grid_spec=pltpu.PrefetchScalarGridSpec(
            num_scalar_prefetch=2,
            grid=(nb,),
            in_specs=[tok_spec,
                      pl.BlockSpec((1, d, f2), lambda i, be, nu: (be[i], 0, 0)),
                      pl.BlockSpec((1, 1, f2), lambda i, be, nu: (be[i], 0, 0)),
                      pl.BlockSpec((1, f, d), lambda i, be, nu: (be[i], 0, 0)),
                      pl.BlockSpec((1, 1, d), lambda i, be, nu: (be[i], 0, 0))],
            out_specs=tok_spec,
            scratch_shapes=[pltpu.VMEM((d, f2), BF16), pltpu.VMEM((f, d), BF16)]),
        compiler_params=pltpu.CompilerParams(dimension_semantics=("arbitrary",),
                                             vmem_limit_bytes=VMEM_LIMIT),
        name="experts",
    )(block_expert, n_used, x_rows, w1, b1.reshape(e, 1, f2), w2, b2.reshape(e, 1, d))


COMBINE_TOKENS = 256


def _combine_kernel(rows_ref, y_hbm, gate_ref, x1_ref, g2_ref, fg_ref, o_ref, buf_ref, sem):
    n = COMBINE_TOKENS * TOP_K_EXPERTS

    def issue(a, carry):
        pltpu.make_async_copy(y_hbm.at[rows_ref[a]],
                              buf_ref.at[a % TOP_K_EXPERTS, a // TOP_K_EXPERTS], sem).start()
        return carry

    lax.fori_loop(0, n, issue, 0)
    _drain(lambda: pltpu.make_async_copy(y_hbm.at[0], buf_ref.at[0, 0], sem), n)

    gates = gate_ref[...]
    moe = gates[:, 0:1] * _load_token_tiles(buf_ref.at[0])
    for k in range(1, TOP_K_EXPERTS):
        moe = moe + gates[:, k:k + 1] * _load_token_tiles(buf_ref.at[k])
    x2 = x1_ref[...] + g2_ref[0] * moe
    ms = jnp.mean(x2 * x2, axis=-1, keepdims=True)
    o_ref[...] = x2 * lax.rsqrt(ms + RMS_EPS) * fg_ref[...]


def _combine(rows_flat, y_rows, gates, x1, g2, fg_row, seq):
    t, d = x1.shape
    tm = COMBINE_TOKENS
    per_b = seq // tm
    n = tm * TOP_K_EXPERTS
    return pl.pallas_call(
        _combine_kernel,
        out_shape=jax.ShapeDtypeStruct((t, d), F32),
        grid=(t // tm,),
        in_specs=[pl.BlockSpec((n,), lambda i: (i,), memory_space=pltpu.SMEM),
                  pl.BlockSpec(memory_space=pl.ANY),
                  pl.BlockSpec((tm, LANES), lambda i: (i, 0)),
                  pl.BlockSpec((tm, d), lambda i: (i, 0)),
                  pl.BlockSpec((1, 1, d), lambda i: (i // per_b, 0, 0)),
                  pl.BlockSpec((1, d), lambda i: (0, 0))],
        out_specs=pl.BlockSpec((tm, d), lambda i: (i, 0)),
        scratch_shapes=[pltpu.VMEM((TOP_K_EXPERTS, tm, ROW_TILES, LANES), F32), pltpu.SemaphoreType.DMA(())],
        compiler_params=pltpu.CompilerParams(vmem_limit_bytes=VMEM_LIMIT),
        name="combine",
    )(rows_flat, y_rows, gates, x1, g2, fg_row)


def kernel(x, c, positions, ada_w, ada_b, norm1_g, w_in, hg_norm_g, lb_logits, w_out, norm2_g,
           router_w, router_b, moe_w1, moe_b1, moe_w2, moe_b2, final_g):
    batch, seq, d = x.shape
    t = batch * seq
    layer = 0
    x2 = x.reshape(t, d)

    c_pad = jnp.concatenate([c, jnp.zeros((SUBLANES - batch, d), F32)], axis=0)
    mod = _adaln(c_pad, ada_w[layer], ada_b[layer][None, :])[:batch]
    shift1, scale1, gate1, shift2, scale2, gate2 = jnp.split(mod, 6, axis=-1)
    row3 = lambda m: m[:, None, :]
    col3 = lambda m: m[:, :, None]

    inv_freq = ROPE_THETA ** (-(jnp.arange(0, ROT_DIM, 2, dtype=F32) / ROT_DIM))
    cos_t, sin_t = _trig(positions.reshape(1, t).astype(F32), inv_freq[:, None])
    ones = jnp.ones((t, A_HEAD_DIM - ROT_DIM), F32)
    c64 = jnp.concatenate([cos_t.T, cos_t.T, ones], axis=1)
    s64 = jnp.concatenate([-sin_t.T, sin_t.T, 0.0 * ones], axis=1)
    c_tab = jnp.concatenate([c64, c64], axis=1)
    s_tab = jnp.concatenate([s64, s64], axis=1)

    wl = w_in[layer]
    sp = np.cumsum((A_WIDTH, A_WIDTH, A_WIDTH, IDX_HEADS * IDX_DIM, IDX_DIM, IDX_HEADS,
                    B_HEADS * B_KEY_DIM, B_HEADS * B_KEY_DIM, B_WIDTH))
    w_qa, w_ka, w_va, w_qi, w_ki, w_wi, w_qb, w_fb, w_ib, w_gb = jnp.split(wl, [int(v) for v in sp], axis=1)
    w_t = jnp.concatenate([w_qa, w_va, w_qi, w_wi, jnp.zeros((d, SUBLANES - IDX_HEADS), F32)], axis=1)
    w_t = w_t.T.astype(BF16)
    w_r = jnp.concatenate([w_ka, w_ki, jnp.zeros((d, LANES - IDX_DIM), F32), w_qb, w_fb, w_ib, w_gb],
                          axis=1).astype(BF16)

    g1n = norm1_g[layer]
    qa_t, v_blk, qi_t, wi_t = _inproj_t(x2.T, g1n[:, None], col3(scale1), col3(shift1),
                                        w_t, cos_t, sin_t, seq)
    ka, ki, qb, fb, ib, gb = _inproj_r(x2, g1n[None, :], row3(scale1), row3(shift1),
                                       w_r, c_tab, s_tab, seq)

    out_a = _dsa(qi_t, wi_t, ki, qa_t, ka, v_blk, batch, seq)

    lower = jnp.cumsum(jax.nn.softmax(lb_logits.astype(F32), axis=0), axis=0)[layer]
    out_b = _hgrn(qb, fb, ib, gb, lower[None, :], hg_norm_g[layer][None, :], batch, seq)

    rw = jnp.concatenate([router_w[layer], jnp.zeros((d, LANES - N_EXPERTS), F32)], axis=1)
    rb = jnp.concatenate([router_b[layer], jnp.full((LANES - N_EXPERTS,), NEG_BIG, F32)])[None, :]
    x1, h2, idx, gates, rank, counts = _outproj(
        out_a, out_b, x2, row3(gate1), w_out[layer].astype(BF16), norm2_g[layer][None, :],
        row3(scale2), row3(shift2), rw, rb, seq)

    counts = counts[0, :N_EXPERTS]
    padded = (counts + EXPERT_BLOCK - 1) // EXPERT_BLOCK * EXPERT_BLOCK
    pad_end = jnp.cumsum(padded)
    pad_start = pad_end - padded
    n_assign = t * TOP_K_EXPERTS
    n_blocks = -(-n_assign // EXPERT_BLOCK) + N_EXPERTS
    n_rows = n_blocks * EXPERT_BLOCK
    block_start = jnp.arange(n_blocks, dtype=I32) * EXPERT_BLOCK
    block_expert = jnp.minimum(jnp.sum(pad_end[None, :] <= block_start[:, None], axis=1),
                               N_EXPERTS - 1).astype(I32)
    n_used = (pad_end[-1:] // EXPERT_BLOCK).astype(I32)
    rows_flat = (pad_start[idx[:, :TOP_K_EXPERTS]] + rank[:, :TOP_K_EXPERTS]).astype(I32).reshape(-1)

    x_rows = _dispatch(rows_flat, h2, n_rows)
    y_rows = _experts(block_expert, n_used, x_rows, moe_w1[layer], moe_b1[layer],
                      moe_w2[layer], moe_b2[layer])
    out = _combine(rows_flat, y_rows, gates, x1, row3(gate2), final_g[None, :], seq)
    return out.reshape(batch, seq, d)
```

```python
import functools

import numpy as np
import jax
import jax.numpy as jnp
from jax import lax
from jax.experimental import pallas as pl
from jax.experimental.pallas import tpu as pltpu

F32 = jnp.float32
BF16 = jnp.bfloat16
I32 = jnp.int32
HIGHEST = lax.Precision.HIGHEST

D_MODEL = 1024
CHUNK = 64
A_HEADS = 8
A_HEAD_DIM = 64
A_WIDTH = A_HEADS * A_HEAD_DIM
IDX_HEADS = 4
IDX_DIM = 64
TOPK_MAX = 256
B_HEADS = 4
B_KEY_DIM = 64
B_VAL_DIM = 128
B_WIDTH = B_HEADS * B_VAL_DIM
ROPE_THETA = 500000.0
ROT_DIM = A_HEAD_DIM // 4
ROT_HALF = ROT_DIM // 2
N_EXPERTS = 32
TOP_K_EXPERTS = 4
SWIGLU_LIMIT = 7.0
SWIGLU_ALPHA = 1.702
EXPERT_BLOCK = 256
RMS_EPS = 1e-6

LANES = 128
SUBLANES = 8
VMEM_LIMIT = 56 * 1024 * 1024

NEG_BIG = -1e30
INT_MIN = -(2 ** 31)
LOG2E = 1.4426950408889634
IDX_K = 4 * IDX_DIM

DSA_TQ = 256
DSA_KB = 256
HG_STEP_CHUNKS = 8
HG_LEVELS = (32, 16, 8, 4, 2, 1)


def _dot(a, b, precision=None):
    return jnp.dot(a, b, preferred_element_type=F32, precision=precision)


def _dot_nt(a, b, precision=None):
    return lax.dot_general(a, b, (((1,), (1,)), ((), ())),
                           preferred_element_type=F32, precision=precision)


def _sigmoid(x):
    return 1.0 / (1.0 + jnp.exp(-x))


def _split_bf16(x):
    hi = x.astype(BF16).astype(F32)
    lo = (x - hi).astype(BF16).astype(F32)
    return hi, lo


ROW_TILES = D_MODEL // LANES


def _load_token_tiles(ref3):
    return jnp.concatenate([ref3[:, s, :] for s in range(ROW_TILES)], axis=1)


def _store_token_tiles(ref3, val):
    for s in range(ROW_TILES):
        ref3[:, s, :] = val[:, s * LANES:(s + 1) * LANES]


def _adaln_kernel(c_ref, w_ref, b_ref, o_ref):
    c = c_ref[...]
    o_ref[...] = _dot(c * _sigmoid(c), w_ref[...], HIGHEST) + b_ref[...]


def _adaln(c_pad, ada_w, ada_b):
    d = c_pad.shape[1]
    n = ada_w.shape[1]
    bn = 1024
    return pl.pallas_call(
        _adaln_kernel,
        out_shape=jax.ShapeDtypeStruct((c_pad.shape[0], n), F32),
        grid=(n // bn,),
        in_specs=[pl.BlockSpec((c_pad.shape[0], d), lambda j: (0, 0)),
                  pl.BlockSpec((d, bn), lambda j: (0, j)),
                  pl.BlockSpec((1, bn), lambda j: (0, j))],
        out_specs=pl.BlockSpec((c_pad.shape[0], bn), lambda j: (0, j)),
        name="adaln",
    )(c_pad, ada_w, ada_b)


def _trig_kernel(pos_ref, freq_ref, cos_ref, sin_ref):
    ang = pos_ref[...] * freq_ref[...]
    cos_ref[...] = jnp.cos(ang)
    sin_ref[...] = jnp.sin(ang)


def _trig(pos_row, freq_col):
    t = pos_row.shape[1]
    bt = 2048
    return pl.pallas_call(
        _trig_kernel,
        out_shape=(jax.ShapeDtypeStruct((ROT_HALF, t), F32),) * 2,
        grid=(t // bt,),
        in_specs=[pl.BlockSpec((1, bt), lambda i: (0, i)),
                  pl.BlockSpec((ROT_HALF, 1), lambda i: (0, 0))],
        out_specs=(pl.BlockSpec((ROT_HALF, bt), lambda i: (0, i)),) * 2,
        name="trig",
    )(pos_row, freq_col)


def _rope_rows(p, c, s):
    x1 = p[0:ROT_HALF]
    x2 = p[ROT_HALF:ROT_DIM]
    return jnp.concatenate([x1 * c - x2 * s, x2 * c + x1 * s, p[ROT_DIM:]], axis=0)


def _inproj_t_kernel(xt_ref, g_ref, sc_ref, sh_ref, w_ref, cos_ref, sin_ref,
                     qa_ref, v_ref, qi_ref, wt_ref):
    xt = xt_ref[...]
    ms = jnp.mean(xt * xt, axis=0, keepdims=True)
    h = xt * lax.rsqrt(ms + RMS_EPS) * g_ref[...]
    h = h * (1.0 + sc_ref[0]) + sh_ref[0]
    hb = h.astype(BF16)
    c = cos_ref[...]
    s = sin_ref[...]
    pq = _dot(w_ref[0:A_WIDTH, :], hb)
    for hh in range(A_HEADS):
        r = _rope_rows(pq[hh * 64:(hh + 1) * 64], c, s) * (A_HEAD_DIM ** -0.5 * LOG2E)
        qa_ref[hh * 64:(hh + 1) * 64, :] = r.astype(BF16)
    pv = _dot(w_ref[A_WIDTH:2 * A_WIDTH, :], hb)
    for cb in range(v_ref.shape[0]):
        v_ref[cb] = pv[:, cb * DSA_KB:(cb + 1) * DSA_KB].astype(BF16)
    pi = _dot(w_ref[2 * A_WIDTH:2 * A_WIDTH + 256, :], hb)
    for hh in range(IDX_HEADS):
        q = _rope_rows(pi[hh * 64:(hh + 1) * 64], c, s) * (IDX_DIM ** -0.5)
        q_hi, q_lo = _split_bf16(q)
        qi_ref[hh * IDX_K:(hh + 1) * IDX_K, :] = jnp.concatenate(
            [q_hi, q_hi, q_lo, jnp.zeros_like(q_hi)], axis=0).astype(BF16)
    pw = _dot(w_ref[2 * A_WIDTH + 256:2 * A_WIDTH + 264, :], hb)
    wt_ref[...] = pw * (IDX_HEADS ** -0.5)


def _inproj_t(xt, g_col, sc_col, sh_col, w_t, cos_t, sin_t, seq):
    d, t = xt.shape
    tn = 512
    per_b = seq // tn
    rows = w_t.shape[0]
    return pl.pallas_call(
        _inproj_t_kernel,
        out_shape=(jax.ShapeDtypeStruct((A_WIDTH, t), BF16),
                   jax.ShapeDtypeStruct((t // DSA_KB, A_WIDTH, DSA_KB), BF16),
                   jax.ShapeDtypeStruct((IDX_HEADS * IDX_K, t), BF16),
                   jax.ShapeDtypeStruct((SUBLANES, t), F32)),
        grid=(t // tn,),
        in_specs=[pl.BlockSpec((d, tn), lambda i: (0, i)),
                  pl.BlockSpec((d, 1), lambda i: (0, 0)),
                  pl.BlockSpec((1, d, 1), lambda i: (i // per_b, 0, 0)),
                  pl.BlockSpec((1, d, 1), lambda i: (i // per_b, 0, 0)),
                  pl.BlockSpec((rows, d), lambda i: (0, 0)),
                  pl.BlockSpec((ROT_HALF, tn), lambda i: (0, i)),
                  pl.BlockSpec((ROT_HALF, tn), lambda i: (0, i))],
        out_specs=(pl.BlockSpec((A_WIDTH, tn), lambda i: (0, i)),
                   pl.BlockSpec((tn // DSA_KB, A_WIDTH, DSA_KB), lambda i: (i, 0, 0)),
                   pl.BlockSpec((IDX_HEADS * IDX_K, tn), lambda i: (0, i)),
                   pl.BlockSpec((SUBLANES, tn), lambda i: (0, i))),
        compiler_params=pltpu.CompilerParams(vmem_limit_bytes=VMEM_LIMIT),
        name="inproj_t",
    )(xt, g_col, sc_col, sh_col, w_t, cos_t, sin_t)


def _inproj_r_kernel(x_ref, g_ref, sc_ref, sh_ref, w_ref, c_ref, s_ref,
                     ka_ref, ki_ref, qb_ref, fb_ref, ib_ref, gb_ref):
    x = x_ref[...]
    ms = jnp.mean(x * x, axis=-1, keepdims=True)
    h = x * lax.rsqrt(ms + RMS_EPS) * g_ref[...]
    h = h * (1.0 + sc_ref[0]) + sh_ref[0]
    hb = h.astype(BF16)
    c = c_ref[...]
    s = s_ref[...]
    lane = lax.broadcasted_iota(I32, c.shape, 1)
    first = (lane % A_HEAD_DIM) < ROT_HALF

    def rope(p):
        partner = jnp.where(first, pltpu.roll(p, LANES - ROT_HALF, 1), pltpu.roll(p, ROT_HALF, 1))
        return p * c + partner * s

    for j in range(A_WIDTH // LANES):
        p = _dot(hb, w_ref[:, j * LANES:(j + 1) * LANES])
        ka_ref[:, j * LANES:(j + 1) * LANES] = rope(p).astype(BF16)
    o = A_WIDTH
    k_hi, k_lo = _split_bf16(rope(_dot(hb, w_ref[:, o:o + LANES])))
    ki_ref[:, 0:LANES] = (k_hi + pltpu.roll(k_lo, IDX_DIM, 1)).astype(BF16)
    ki_ref[:, LANES:2 * LANES] = k_hi.astype(BF16)
    o += LANES
    qb_ref[...] = _dot(hb, w_ref[:, o:o + 256])
    o += 256
    fb_ref[...] = _dot(hb, w_ref[:, o:o + 256])
    o += 256
    ib_ref[...] = _dot(hb, w_ref[:, o:o + 512])
    o += 512
    gb_ref[...] = _dot(hb, w_ref[:, o:o + 512])


def _inproj_r(x2, g_row, sc_row, sh_row, w_r, c_tab, s_tab, seq):
    t, d = x2.shape
    tm = 512
    per_b = seq // tm
    cols = w_r.shape[1]
    widths = (A_WIDTH, IDX_K, 256, 256, 512, 512)
    dts = (BF16, BF16, F32, F32, F32, F32)
    return pl.pallas_call(
        _inproj_r_kernel,
        out_shape=tuple(jax.ShapeDtypeStruct((t, w), dt) for w, dt in zip(widths, dts)),
        grid=(t // tm,),
        in_specs=[pl.BlockSpec((tm, d), lambda i: (i, 0)),
                  pl.BlockSpec((1, d), lambda i: (0, 0)),
                  pl.BlockSpec((1, 1, d), lambda i: (i // per_b, 0, 0)),
                  pl.BlockSpec((1, 1, d), lambda i: (i // per_b, 0, 0)),
                  pl.BlockSpec((d, cols), lambda i: (0, 0)),
                  pl.BlockSpec((tm, LANES), lambda i: (i, 0)),
                  pl.BlockSpec((tm, LANES), lambda i: (i, 0))],
        out_specs=tuple(pl.BlockSpec((tm, w), lambda i: (i, 0)) for w in widths),
        compiler_params=pltpu.CompilerParams(vmem_limit_bytes=VMEM_LIMIT),
        name="inproj_r",
    )(x2, g_row, sc_row, sh_row, w_r, c_tab, s_tab)


def _dsa_kernel(qi_ref, wt_ref, ki_ref, qa_ref, ka_ref, v_ref, o_ref,
                keys_ref, m_ref, l_ref, acc_ref):
    j = pl.program_id(1)
    nkb = j + 1
    tq = DSA_TQ
    kb_rows = DSA_KB
    row = lax.broadcasted_iota(I32, (kb_rows, tq), 0)
    col = lax.broadcasted_iota(I32, (kb_rows, tq), 1)
    q_chunk = (j * tq + col) // CHUNK

    def score_keys(kb):
        r0 = pl.multiple_of(kb * kb_rows, kb_rows)
        ki = ki_ref[pl.ds(r0, kb_rows), :]
        sc = jnp.zeros((kb_rows, tq), F32)
        for h in range(IDX_HEADS):
            lg = _dot(ki, qi_ref[h * IDX_K:(h + 1) * IDX_K, :])
            sc = sc + wt_ref[h:h + 1, :] * jnp.maximum(lg, 0.0)
        sc = jnp.where(sc == 0.0, 0.0, sc)
        bits = pltpu.bitcast(sc, I32)
        return jnp.where(bits < 0, bits ^ 0x7FFFFFFF, bits)

    def score_block(kb, carry):
        keys_ref[kb] = score_keys(kb)
        return carry

    lax.fori_loop(0, j, score_block, 0)
    adm = ((j * kb_rows + row) // CHUNK) <= q_chunk
    keys_ref[j] = jnp.where(adm, score_keys(j), INT_MIN)

    def count(pred_fn):
        def body(kb, acc):
            return acc + jnp.sum(jnp.where(pred_fn(keys_ref[kb], kb), 1, 0), axis=0, keepdims=True)
        return lax.fori_loop(0, nkb, body, jnp.zeros((1, tq), I32))

    def bit_step(i, res):
        cand = res ^ lax.shift_left(jnp.int32(1), 31 - i)
        cnt = count(lambda k, kb: k >= cand)
        return jnp.where(cnt >= TOPK_MAX, cand, res)

    thr = lax.fori_loop(0, 32, bit_step, jnp.full((1, tq), INT_MIN, I32))
    n_gt = count(lambda k, kb: k > thr)
    n_eq = count(lambda k, kb: k == thr)
    need = TOPK_MAX - n_gt
    live = thr != INT_MIN
    excess = jnp.logical_and(n_gt + n_eq > TOPK_MAX, live)
    any_excess = jnp.max(jnp.where(excess, 1, 0)) > 0

    def idx_search():
        def step(i, p):
            cand = p | lax.shift_left(jnp.int32(1), 11 - i)
            cnt = count(lambda k, kb: jnp.logical_and(k == thr, kb * kb_rows + row < cand))
            return jnp.where(cnt < need, cand, p)
        return lax.fori_loop(0, 12, step, jnp.zeros((1, tq), I32))

    p_full = jnp.full((1, tq), 4095, I32)
    p_idx = lax.cond(any_excess, idx_search, lambda: p_full)
    p_idx = jnp.where(live, p_idx, -1)

    m_ref[...] = jnp.full(m_ref.shape, NEG_BIG, F32)
    l_ref[...] = jnp.zeros(l_ref.shape, F32)
    acc_ref[...] = jnp.zeros(acc_ref.shape, F32)

    def attn_block(kb, carry):
        r0 = pl.multiple_of(kb * kb_rows, kb_rows)
        key = keys_ref[kb]
        sel = jnp.logical_or(key > thr,
                             jnp.logical_and(key == thr, kb * kb_rows + row <= p_idx))
        bias = jnp.where(sel, 0.0, NEG_BIG)
        heads = [slice(h * A_HEAD_DIM, (h + 1) * A_HEAD_DIM) for h in range(A_HEADS)]
        scores = [_dot(ka_ref[pl.ds(r0, kb_rows), hs], qa_ref[hs, :]) for hs in heads]
        for h, hs in enumerate(heads):
            s = scores[h] + bias
            m_old = m_ref[h][0:1, :]
            m_new = jnp.maximum(m_old, jnp.max(s, axis=0, keepdims=True))
            alpha = jnp.exp2(m_old - m_new)
            p = jnp.exp2(s - m_new)
            l_new = alpha * l_ref[h][0:1, :] + jnp.sum(p, axis=0, keepdims=True)
            acc_ref[hs, :] = alpha * acc_ref[hs, :] + _dot(v_ref[kb, hs, :], p.astype(BF16))
            m_ref[h] = jnp.broadcast_to(m_new, (SUBLANES, tq))
            l_ref[h] = jnp.broadcast_to(l_new, (SUBLANES, tq))
        return carry

    lax.fori_loop(0, nkb, attn_block, 0)

    for h in range(A_HEADS):
        hs = slice(h * A_HEAD_DIM, (h + 1) * A_HEAD_DIM)
        acc_ref[hs, :] = acc_ref[hs, :] / l_ref[h][0:1, :]
    o_ref[...] = acc_ref[...].T


def _dsa(qi_t, w_t, ki, qa_t, ka, v_blk, batch, seq):
    t = ka.shape[0]
    nqb = seq // DSA_TQ
    nkb = seq // DSA_KB
    return pl.pallas_call(
        _dsa_kernel,
        out_shape=jax.ShapeDtypeStruct((t, A_WIDTH), F32),
        grid=(batch, nqb),
        in_specs=[pl.BlockSpec((IDX_HEADS * IDX_K, DSA_TQ), lambda b, j: (0, b * nqb + j)),
                  pl.BlockSpec((SUBLANES, DSA_TQ), lambda b, j: (0, b * nqb + j)),
                  pl.BlockSpec((seq, IDX_K), lambda b, j: (b, 0)),
                  pl.BlockSpec((A_WIDTH, DSA_TQ), lambda b, j: (0, b * nqb + j)),
                  pl.BlockSpec((seq, A_WIDTH), lambda b, j: (b, 0)),
                  pl.BlockSpec((nkb, A_WIDTH, DSA_KB), lambda b, j: (b, 0, 0))],
        out_specs=pl.BlockSpec((DSA_TQ, A_WIDTH), lambda b, j: (b * nqb + j, 0)),
        scratch_shapes=[pltpu.VMEM((nkb, DSA_KB, DSA_TQ), I32),
                        pltpu.VMEM((A_HEADS, SUBLANES, DSA_TQ), F32),
                        pltpu.VMEM((A_HEADS, SUBLANES, DSA_TQ), F32),
                        pltpu.VMEM((A_WIDTH, DSA_TQ), F32)],
        compiler_params=pltpu.CompilerParams(vmem_limit_bytes=VMEM_LIMIT),
        name="dsa",
    )(qi_t, w_t, ki, qa_t, ka, v_blk)


def _hgrn_constants():
    c = CHUNK
    t = np.arange(c)[:, None]
    u = np.arange(c)[None, :]
    blocks = [(u <= t), (u > t)]
    masks = []
    for half in HG_LEVELS:
        mid = (t // (2 * half)) * (2 * half) + half - 1
        right = ((t // half) % 2) == 1
        blocks.append(right & (u > mid) & (u <= t))
        blocks.append((~right) & (u > t) & (u <= mid))
        tt, ss = t, u
        masks.append(((tt // (2 * half)) == (ss // (2 * half)))
                     & ((((tt // half) % 2) == 1) & (((ss // half) % 2) == 0)))
    masks.append(t == u)
    cm = np.concatenate(blocks, axis=0).astype(np.float32)
    cm3 = np.concatenate([cm, cm, cm], axis=1)
    return cm3, np.stack(masks).astype(np.float32)


def _hgrn_kernel(qb_ref, fb_ref, ib_ref, gb_ref, lb_ref, hg_ref, cm_ref, mask_ref, eye_ref,
                 o_ref, state_ref):
    @pl.when(pl.program_id(1) == 0)
    def _():
        state_ref[...] = jnp.zeros(state_ref.shape, F32)

    lb = lb_ref[...]
    c = CHUNK
    nlev = len(HG_LEVELS)

    def chunk_step(ci, carry):
        r0 = pl.multiple_of(ci * c, c)
        f = lb + (1.0 - lb) * _sigmoid(fb_ref[pl.ds(r0, c), :])
        lf = jnp.log(f)
        kin = 1.0 - f
        hi = lf.astype(BF16)
        r1 = lf - hi.astype(F32)
        mid = r1.astype(BF16)
        lo = (r1 - mid.astype(F32)).astype(BF16)
        e = jnp.exp(_dot(cm_ref[...], jnp.concatenate([hi, mid, lo], axis=0)))
        q = qb_ref[pl.ds(r0, c), :]
        q_in = (q * e[0:c]).astype(BF16)
        k_out = (kin * e[c:2 * c]).astype(BF16)
        e_last = e[c - 1:c, :]
        q_lv = [(q * e[(2 + 2 * i) * c:(3 + 2 * i) * c]).astype(BF16) for i in range(nlev)]
        k_lv = [(kin * e[(3 + 2 * i) * c:(4 + 2 * i) * c]).astype(BF16) for i in range(nlev)]
        q_lv.append(q.astype(BF16))
        k_lv.append(kin.astype(BF16))
        for h in range(B_HEADS):
            ks = slice(h * B_KEY_DIM, (h + 1) * B_KEY_DIM)
            vs = slice(h * B_VAL_DIM, (h + 1) * B_VAL_DIM)
            attn = jnp.zeros((c, c), F32)
            for i in range(nlev + 1):
                attn = attn + mask_ref[i] * _dot_nt(q_lv[i][:, ks], k_lv[i][:, ks])
            v = ib_ref[pl.ds(r0, c), vs]
            vb = v.astype(BF16)
            st = state_ref[h]
            o = _dot_nt(q_in[:, ks], st.astype(BF16)) + _dot(attn.astype(BF16), vb)
            v_t = _dot_nt(eye_ref[...], vb).astype(BF16)
            state_ref[h] = st * e_last[:, ks] + _dot(v_t, k_out[:, ks])
            ms = jnp.mean(o * o, axis=-1, keepdims=True)
            y = o * lax.rsqrt(ms + RMS_EPS) * hg_ref[...]
            g = gb_ref[pl.ds(r0, c), vs]
            o_ref[pl.ds(r0, c), vs] = y * (g * _sigmoid(g))
        return carry

    lax.fori_loop(0, HG_STEP_CHUNKS, chunk_step, 0)


def _hgrn(qb, fb, ib, gb, lb_row, hg_row, batch, seq):
    t = qb.shape[0]
    tm = HG_STEP_CHUNKS * CHUNK
    per_b = seq // tm
    cm3, masks = _hgrn_constants()
    cm3 = jnp.asarray(cm3, BF16)
    masks = jnp.asarray(masks, F32)
    eye = jnp.eye(B_VAL_DIM, dtype=BF16)
    return pl.pallas_call(
        _hgrn_kernel,
        out_shape=jax.ShapeDtypeStruct((t, B_WIDTH), F32),
        grid=(batch, per_b),
        in_specs=[pl.BlockSpec((tm, 256), lambda b, i: (b * per_b + i, 0)),
                  pl.BlockSpec((tm, 256), lambda b, i: (b * per_b + i, 0)),
                  pl.BlockSpec((tm, B_WIDTH), lambda b, i: (b * per_b + i, 0)),
                  pl.BlockSpec((tm, B_WIDTH), lambda b, i: (b * per_b + i, 0)),
                  pl.BlockSpec((1, 256), lambda b, i: (0, 0)),
                  pl.BlockSpec((1, B_VAL_DIM), lambda b, i: (0, 0)),
                  pl.BlockSpec(cm3.shape, lambda b, i: (0, 0)),
                  pl.BlockSpec(masks.shape, lambda b, i: (0, 0, 0)),
                  pl.BlockSpec(eye.shape, lambda b, i: (0, 0))],
        out_specs=pl.BlockSpec((tm, B_WIDTH), lambda b, i: (b * per_b + i, 0)),
        scratch_shapes=[pltpu.VMEM((B_HEADS, B_VAL_DIM, B_KEY_DIM), F32)],
        compiler_params=pltpu.CompilerParams(vmem_limit_bytes=VMEM_LIMIT),
        name="hgrn",
    )(qb, fb, ib, gb, lb_row, hg_row, cm3, masks, eye)


def _outproj_kernel(oa_ref, ob_ref, x_ref, g1_ref, wo_ref, n2_ref, sc_ref, sh_ref,
                    rw_ref, rb_ref, tri_ref,
                    x1_ref, h2_ref, idx_ref, gate_ref, rank_ref, cnt_ref, run_ref):
    @pl.when(pl.program_id(0) == 0)
    def _():
        run_ref[...] = jnp.zeros(run_ref.shape, F32)

    half = A_WIDTH
    mix = _dot(oa_ref[...].astype(BF16), wo_ref[0:half, :]) + _dot(ob_ref[...].astype(BF16), wo_ref[half:, :])
    x1 = x_ref[...] + g1_ref[0] * mix
    x1_ref[...] = x1
    ms = jnp.mean(x1 * x1, axis=-1, keepdims=True)
    h2 = x1 * lax.rsqrt(ms + RMS_EPS) * n2_ref[...]
    h2 = h2 * (1.0 + sc_ref[0]) + sh_ref[0]
    _store_token_tiles(h2_ref, h2)
    logits = _dot(h2, rw_ref[...], HIGHEST) + rb_ref[...]
    tm = logits.shape[0]
    lane = lax.broadcasted_iota(I32, (tm, LANES), 1)
    work = logits
    vals, idxs = [], []
    for _ in range(TOP_K_EXPERTS):
        m = jnp.max(work, axis=-1, keepdims=True)
        ix = jnp.min(jnp.where(work == m, lane, LANES), axis=-1, keepdims=True)
        vals.append(m)
        idxs.append(ix)
        work = jnp.where(lane == ix, -jnp.inf, work)
    es = [jnp.exp(v - vals[0]) for v in vals]
    tot = es[0] + es[1] + es[2] + es[3]
    onehot = jnp.zeros((tm, LANES), F32)
    idx_out = jnp.zeros((tm, LANES), I32)
    gate_out = jnp.zeros((tm, LANES), F32)
    for k in range(TOP_K_EXPERTS):
        onehot = onehot + jnp.where(lane == idxs[k], 1.0, 0.0)
        idx_out = jnp.where(lane == k, idxs[k], idx_out)
        gate_out = jnp.where(lane == k, es[k] / tot, gate_out)
    idx_ref[...] = idx_out
    gate_ref[...] = gate_out
    before = _dot(tri_ref[...], onehot.astype(BF16)) + run_ref[...]
    rank_out = jnp.zeros((tm, LANES), F32)
    for k in range(TOP_K_EXPERTS):
        rk = jnp.sum(jnp.where(lane == idxs[k], before, 0.0), axis=-1, keepdims=True)
        rank_out = jnp.where(lane == k, rk, rank_out)
    rank_ref[...] = rank_out.astype(I32)
    run = run_ref[...] + jnp.sum(onehot, axis=0, keepdims=True)
    run_ref[...] = run
    cnt_ref[...] = run.astype(I32)


def _outproj(oa, ob, x2, g1, wo, n2_row, sc2, sh2, rw, rb, seq):
    t, d = x2.shape
    tm = 512
    per_b = seq // tm
    tri = jnp.asarray(np.tril(np.ones((tm, tm), np.float32), -1), BF16)
    row_spec = lambda w: pl.BlockSpec((tm, w), lambda i: (i, 0))
    mod_spec = pl.BlockSpec((1, 1, d), lambda i: (i // per_b, 0, 0))
    full = lambda a: pl.BlockSpec(a.shape, lambda i: (0,) * a.ndim)
    return pl.pallas_call(
        _outproj_kernel,
        out_shape=(jax.ShapeDtypeStruct((t, d), F32), jax.ShapeDtypeStruct((t, ROW_TILES, LANES), F32),
                   jax.ShapeDtypeStruct((t, LANES), I32), jax.ShapeDtypeStruct((t, LANES), F32),
                   jax.ShapeDtypeStruct((t, LANES), I32), jax.ShapeDtypeStruct((1, LANES), I32)),
        grid=(t // tm,),
        in_specs=[row_spec(A_WIDTH), row_spec(B_WIDTH), row_spec(d), mod_spec, full(wo),
                  full(n2_row), mod_spec, mod_spec, full(rw), full(rb), full(tri)],
        out_specs=(row_spec(d), pl.BlockSpec((tm, ROW_TILES, LANES), lambda i: (i, 0, 0)),
                   row_spec(LANES), row_spec(LANES), row_spec(LANES),
                   pl.BlockSpec((1, LANES), lambda i: (0, 0))),
        scratch_shapes=[pltpu.VMEM((1, LANES), F32)],
        compiler_params=pltpu.CompilerParams(dimension_semantics=("arbitrary",),
                                             vmem_limit_bytes=VMEM_LIMIT),
        name="outproj",
    )(oa, ob, x2, g1, wo, n2_row, sc2, sh2, rw, rb, tri)


DISPATCH_TOKENS = 256
WAIT_UNROLL = 16


def _drain(make_copy, n):
    def body(g, carry):
        for _ in range(WAIT_UNROLL):
            make_copy().wait()
        return carry

    lax.fori_loop(0, n // WAIT_UNROLL, body, 0)


def _dispatch_kernel(rows_ref, h2_ref, zero_hbm, out_hbm, sem):
    del zero_hbm
    n = DISPATCH_TOKENS * TOP_K_EXPERTS

    def issue(a, carry):
        pltpu.make_async_copy(h2_ref.at[a // TOP_K_EXPERTS], out_hbm.at[rows_ref[a]], sem).start()
        return carry

    lax.fori_loop(0, n, issue, 0)
    _drain(lambda: pltpu.make_async_copy(h2_ref.at[0], out_hbm.at[0], sem), n)


def _dispatch(rows_flat, h2, n_rows):
    t = h2.shape[0]
    n = DISPATCH_TOKENS * TOP_K_EXPERTS
    zeros = jnp.zeros((n_rows, ROW_TILES, LANES), F32)
    return pl.pallas_call(
        _dispatch_kernel,
        out_shape=jax.ShapeDtypeStruct((n_rows, ROW_TILES, LANES), F32),
        grid=(t // DISPATCH_TOKENS,),
        in_specs=[pl.BlockSpec((n,), lambda i: (i,), memory_space=pltpu.SMEM),
                  pl.BlockSpec((DISPATCH_TOKENS, ROW_TILES, LANES), lambda i: (i, 0, 0)),
                  pl.BlockSpec(memory_space=pl.ANY)],
        out_specs=pl.BlockSpec(memory_space=pl.ANY),
        scratch_shapes=[pltpu.SemaphoreType.DMA(())],
        input_output_aliases={2: 0},
        name="dispatch",
    )(rows_flat, h2, zeros)


def _experts_kernel(be_ref, nu_ref, x_ref, w1_ref, b1_ref, w2_ref, b2_ref, y_ref, w1b_ref, w2b_ref):
    i = pl.program_id(0)
    prev = be_ref[jnp.maximum(i - 1, 0)]
    changed = jnp.logical_or(i == 0, be_ref[i] != prev)
    used = i < nu_ref[0]

    @pl.when(jnp.logical_and(changed, used))
    def _():
        w1b_ref[...] = w1_ref[0].astype(BF16)
        w2b_ref[...] = w2_ref[0].astype(BF16)

    @pl.when(used)
    def _():
        f = w2b_ref.shape[0]
        hg = _dot(_load_token_tiles(x_ref).astype(BF16), w1b_ref[...]) + b1_ref[0]
        glu = jnp.minimum(hg[:, :f], SWIGLU_LIMIT)
        lin = jnp.clip(hg[:, f:], -SWIGLU_LIMIT, SWIGLU_LIMIT)
        act = glu * _sigmoid(SWIGLU_ALPHA * glu) * (lin + 1.0)
        _store_token_tiles(y_ref, _dot(act.astype(BF16), w2b_ref[...]) + b2_ref[0])

    @pl.when(jnp.logical_not(used))
    def _():
        y_ref[...] = jnp.zeros(y_ref.shape, F32)


def _experts(block_expert, n_used, x_rows, w1, b1, w2, b2):
    n_rows = x_rows.shape[0]
    e, d, f2 = w1.shape
    f = w2.shape[1]
    nb = n_rows // EXPERT_BLOCK
    tok_spec = pl.BlockSpec((EXPERT_BLOCK, ROW_TILES, LANES), lambda i, be, nu: (i, 0, 0))
    return pl.pallas_call(
        _experts_kernel,
        out_shape=jax.ShapeDtypeStruct((n_rows, ROW_TILES, LANES), F32),
        grid_spec=pltpu.PrefetchScalarGridSpec(
            num_scalar_prefetch=2,
            grid=(nb,),
            in_specs=[tok_spec,
                      pl.BlockSpec((1, d, f2), lambda i, be, nu: (be[i], 0, 0)),
                      pl.BlockSpec((1, 1, f2), lambda i, be, nu: (be[i], 0, 0)),
                      pl.BlockSpec((1, f, d), lambda i, be, nu: (be[i], 0, 0)),
                      pl.BlockSpec((1, 1, d), lambda i, be, nu: (be[i], 0, 0))],
            out_specs=tok_spec,
            scratch_shapes=[pltpu.VMEM((d, f2), BF16), pltpu.VMEM((f, d), BF16)]),
        compiler_params=pltpu.CompilerParams(dimension_semantics=("arbitrary",),
                                             vmem_limit_bytes=VMEM_LIMIT),
        name="experts",
    )(block_expert, n_used, x_rows, w1, b1.reshape(e, 1, f2), w2, b2.reshape(e, 1, d))


COMBINE_TOKENS = 256


def _combine_kernel(rows_ref, y_hbm, gate_ref, x1_ref, g2_ref, fg_ref, o_ref, buf_ref, sem):
    n = COMBINE_TOKENS * TOP_K_EXPERTS

    def issue(a, carry):
        pltpu.make_async_copy(y_hbm.at[rows_ref[a]],
                              buf_ref.at[a % TOP_K_EXPERTS, a // TOP_K_EXPERTS], sem).start()
        return carry

    lax.fori_loop(0, n, issue, 0)
    _drain(lambda: pltpu.make_async_copy(y_hbm.at[0], buf_ref.at[0, 0], sem), n)

    gates = gate_ref[...]
    moe = gates[:, 0:1] * _load_token_tiles(buf_ref.at[0])
    for k in range(1, TOP_K_EXPERTS):
        moe = moe + gates[:, k:k + 1] * _load_token_tiles(buf_ref.at[k])
    x2 = x1_ref[...] + g2_ref[0] * moe
    ms = jnp.mean(x2 * x2, axis=-1, keepdims=True)
    o_ref[...] = x2 * lax.rsqrt(ms + RMS_EPS) * fg_ref[...]


def _combine(rows_flat, y_rows, gates, x1, g2, fg_row, seq):
    t, d = x1.shape
    tm = COMBINE_TOKENS
    per_b = seq // tm
    n = tm * TOP_K_EXPERTS
    return pl.pallas_call(
        _combine_kernel,
        out_shape=jax.ShapeDtypeStruct((t, d), F32),
        grid=(t // tm,),
        in_specs=[pl.BlockSpec((n,), lambda i: (i,), memory_space=pltpu.SMEM),
                  pl.BlockSpec(memory_space=pl.ANY),
                  pl.BlockSpec((tm, LANES), lambda i: (i, 0)),
                  pl.BlockSpec((tm, d), lambda i: (i, 0)),
                  pl.BlockSpec((1, 1, d), lambda i: (i // per_b, 0, 0)),
                  pl.BlockSpec((1, d), lambda i: (0, 0))],
        out_specs=pl.BlockSpec((tm, d), lambda i: (i, 0)),
        scratch_shapes=[pltpu.VMEM((TOP_K_EXPERTS, tm, ROW_TILES, LANES), F32), pltpu.SemaphoreType.DMA(())],
        compiler_params=pltpu.CompilerParams(vmem_limit_bytes=VMEM_LIMIT),
        name="combine",
    )(rows_flat, y_rows, gates, x1, g2, fg_row)


def kernel(x, c, positions, ada_w, ada_b, norm1_g, w_in, hg_norm_g, lb_logits, w_out, norm2_g,
           router_w, router_b, moe_w1, moe_b1, moe_w2, moe_b2, final_g):
    batch, seq, d = x.shape
    t = batch * seq
    layer = 0
    x2 = x.reshape(t, d)

    c_pad = jnp.concatenate([c, jnp.zeros((SUBLANES - batch, d), F32)], axis=0)
    mod = _adaln(c_pad, ada_w[layer], ada_b[layer][None, :])[:batch]
    shift1, scale1, gate1, shift2, scale2, gate2 = jnp.split(mod, 6, axis=-1)
    row3 = lambda m: m[:, None, :]
    col3 = lambda m: m[:, :, None]

    inv_freq = ROPE_THETA ** (-(jnp.arange(0, ROT_DIM, 2, dtype=F32) / ROT_DIM))
    cos_t, sin_t = _trig(positions.reshape(1, t).astype(F32), inv_freq[:, None])
    ones = jnp.ones((t, A_HEAD_DIM - ROT_DIM), F32)
    c64 = jnp.concatenate([cos_t.T, cos_t.T, ones], axis=1)
    s64 = jnp.concatenate([-sin_t.T, sin_t.T, 0.0 * ones], axis=1)
    c_tab = jnp.concatenate([c64, c64], axis=1)
    s_tab = jnp.concatenate([s64, s64], axis=1)

    wl = w_in[layer]
    sp = np.cumsum((A_WIDTH, A_WIDTH, A_WIDTH, IDX_HEADS * IDX_DIM, IDX_DIM, IDX_HEADS,
                    B_HEADS * B_KEY_DIM, B_HEADS * B_KEY_DIM, B_WIDTH))
    w_qa, w_ka, w_va, w_qi, w_ki, w_wi, w_qb, w_fb, w_ib, w_gb = jnp.split(wl, [int(v) for v in sp], axis=1)
    w_t = jnp.concatenate([w_qa, w_va, w_qi, w_wi, jnp.zeros((d, SUBLANES - IDX_HEADS), F32)], axis=1)
    w_t = w_t.T.astype(BF16)
    w_r = jnp.concatenate([w_ka, w_ki, jnp.zeros((d, LANES - IDX_DIM), F32), w_qb, w_fb, w_ib, w_gb],
                          axis=1).astype(BF16)

    g1n = norm1_g[layer]
    qa_t, v_blk, qi_t, wi_t = _inproj_t(x2.T, g1n[:, None], col3(scale1), col3(shift1),
                                        w_t, cos_t, sin_t, seq)
    ka, ki, qb, fb, ib, gb = _inproj_r(x2, g1n[None, :], row3(scale1), row3(shift1),
                                       w_r, c_tab, s_tab, seq)

    out_a = _dsa(qi_t, wi_t, ki, qa_t, ka, v_blk, batch, seq)

    lower = jnp.cumsum(jax.nn.softmax(lb_logits.astype(F32), axis=0), axis=0)[layer]
    out_b = _hgrn(qb, fb, ib, gb, lower[None, :], hg_norm_g[layer][None, :], batch, seq)

    rw = jnp.concatenate([router_w[layer], jnp.zeros((d, LANES - N_EXPERTS), F32)], axis=1)
    rb = jnp.concatenate([router_b[layer], jnp.full((LANES - N_EXPERTS,), NEG_BIG, F32)])[None, :]
    x1, h2, idx, gates, rank, counts = _outproj(
        out_a, out_b, x2, row3(gate1), w_out[layer].astype(BF16), norm2_g[layer][None, :],
        row3(scale2), row3(shift2), rw, rb, seq)

    counts = counts[0, :N_EXPERTS]
    padded = (counts + EXPERT_BLOCK - 1) // EXPERT_BLOCK * EXPERT_BLOCK
    pad_end = jnp.cumsum(padded)
    pad_start = pad_end - padded
    n_assign = t * TOP_K_EXPERTS
    n_blocks = -(-n_assign // EXPERT_BLOCK) + N_EXPERTS
    n_rows = n_blocks * EXPERT_BLOCK
    block_start = jnp.arange(n_blocks, dtype=I32) * EXPERT_BLOCK
    block_expert = jnp.minimum(jnp.sum(pad_end[None, :] <= block_start[:, None], axis=1),
                               N_EXPERTS - 1).astype(I32)
    n_used = (pad_end[-1:] // EXPERT_BLOCK).astype(I32)
    rows_flat = (pad_start[idx[:, :TOP_K_EXPERTS]] + rank[:, :TOP_K_EXPERTS]).astype(I32).reshape(-1)

    x_rows = _dispatch(rows_flat, h2, n_rows)
    y_rows = _experts(block_expert, n_used, x_rows, moe_w1[layer], moe_b1[layer],
                      moe_w2[layer], moe_b2[layer])
    out = _combine(rows_flat, y_rows, gates, x1, row3(gate2), final_g[None, :], seq)
    return out.reshape(batch, seq, d)
```

```python
import functools

import numpy as np
import jax
import jax.numpy as jnp
from jax import lax
from jax.experimental import pallas as pl
from jax.experimental.pallas import tpu as pltpu

F32 = jnp.float32
BF16 = jnp.bfloat16
I32 = jnp.int32
HIGHEST = lax.Precision.HIGHEST

D_MODEL = 1024
CHUNK = 64
A_HEADS = 8
A_HEAD_DIM = 64
A_WIDTH = A_HEADS * A_HEAD_DIM
IDX_HEADS = 4
IDX_DIM = 64
TOPK_MAX = 256
B_HEADS = 4
B_KEY_DIM = 64
B_VAL_DIM = 128
B_WIDTH = B_HEADS * B_VAL_DIM
ROPE_THETA = 500000.0
ROT_DIM = A_HEAD_DIM // 4
ROT_HALF = ROT_DIM // 2
N_EXPERTS = 32
TOP_K_EXPERTS = 4
SWIGLU_LIMIT = 7.0
SWIGLU_ALPHA = 1.702
EXPERT_BLOCK = 256
RMS_EPS = 1e-6

LANES = 128
SUBLANES = 8
VMEM_LIMIT = 56 * 1024 * 1024

NEG_BIG = -1e30
INT_MIN = -(2 ** 31)
LOG2E = 1.4426950408889634
IDX_K = 4 * IDX_DIM

DSA_TQ = 256
DSA_KB = 256
HG_STEP_CHUNKS = 8
HG_LEVELS = (32, 16, 8, 4, 2, 1)


def _dot(a, b, precision=None):
    return jnp.dot(a, b, preferred_element_type=F32, precision=precision)


def _dot_nt(a, b, precision=None):
    return lax.dot_general(a, b, (((1,), (1,)), ((), ())),
                           preferred_element_type=F32, precision=precision)


def _sigmoid(x):
    return 1.0 / (1.0 + jnp.exp(-x))


def _split_bf16(x):
    hi = x.astype(BF16).astype(F32)
    lo = (x - hi).astype(BF16).astype(F32)
    return hi, lo


ROW_TILES = D_MODEL // LANES


def _load_token_tiles(ref3):
    return jnp.concatenate([ref3[:, s, :] for s in range(ROW_TILES)], axis=1)


def _store_token_tiles(ref3, val):
    for s in range(ROW_TILES):
        ref3[:, s, :] = val[:, s * LANES:(s + 1) * LANES]


def _adaln_kernel(c_ref, w_ref, b_ref, o_ref):
    c = c_ref[...]
    o_ref[...] = _dot(c * _sigmoid(c), w_ref[...], HIGHEST) + b_ref[...]


def _adaln(c_pad, ada_w, ada_b):
    d = c_pad.shape[1]
    n = ada_w.shape[1]
    bn = 1024
    return pl.pallas_call(
        _adaln_kernel,
        out_shape=jax.ShapeDtypeStruct((c_pad.shape[0], n), F32),
        grid=(n // bn,),
        in_specs=[pl.BlockSpec((c_pad.shape[0], d), lambda j: (0, 0)),
                  pl.BlockSpec((d, bn), lambda j: (0, j)),
                  pl.BlockSpec((1, bn), lambda j: (0, j))],
        out_specs=pl.BlockSpec((c_pad.shape[0], bn), lambda j: (0, j)),
        name="adaln",
    )(c_pad, ada_w, ada_b)


def _trig_kernel(pos_ref, freq_ref, cos_ref, sin_ref):
    ang = pos_ref[...] * freq_ref[...]
    cos_ref[...] = jnp.cos(ang)
    sin_ref[...] = jnp.sin(ang)


def _trig(pos_row, freq_col):
    t = pos_row.shape[1]
    bt = 2048
    return pl.pallas_call(
        _trig_kernel,
        out_shape=(jax.ShapeDtypeStruct((ROT_HALF, t), F32),) * 2,
        grid=(t // bt,),
        in_specs=[pl.BlockSpec((1, bt), lambda i: (0, i)),
                  pl.BlockSpec((ROT_HALF, 1), lambda i: (0, 0))],
        out_specs=(pl.BlockSpec((ROT_HALF, bt), lambda i: (0, i)),) * 2,
        name="trig",
    )(pos_row, freq_col)


def _rope_rows(p, c, s):
    x1 = p[0:ROT_HALF]
    x2 = p[ROT_HALF:ROT_DIM]
    return jnp.concatenate([x1 * c - x2 * s, x2 * c + x1 * s, p[ROT_DIM:]], axis=0)


def _inproj_t_kernel(xt_ref, g_ref, sc_ref, sh_ref, w_ref, cos_ref, sin_ref,
                     qa_ref, v_ref, qi_ref, wt_ref):
    xt = xt_ref[...]
    ms = jnp.mean(xt * xt, axis=0, keepdims=True)
    h = xt * lax.rsqrt(ms + RMS_EPS) * g_ref[...]
    h = h * (1.0 + sc_ref[0]) + sh_ref[0]
    hb = h.astype(BF16)
    c = cos_ref[...]
    s = sin_ref[...]
    pq = _dot(w_ref[0:A_WIDTH, :], hb)
    for hh in range(A_HEADS):
        r = _rope_rows(pq[hh * 64:(hh + 1) * 64], c, s) * (A_HEAD_DIM ** -0.5 * LOG2E)
        qa_ref[hh * 64:(hh + 1) * 64, :] = r.astype(BF16)
    pv = _dot(w_ref[A_WIDTH:2 * A_WIDTH, :], hb)
    for cb in range(v_ref.shape[0]):
        v_ref[cb] = pv[:, cb * DSA_KB:(cb + 1) * DSA_KB].astype(BF16)
    pi = _dot(w_ref[2 * A_WIDTH:2 * A_WIDTH + 256, :], hb)
    for hh in range(IDX_HEADS):
        q = _rope_rows(pi[hh * 64:(hh + 1) * 64], c, s) * (IDX_DIM ** -0.5)
        q_hi, q_lo = _split_bf16(q)
        qi_ref[hh * IDX_K:(hh + 1) * IDX_K, :] = jnp.concatenate(
            [q_hi, q_hi, q_lo, jnp.zeros_like(q_hi)], axis=0).astype(BF16)
    pw = _dot(w_ref[2 * A_WIDTH + 256:2 * A_WIDTH + 264, :], hb)
    wt_ref[...] = pw * (IDX_HEADS ** -0.5)


def _inproj_t(xt, g_col, sc_col, sh_col, w_t, cos_t, sin_t, seq):
    d, t = xt.shape
    tn = 512
    per_b = seq // tn
    rows = w_t.shape[0]
    return pl.pallas_call(
        _inproj_t_kernel,
        out_shape=(jax.ShapeDtypeStruct((A_WIDTH, t), BF16),
                   jax.ShapeDtypeStruct((t // DSA_KB, A_WIDTH, DSA_KB), BF16),
                   jax.ShapeDtypeStruct((IDX_HEADS * IDX_K, t), BF16),
                   jax.ShapeDtypeStruct((SUBLANES, t), F32)),
        grid=(t // tn,),
        in_specs=[pl.BlockSpec((d, tn), lambda i: (0, i)),
                  pl.BlockSpec((d, 1), lambda i: (0, 0)),
                  pl.BlockSpec((1, d, 1), lambda i: (i // per_b, 0, 0)),
                  pl.BlockSpec((1, d, 1), lambda i: (i // per_b, 0, 0)),
                  pl.BlockSpec((rows, d), lambda i: (0, 0)),
                  pl.BlockSpec((ROT_HALF, tn), lambda i: (0, i)),
                  pl.BlockSpec((ROT_HALF, tn), lambda i: (0, i))],
        out_specs=(pl.BlockSpec((A_WIDTH, tn), lambda i: (0, i)),
                   pl.BlockSpec((tn // DSA_KB, A_WIDTH, DSA_KB), lambda i: (i, 0, 0)),
                   pl.BlockSpec((IDX_HEADS * IDX_K, tn), lambda i: (0, i)),
                   pl.BlockSpec((SUBLANES, tn), lambda i: (0, i))),
        compiler_params=pltpu.CompilerParams(vmem_limit_bytes=VMEM_LIMIT),
        name="inproj_t",
    )(xt, g_col, sc_col, sh_col, w_t, cos_t, sin_t)


def _inproj_r_kernel(x_ref, g_ref, sc_ref, sh_ref, w_ref, c_ref, s_ref,
                     ka_ref, ki_ref, qb_ref, fb_ref, ib_ref, gb_ref):
    x = x_ref[...]
    ms = jnp.mean(x * x, axis=-1, keepdims=True)
    h = x * lax.rsqrt(ms + RMS_EPS) * g_ref[...]
    h = h * (1.0 + sc_ref[0]) + sh_ref[0]
    hb = h.astype(BF16)
    c = c_ref[...]
    s = s_ref[...]
    lane = lax.broadcasted_iota(I32, c.shape, 1)
    first = (lane % A_HEAD_DIM) < ROT_HALF

    def rope(p):
        partner = jnp.where(first, pltpu.roll(p, LANES - ROT_HALF, 1), pltpu.roll(p, ROT_HALF, 1))
        return p * c + partner * s

    for j in range(A_WIDTH // LANES):
        p = _dot(hb, w_ref[:, j * LANES:(j + 1) * LANES])
        ka_ref[:, j * LANES:(j + 1) * LANES] = rope(p).astype(BF16)
    o = A_WIDTH
    k_hi, k_lo = _split_bf16(rope(_dot(hb, w_ref[:, o:o + LANES])))
    ki_ref[:, 0:LANES] = (k_hi + pltpu.roll(k_lo, IDX_DIM, 1)).astype(BF16)
    ki_ref[:, LANES:2 * LANES] = k_hi.astype(BF16)
    o += LANES
    qb_ref[...] = _dot(hb, w_ref[:, o:o + 256])
    o += 256
    fb_ref[...] = _dot(hb, w_ref[:, o:o + 256])
    o += 256
    ib_ref[...] = _dot(hb, w_ref[:, o:o + 512])
    o += 512
    gb_ref[...] = _dot(hb, w_ref[:, o:o + 512])


def _inproj_r(x2, g_row, sc_row, sh_row, w_r, c_tab, s_tab, seq):
    t, d = x2.shape
    tm = 512
    per_b = seq // tm
    cols = w_r.shape[1]
    widths = (A_WIDTH, IDX_K, 256, 256, 512, 512)
    dts = (BF16, BF16, F32, F32, F32, F32)
    return pl.pallas_call(
        _inproj_r_kernel,
        out_shape=tuple(jax.ShapeDtypeStruct((t, w), dt) for w, dt in zip(widths, dts)),
        grid=(t // tm,),
        in_specs=[pl.BlockSpec((tm, d), lambda i: (i, 0)),
                  pl.BlockSpec((1, d), lambda i: (0, 0)),
                  pl.BlockSpec((1, 1, d), lambda i: (i // per_b, 0, 0)),
                  pl.BlockSpec((1, 1, d), lambda i: (i // per_b, 0, 0)),
                  pl.BlockSpec((d, cols), lambda i: (0, 0)),
                  pl.BlockSpec((tm, LANES), lambda i: (i, 0)),
                  pl.BlockSpec((tm, LANES), lambda i: (i, 0))],
        out_specs=tuple(pl.BlockSpec((tm, w), lambda i: (i, 0)) for w in widths),
        compiler_params=pltpu.CompilerParams(vmem_limit_bytes=VMEM_LIMIT),
        name="inproj_r",
    )(x2, g_row, sc_row, sh_row, w_r, c_tab, s_tab)


def _dsa_kernel(qi_ref, wt_ref, ki_ref, qa_ref, ka_ref, v_ref, o_ref,
                keys_ref, m_ref, l_ref, acc_ref, s_ref, p_ref):
    j = pl.program_id(1)
    nkb = j + 1
    tq = DSA_TQ
    kb_rows = DSA_KB
    row = lax.broadcasted_iota(I32, (kb_rows, tq), 0)
    col = lax.broadcasted_iota(I32, (kb_rows, tq), 1)
    q_chunk = (j * tq + col) // CHUNK

    def score_keys(kb):
        r0 = pl.multiple_of(kb * kb_rows, kb_rows)
        ki = ki_ref[pl.ds(r0, kb_rows), :]
        sc = jnp.zeros((kb_rows, tq), F32)
        for h in range(IDX_HEADS):
            lg = _dot(ki, qi_ref[h * IDX_K:(h + 1) * IDX_K, :])
            sc = sc + wt_ref[h:h + 1, :] * jnp.maximum(lg, 0.0)
        sc = jnp.where(sc == 0.0, 0.0, sc)
        bits = pltpu.bitcast(sc, I32)
        return jnp.where(bits < 0, bits ^ 0x7FFFFFFF, bits)

    def score_block(kb, carry):
        keys_ref[kb] = score_keys(kb)
        return carry

    lax.fori_loop(0, j, score_block, 0)
    adm = ((j * kb_rows + row) // CHUNK) <= q_chunk
    keys_ref[j] = jnp.where(adm, score_keys(j), INT_MIN)

    def count(pred_fn):
        def body(kb, acc):
            return acc + jnp.sum(jnp.where(pred_fn(keys_ref[kb], kb), 1, 0), axis=0, keepdims=True)
        return lax.fori_loop(0, nkb, body, jnp.zeros((1, tq), I32))

    def bit_step(i, res):
        cand = res ^ lax.shift_left(jnp.int32(1), 31 - i)
        cnt = count(lambda k, kb: k >= cand)
        return jnp.where(cnt >= TOPK_MAX, cand, res)

    thr = lax.fori_loop(0, 32, bit_step, jnp.full((1, tq), INT_MIN, I32))
    n_gt = count(lambda k, kb: k > thr)
    n_eq = count(lambda k, kb: k == thr)
    need = TOPK_MAX - n_gt
    live = thr != INT_MIN
    excess = jnp.logical_and(n_gt + n_eq > TOPK_MAX, live)
    any_excess = jnp.max(jnp.where(excess, 1, 0)) > 0

    def idx_search():
        def step(i, p):
            cand = p | lax.shift_left(jnp.int32(1), 11 - i)
            cnt = count(lambda k, kb: jnp.logical_and(k == thr, kb * kb_rows + row < cand))
            return jnp.where(cnt < need, cand, p)
        return lax.fori_loop(0, 12, step, jnp.zeros((1, tq), I32))

    p_full = jnp.full((1, tq), 4095, I32)
    p_idx = lax.cond(any_excess, idx_search, lambda: p_full)
    p_idx = jnp.where(live, p_idx, -1)

    m_ref[...] = jnp.full(m_ref.shape, NEG_BIG, F32)
    l_ref[...] = jnp.zeros(l_ref.shape, F32)
    acc_ref[...] = jnp.zeros(acc_ref.shape, F32)

    def attn_block(kb, carry):
        r0 = pl.multiple_of(kb * kb_rows, kb_rows)
        key = keys_ref[kb]
        sel = jnp.logical_or(key > thr,
                             jnp.logical_and(key == thr, kb * kb_rows + row <= p_idx))
        bias = jnp.where(sel, 0.0, NEG_BIG)
        heads = [slice(h * A_HEAD_DIM, (h + 1) * A_HEAD_DIM) for h in range(A_HEADS)]
        for h, hs in enumerate(heads):
            s_ref[h] = _dot(ka_ref[pl.ds(r0, kb_rows), hs], qa_ref[hs, :])
        alphas = []
        for h, hs in enumerate(heads):
            s = s_ref[h] + bias
            m_old = m_ref[h][0:1, :]
            m_new = jnp.maximum(m_old, jnp.max(s, axis=0, keepdims=True))
            alpha = jnp.exp2(m_old - m_new)
            p = jnp.exp2(s - m_new)
            l_new = alpha * l_ref[h][0:1, :] + jnp.sum(p, axis=0, keepdims=True)
            p_ref[h] = p.astype(BF16)
            m_ref[h] = jnp.broadcast_to(m_new, (SUBLANES, tq))
            l_ref[h] = jnp.broadcast_to(l_new, (SUBLANES, tq))
            alphas.append(alpha)
        for h, hs in enumerate(heads):
            acc_ref[hs, :] = alphas[h] * acc_ref[hs, :] + _dot(v_ref[kb, hs, :], p_ref[h])
        return carry

    lax.fori_loop(0, nkb, attn_block, 0)

    for h in range(A_HEADS):
        hs = slice(h * A_HEAD_DIM, (h + 1) * A_HEAD_DIM)
        acc_ref[hs, :] = acc_ref[hs, :] / l_ref[h][0:1, :]
    o_ref[...] = acc_ref[...].T


def _dsa(qi_t, w_t, ki, qa_t, ka, v_blk, batch, seq):
    t = ka.shape[0]
    nqb = seq // DSA_TQ
    nkb = seq // DSA_KB
    return pl.pallas_call(
        _dsa_kernel,
        out_shape=jax.ShapeDtypeStruct((t, A_WIDTH), F32),
        grid=(batch, nqb),
        in_specs=[pl.BlockSpec((IDX_HEADS * IDX_K, DSA_TQ), lambda b, j: (0, b * nqb + j)),
                  pl.BlockSpec((SUBLANES, DSA_TQ), lambda b, j: (0, b * nqb + j)),
                  pl.BlockSpec((seq, IDX_K), lambda b, j: (b, 0)),
                  pl.BlockSpec((A_WIDTH, DSA_TQ), lambda b, j: (0, b * nqb + j)),
                  pl.BlockSpec((seq, A_WIDTH), lambda b, j: (b, 0)),
                  pl.BlockSpec((nkb, A_WIDTH, DSA_KB), lambda b, j: (b, 0, 0))],
        out_specs=pl.BlockSpec((DSA_TQ, A_WIDTH), lambda b, j: (b * nqb + j, 0)),
        scratch_shapes=[pltpu.VMEM((nkb, DSA_KB, DSA_TQ), I32),
                        pltpu.VMEM((A_HEADS, SUBLANES, DSA_TQ), F32),
                        pltpu.VMEM((A_HEADS, SUBLANES, DSA_TQ), F32),
                        pltpu.VMEM((A_WIDTH, DSA_TQ), F32),
                        pltpu.VMEM((A_HEADS, DSA_KB, DSA_TQ), F32),
                        pltpu.VMEM((A_HEADS, DSA_KB, DSA_TQ), BF16)],
        compiler_params=pltpu.CompilerParams(vmem_limit_bytes=VMEM_LIMIT),
        name="dsa",
    )(qi_t, w_t, ki, qa_t, ka, v_blk)


def _hgrn_constants():
    c = CHUNK
    t = np.arange(c)[:, None]
    u = np.arange(c)[None, :]
    blocks = [(u <= t), (u > t)]
    masks = []
    for half in HG_LEVELS:
        mid = (t // (2 * half)) * (2 * half) + half - 1
        right = ((t // half) % 2) == 1
        blocks.append(right & (u > mid) & (u <= t))
        blocks.append((~right) & (u > t) & (u <= mid))
        tt, ss = t, u
        masks.append(((tt // (2 * half)) == (ss // (2 * half)))
                     & ((((tt // half) % 2) == 1) & (((ss // half) % 2) == 0)))
    masks.append(t == u)
    cm = np.concatenate(blocks, axis=0).astype(np.float32)
    cm3 = np.concatenate([cm, cm, cm], axis=1)
    return cm3, np.stack(masks).astype(np.float32)


def _hgrn_kernel(qb_ref, fb_ref, ib_ref, gb_ref, lb_ref, hg_ref, cm_ref, mask_ref, eye_ref,
                 o_ref, state_ref):
    @pl.when(pl.program_id(1) == 0)
    def _():
        state_ref[...] = jnp.zeros(state_ref.shape, F32)

    lb = lb_ref[...]
    c = CHUNK
    nlev = len(HG_LEVELS)

    def chunk_step(ci, carry):
        r0 = pl.multiple_of(ci * c, c)
        f = lb + (1.0 - lb) * _sigmoid(fb_ref[pl.ds(r0, c), :])
        lf = jnp.log(f)
        kin = 1.0 - f
        hi = lf.astype(BF16)
        r1 = lf - hi.astype(F32)
        mid = r1.astype(BF16)
        lo = (r1 - mid.astype(F32)).astype(BF16)
        e = jnp.exp(_dot(cm_ref[...], jnp.concatenate([hi, mid, lo], axis=0)))
        q = qb_ref[pl.ds(r0, c), :]
        q_in = (q * e[0:c]).astype(BF16)
        k_out = (kin * e[c:2 * c]).astype(BF16)
        e_last = e[c - 1:c, :]
        q_lv = [(q * e[(2 + 2 * i) * c:(3 + 2 * i) * c]).astype(BF16) for i in range(nlev)]
        k_lv = [(kin * e[(3 + 2 * i) * c:(4 + 2 * i) * c]).astype(BF16) for i in range(nlev)]
        q_lv.append(q.astype(BF16))
        k_lv.append(kin.astype(BF16))
        for h in range(B_HEADS):
            ks = slice(h * B_KEY_DIM, (h + 1) * B_KEY_DIM)
            vs = slice(h * B_VAL_DIM, (h + 1) * B_VAL_DIM)
            attn = jnp.zeros((c, c), F32)
            for i in range(nlev + 1):
                attn = attn + mask_ref[i] * _dot_nt(q_lv[i][:, ks], k_lv[i][:, ks])
            v = ib_ref[pl.ds(r0, c), vs]
            vb = v.astype(BF16)
            st = state_ref[h]
            o = _dot_nt(q_in[:, ks], st.astype(BF16)) + _dot(attn.astype(BF16), vb)
            v_t = _dot_nt(eye_ref[...], vb).astype(BF16)
            state_ref[h] = st * e_last[:, ks] + _dot(v_t, k_out[:, ks])
            ms = jnp.mean(o * o, axis=-1, keepdims=True)
            y = o * lax.rsqrt(ms + RMS_EPS) * hg_ref[...]
            g = gb_ref[pl.ds(r0, c), vs]
            o_ref[pl.ds(r0, c), vs] = y * (g * _sigmoid(g))
        return carry

    lax.fori_loop(0, HG_STEP_CHUNKS, chunk_step, 0)


def _hgrn(qb, fb, ib, gb, lb_row, hg_row, batch, seq):
    t = qb.shape[0]
    tm = HG_STEP_CHUNKS * CHUNK
    per_b = seq // tm
    cm3, masks = _hgrn_constants()
    cm3 = jnp.asarray(cm3, BF16)
    masks = jnp.asarray(masks, F32)
    eye = jnp.eye(B_VAL_DIM, dtype=BF16)
    return pl.pallas_call(
        _hgrn_kernel,
        out_shape=jax.ShapeDtypeStruct((t, B_WIDTH), F32),
        grid=(batch, per_b),
        in_specs=[pl.BlockSpec((tm, 256), lambda b, i: (b * per_b + i, 0)),
                  pl.BlockSpec((tm, 256), lambda b, i: (b * per_b + i, 0)),
                  pl.BlockSpec((tm, B_WIDTH), lambda b, i: (b * per_b + i, 0)),
                  pl.BlockSpec((tm, B_WIDTH), lambda b, i: (b * per_b + i, 0)),
                  pl.BlockSpec((1, 256), lambda b, i: (0, 0)),
                  pl.BlockSpec((1, B_VAL_DIM), lambda b, i: (0, 0)),
                  pl.BlockSpec(cm3.shape, lambda b, i: (0, 0)),
                  pl.BlockSpec(masks.shape, lambda b, i: (0, 0, 0)),
                  pl.BlockSpec(eye.shape, lambda b, i: (0, 0))],
        out_specs=pl.BlockSpec((tm, B_WIDTH), lambda b, i: (b * per_b + i, 0)),
        scratch_shapes=[pltpu.VMEM((B_HEADS, B_VAL_DIM, B_KEY_DIM), F32)],
        compiler_params=pltpu.CompilerParams(vmem_limit_bytes=VMEM_LIMIT),
        name="hgrn",
    )(qb, fb, ib, gb, lb_row, hg_row, cm3, masks, eye)


def _outproj_kernel(oa_ref, ob_ref, x_ref, g1_ref, wo_ref, n2_ref, sc_ref, sh_ref,
                    rw_ref, rb_ref, tri_ref,
                    x1_ref, h2_ref, idx_ref, gate_ref, rank_ref, cnt_ref, run_ref):
    @pl.when(pl.program_id(0) == 0)
    def _():
        run_ref[...] = jnp.zeros(run_ref.shape, F32)

    half = A_WIDTH
    mix = _dot(oa_ref[...].astype(BF16), wo_ref[0:half, :]) + _dot(ob_ref[...].astype(BF16), wo_ref[half:, :])
    x1 = x_ref[...] + g1_ref[0] * mix
    x1_ref[...] = x1
    ms = jnp.mean(x1 * x1, axis=-1, keepdims=True)
    h2 = x1 * lax.rsqrt(ms + RMS_EPS) * n2_ref[...]
    h2 = h2 * (1.0 + sc_ref[0]) + sh_ref[0]
    _store_token_tiles(h2_ref, h2)
    logits = _dot(h2, rw_ref[...], HIGHEST) + rb_ref[...]
    tm = logits.shape[0]
    lane = lax.broadcasted_iota(I32, (tm, LANES), 1)
    work = logits
    vals, idxs = [], []
    for _ in range(TOP_K_EXPERTS):
        m = jnp.max(work, axis=-1, keepdims=True)
        ix = jnp.min(jnp.where(work == m, lane, LANES), axis=-1, keepdims=True)
        vals.append(m)
        idxs.append(ix)
        work = jnp.where(lane == ix, -jnp.inf, work)
    es = [jnp.exp(v - vals[0]) for v in vals]
    tot = es[0] + es[1] + es[2] + es[3]
    onehot = jnp.zeros((tm, LANES), F32)
    idx_out = jnp.zeros((tm, LANES), I32)
    gate_out = jnp.zeros((tm, LANES), F32)
    for k in range(TOP_K_EXPERTS):
        onehot = onehot + jnp.where(lane == idxs[k], 1.0, 0.0)
        idx_out = jnp.where(lane == k, idxs[k], idx_out)
        gate_out = jnp.where(lane == k, es[k] / tot, gate_out)
    idx_ref[...] = idx_out
    gate_ref[...] = gate_out
    before = _dot(tri_ref[...], onehot.astype(BF16)) + run_ref[...]
    rank_out = jnp.zeros((tm, LANES), F32)
    for k in range(TOP_K_EXPERTS):
        rk = jnp.sum(jnp.where(lane == idxs[k], before, 0.0), axis=-1, keepdims=True)
        rank_out = jnp.where(lane == k, rk, rank_out)
    rank_ref[...] = rank_out.astype(I32)
    run = run_ref[...] + jnp.sum(onehot, axis=0, keepdims=True)
    run_ref[...] = run
    cnt_ref[...] = run.astype(I32)


def _outproj(oa, ob, x2, g1, wo, n2_row, sc2, sh2, rw, rb, seq):
    t, d = x2.shape
    tm = 512
    per_b = seq // tm
    tri = jnp.asarray(np.tril(np.ones((tm, tm), np.float32), -1), BF16)
    row_spec = lambda w: pl.BlockSpec((tm, w), lambda i: (i, 0))
    mod_spec = pl.BlockSpec((1, 1, d), lambda i: (i // per_b, 0, 0))
    full = lambda a: pl.BlockSpec(a.shape, lambda i: (0,) * a.ndim)
    return pl.pallas_call(
        _outproj_kernel,
        out_shape=(jax.ShapeDtypeStruct((t, d), F32), jax.ShapeDtypeStruct((t, ROW_TILES, LANES), F32),
                   jax.ShapeDtypeStruct((t, LANES), I32), jax.ShapeDtypeStruct((t, LANES), F32),
                   jax.ShapeDtypeStruct((t, LANES), I32), jax.ShapeDtypeStruct((1, LANES), I32)),
        grid=(t // tm,),
        in_specs=[row_spec(A_WIDTH), row_spec(B_WIDTH), row_spec(d), mod_spec, full(wo),
                  full(n2_row), mod_spec, mod_spec, full(rw), full(rb), full(tri)],
        out_specs=(row_spec(d), pl.BlockSpec((tm, ROW_TILES, LANES), lambda i: (i, 0, 0)),
                   row_spec(LANES), row_spec(LANES), row_spec(LANES),
                   pl.BlockSpec((1, LANES), lambda i: (0, 0))),
        scratch_shapes=[pltpu.VMEM((1, LANES), F32)],
        compiler_params=pltpu.CompilerParams(dimension_semantics=("arbitrary",),
                                             vmem_limit_bytes=VMEM_LIMIT),
        name="outproj",
    )(oa, ob, x2, g1, wo, n2_row, sc2, sh2, rw, rb, tri)


DISPATCH_TOKENS = 256
WAIT_UNROLL = 16


def _drain(make_copy, n):
    def body(g, carry):
        for _ in range(WAIT_UNROLL):
            make_copy().wait()
        return carry

    lax.fori_loop(0, n // WAIT_UNROLL, body, 0)


def _dispatch_kernel(rows_ref, h2_ref, zero_hbm, out_hbm, sem):
    del zero_hbm
    n = DISPATCH_TOKENS * TOP_K_EXPERTS

    def issue(t, carry):
        for k in range(TOP_K_EXPERTS):
            pltpu.make_async_copy(h2_ref.at[t], out_hbm.at[rows_ref[t * TOP_K_EXPERTS + k]],
                                  sem).start(priority=k % 2)
        return carry

    lax.fori_loop(0, DISPATCH_TOKENS, issue, 0)
    _drain(lambda: pltpu.make_async_copy(h2_ref.at[0], out_hbm.at[0], sem), n)


def _dispatch(rows_flat, h2, n_rows):
    t = h2.shape[0]
    n = DISPATCH_TOKENS * TOP_K_EXPERTS
    zeros = jnp.zeros((n_rows, ROW_TILES, LANES), F32)
    return pl.pallas_call(
        _dispatch_kernel,
        out_shape=jax.ShapeDtypeStruct((n_rows, ROW_TILES, LANES), F32),
        grid=(t // DISPATCH_TOKENS,),
        in_specs=[pl.BlockSpec((n,), lambda i: (i,), memory_space=pltpu.SMEM),
                  pl.BlockSpec((DISPATCH_TOKENS, ROW_TILES, LANES), lambda i: (i, 0, 0)),
                  pl.BlockSpec(memory_space=pl.ANY)],
        out_specs=pl.BlockSpec(memory_space=pl.ANY),
        scratch_shapes=[pltpu.SemaphoreType.DMA(())],
        input_output_aliases={2: 0},
        name="dispatch",
    )(rows_flat, h2, zeros)


def _experts_kernel(be_ref, nu_ref, x_ref, w1_ref, b1_ref, w2_ref, b2_ref, y_ref, w1b_ref, w2b_ref):
    i = pl.program_id(0)
    prev = be_ref[jnp.maximum(i - 1, 0)]
    changed = jnp.logical_or(i == 0, be_ref[i] != prev)
    used = i < nu_ref[0]

    @pl.when(jnp.logical_and(changed, used))
    def _():
        w1b_ref[...] = w1_ref[0].astype(BF16)
        w2b_ref[...] = w2_ref[0].astype(BF16)

    @pl.when(used)
    def _():
        f = w2b_ref.shape[0]
        hg = _dot(_load_token_tiles(x_ref).astype(BF16), w1b_ref[...]) + b1_ref[0]
        glu = jnp.minimum(hg[:, :f], SWIGLU_LIMIT)
        lin = jnp.clip(hg[:, f:], -SWIGLU_LIMIT, SWIGLU_LIMIT)
        act = glu * _sigmoid(SWIGLU_ALPHA * glu) * (lin + 1.0)
        _store_token_tiles(y_ref, _dot(act.astype(BF16), w2b_ref[...]) + b2_ref[0])

    @pl.when(jnp.logical_not(used))
    def _():
        y_ref[...] = jnp.zeros(y_ref.shape, F32)


def _experts(block_expert, n_used, x_rows, w1, b1, w2, b2):
    n_rows = x_rows.shape[0]
    e, d, f2 = w1.shape
    f = w2.shape[1]
    nb = n_rows // EXPERT_BLOCK
    tok_spec = pl.BlockSpec((EXPERT_BLOCK, ROW_TILES, LANES), lambda i, be, nu: (i, 0, 0))
    return pl.pallas_call(
        _experts_kernel,
        out_shape=jax.ShapeDtypeStruct((n_rows, ROW_TILES, LANES), F32),
        grid_spec=pltpu.PrefetchScalarGridSpec(
            num_scalar_prefetch=2,
            grid=(nb,),
            in_specs=[tok_spec,
                      pl.BlockSpec((1, d, f2), lambda i, be, nu: (be[i], 0, 0)),
                      pl.BlockSpec((1, 1, f2), lambda i, be, nu: (be[i], 0, 0)),
                      pl.BlockSpec((1, f, d), lambda i, be, nu: (be[i], 0, 0)),
                      pl.BlockSpec((1, 1, d), lambda i, be, nu: (be[i], 0, 0))],
            out_specs=tok_spec,
            scratch_shapes=[pltpu.VMEM((d, f2), BF16), pltpu.VMEM((f, d), BF16)]),
        compiler_params=pltpu.CompilerParams(dimension_semantics=("arbitrary",),
                                             vmem_limit_bytes=VMEM_LIMIT),
        name="experts",
    )(block_expert, n_used, x_rows, w1, b1.reshape(e, 1, f2), w2, b2.reshape(e, 1, d))


COMBINE_TOKENS = 256


def _combine_kernel(rows_ref, y_hbm, gate_ref, x1_ref, g2_ref, fg_ref, o_ref, buf_ref, sem):
    n = COMBINE_TOKENS * TOP_K_EXPERTS

    def issue(t, carry):
        for k in range(TOP_K_EXPERTS):
            pltpu.make_async_copy(y_hbm.at[rows_ref[t * TOP_K_EXPERTS + k]], buf_ref.at[k, t],
                                  sem).start(priority=k % 2)
        return carry

    lax.fori_loop(0, COMBINE_TOKENS, issue, 0)
    _drain(lambda: pltpu.make_async_copy(y_hbm.at[0], buf_ref.at[0, 0], sem), n)

    gates = gate_ref[...]
    moe = gates[:, 0:1] * _load_token_tiles(buf_ref.at[0])
    for k in range(1, TOP_K_EXPERTS):
        moe = moe + gates[:, k:k + 1] * _load_token_tiles(buf_ref.at[k])
    x2 = x1_ref[...] + g2_ref[0] * moe
    ms = jnp.mean(x2 * x2, axis=-1, keepdims=True)
    o_ref[...] = x2 * lax.rsqrt(ms + RMS_EPS) * fg_ref[...]


def _combine(rows_flat, y_rows, gates, x1, g2, fg_row, seq):
    t, d = x1.shape
    tm = COMBINE_TOKENS
    per_b = seq // tm
    n = tm * TOP_K_EXPERTS
    return pl.pallas_call(
        _combine_kernel,
        out_shape=jax.ShapeDtypeStruct((t, d), F32),
        grid=(t // tm,),
        in_specs=[pl.BlockSpec((n,), lambda i: (i,), memory_space=pltpu.SMEM),
                  pl.BlockSpec(memory_space=pl.ANY),
                  pl.BlockSpec((tm, LANES), lambda i: (i, 0)),
                  pl.BlockSpec((tm, d), lambda i: (i, 0)),
                  pl.BlockSpec((1, 1, d), lambda i: (i // per_b, 0, 0)),
                  pl.BlockSpec((1, d), lambda i: (0, 0))],
        out_specs=pl.BlockSpec((tm, d), lambda i: (i, 0)),
        scratch_shapes=[pltpu.VMEM((TOP_K_EXPERTS, tm, ROW_TILES, LANES), F32), pltpu.SemaphoreType.DMA(())],
        compiler_params=pltpu.CompilerParams(vmem_limit_bytes=VMEM_LIMIT),
        name="combine",
    )(rows_flat, y_rows, gates, x1, g2, fg_row)


def kernel(x, c, positions, ada_w, ada_b, norm1_g, w_in, hg_norm_g, lb_logits, w_out, norm2_g,
           router_w, router_b, moe_w1, moe_b1, moe_w2, moe_b2, final_g):
    batch, seq, d = x.shape
    t = batch * seq
    layer = 0
    x2 = x.reshape(t, d)

    c_pad = jnp.concatenate([c, jnp.zeros((SUBLANES - batch, d), F32)], axis=0)
    mod = _adaln(c_pad, ada_w[layer], ada_b[layer][None, :])[:batch]
    shift1, scale1, gate1, shift2, scale2, gate2 = jnp.split(mod, 6, axis=-1)
    row3 = lambda m: m[:, None, :]
    col3 = lambda m: m[:, :, None]

    inv_freq = ROPE_THETA ** (-(jnp.arange(0, ROT_DIM, 2, dtype=F32) / ROT_DIM))
    cos_t, sin_t = _trig(positions.reshape(1, t).astype(F32), inv_freq[:, None])
    ones = jnp.ones((t, A_HEAD_DIM - ROT_DIM), F32)
    c64 = jnp.concatenate([cos_t.T, cos_t.T, ones], axis=1)
    s64 = jnp.concatenate([-sin_t.T, sin_t.T, 0.0 * ones], axis=1)
    c_tab = jnp.concatenate([c64, c64], axis=1)
    s_tab = jnp.concatenate([s64, s64], axis=1)

    wl = w_in[layer]
    sp = np.cumsum((A_WIDTH, A_WIDTH, A_WIDTH, IDX_HEADS * IDX_DIM, IDX_DIM, IDX_HEADS,
                    B_HEADS * B_KEY_DIM, B_HEADS * B_KEY_DIM, B_WIDTH))
    w_qa, w_ka, w_va, w_qi, w_ki, w_wi, w_qb, w_fb, w_ib, w_gb = jnp.split(wl, [int(v) for v in sp], axis=1)
    w_t = jnp.concatenate([w_qa, w_va, w_qi, w_wi, jnp.zeros((d, SUBLANES - IDX_HEADS), F32)], axis=1)
    w_t = w_t.T.astype(BF16)
    w_r = jnp.concatenate([w_ka, w_ki, jnp.zeros((d, LANES - IDX_DIM), F32), w_qb, w_fb, w_ib, w_gb],
                          axis=1).astype(BF16)

    g1n = norm1_g[layer]
    qa_t, v_blk, qi_t, wi_t = _inproj_t(x2.T, g1n[:, None], col3(scale1), col3(shift1),
                                        w_t, cos_t, sin_t, seq)
    ka, ki, qb, fb, ib, gb = _inproj_r(x2, g1n[None, :], row3(scale1), row3(shift1),
                                       w_r, c_tab, s_tab, seq)

    out_a = _dsa(qi_t, wi_t, ki, qa_t, ka, v_blk, batch, seq)

    lower = jnp.cumsum(jax.nn.softmax(lb_logits.astype(F32), axis=0), axis=0)[layer]
    out_b = _hgrn(qb, fb, ib, gb, lower[None, :], hg_norm_g[layer][None, :], batch, seq)

    rw = jnp.concatenate([router_w[layer], jnp.zeros((d, LANES - N_EXPERTS), F32)], axis=1)
    rb = jnp.concatenate([router_b[layer], jnp.full((LANES - N_EXPERTS,), NEG_BIG, F32)])[None, :]
    x1, h2, idx, gates, rank, counts = _outproj(
        out_a, out_b, x2, row3(gate1), w_out[layer].astype(BF16), norm2_g[layer][None, :],
        row3(scale2), row3(shift2), rw, rb, seq)

    counts = counts[0, :N_EXPERTS]
    padded = (counts + EXPERT_BLOCK - 1) // EXPERT_BLOCK * EXPERT_BLOCK
    pad_end = jnp.cumsum(padded)
    pad_start = pad_end - padded
    n_assign = t * TOP_K_EXPERTS
    n_blocks = -(-n_assign // EXPERT_BLOCK) + N_EXPERTS
    n_rows = n_blocks * EXPERT_BLOCK
    block_start = jnp.arange(n_blocks, dtype=I32) * EXPERT_BLOCK
    block_expert = jnp.minimum(jnp.sum(pad_end[None, :] <= block_start[:, None], axis=1),
                               N_EXPERTS - 1).astype(I32)
    n_used = (pad_end[-1:] // EXPERT_BLOCK).astype(I32)
    rows_flat = (pad_start[idx[:, :TOP_K_EXPERTS]] + rank[:, :TOP_K_EXPERTS]).astype(I32).reshape(-1)

    x_rows = _dispatch(rows_flat, h2, n_rows)
    y_rows = _experts(block_expert, n_used, x_rows, moe_w1[layer], moe_b1[layer],
                      moe_w2[layer], moe_b2[layer])
    out = _combine(rows_flat, y_rows, gates, x1, row3(gate2), final_g[None, :], seq)
    return out.reshape(batch, seq, d)
```

```python
import functools

import numpy as np
import jax
import jax.numpy as jnp
from jax import lax
from jax.experimental import pallas as pl
from jax.experimental.pallas import tpu as pltpu

F32 = jnp.float32
BF16 = jnp.bfloat16
I32 = jnp.int32
HIGHEST = lax.Precision.HIGHEST

D_MODEL = 1024
CHUNK = 64
A_HEADS = 8
A_HEAD_DIM = 64
A_WIDTH = A_HEADS * A_HEAD_DIM
IDX_HEADS = 4
IDX_DIM = 64
TOPK_MAX = 256
B_HEADS = 4
B_KEY_DIM = 64
B_VAL_DIM = 128
B_WIDTH = B_HEADS * B_VAL_DIM
ROPE_THETA = 500000.0
ROT_DIM = A_HEAD_DIM // 4
ROT_HALF = ROT_DIM // 2
N_EXPERTS = 32
TOP_K_EXPERTS = 4
SWIGLU_LIMIT = 7.0
SWIGLU_ALPHA = 1.702
EXPERT_BLOCK = 256
RMS_EPS = 1e-6

LANES = 128
SUBLANES = 8
VMEM_LIMIT = 56 * 1024 * 1024

NEG_BIG = -1e30
INT_MIN = -(2 ** 31)
INT_MAX = 2 ** 31 - 1
LOG2E = 1.4426950408889634
IDX_K = 4 * IDX_DIM

DSA_TQ = 256
DSA_KB = 256
HG_STEP_CHUNKS = 8
HG_LEVELS = (32, 16, 8, 4, 2, 1)


def _dot(a, b, precision=None):
    return jnp.dot(a, b, preferred_element_type=F32, precision=precision)


def _dot_nt(a, b, precision=None):
    return lax.dot_general(a, b, (((1,), (1,)), ((), ())),
                           preferred_element_type=F32, precision=precision)


def _sigmoid(x):
    return 1.0 / (1.0 + jnp.exp(-x))


def _split_bf16(x):
    hi = x.astype(BF16).astype(F32)
    lo = (x - hi).astype(BF16).astype(F32)
    return hi, lo


ROW_TILES = D_MODEL // LANES


def _load_token_tiles(ref3):
    return jnp.concatenate([ref3[:, s, :] for s in range(ROW_TILES)], axis=1)


def _store_token_tiles(ref3, val):
    for s in range(ROW_TILES):
        ref3[:, s, :] = val[:, s * LANES:(s + 1) * LANES]


def _adaln_kernel(c_ref, w_ref, b_ref, o_ref):
    c = c_ref[...]
    o_ref[...] = _dot(c * _sigmoid(c), w_ref[...], HIGHEST) + b_ref[...]


def _adaln(c_pad, ada_w, ada_b):
    d = c_pad.shape[1]
    n = ada_w.shape[1]
    bn = 1024
    return pl.pallas_call(
        _adaln_kernel,
        out_shape=jax.ShapeDtypeStruct((c_pad.shape[0], n), F32),
        grid=(n // bn,),
        in_specs=[pl.BlockSpec((c_pad.shape[0], d), lambda j: (0, 0)),
                  pl.BlockSpec((d, bn), lambda j: (0, j)),
                  pl.BlockSpec((1, bn), lambda j: (0, j))],
        out_specs=pl.BlockSpec((c_pad.shape[0], bn), lambda j: (0, j)),
        name="adaln",
    )(c_pad, ada_w, ada_b)


def _trig_kernel(pos_ref, freq_ref, cos_ref, sin_ref):
    ang = pos_ref[...] * freq_ref[...]
    cos_ref[...] = jnp.cos(ang)
    sin_ref[...] = jnp.sin(ang)


def _trig(pos_row, freq_col):
    t = pos_row.shape[1]
    bt = 2048
    return pl.pallas_call(
        _trig_kernel,
        out_shape=(jax.ShapeDtypeStruct((ROT_HALF, t), F32),) * 2,
        grid=(t // bt,),
        in_specs=[pl.BlockSpec((1, bt), lambda i: (0, i)),
                  pl.BlockSpec((ROT_HALF, 1), lambda i: (0, 0))],
        out_specs=(pl.BlockSpec((ROT_HALF, bt), lambda i: (0, i)),) * 2,
        name="trig",
    )(pos_row, freq_col)


def _rope_rows(p, c, s):
    x1 = p[0:ROT_HALF]
    x2 = p[ROT_HALF:ROT_DIM]
    return jnp.concatenate([x1 * c - x2 * s, x2 * c + x1 * s, p[ROT_DIM:]], axis=0)


def _inproj_t_kernel(xt_ref, g_ref, sc_ref, sh_ref, w_ref, cos_ref, sin_ref,
                     qa_ref, v_ref, qi_ref, wt_ref):
    xt = xt_ref[...]
    ms = jnp.mean(xt * xt, axis=0, keepdims=True)
    h = xt * lax.rsqrt(ms + RMS_EPS) * g_ref[...]
    h = h * (1.0 + sc_ref[0]) + sh_ref[0]
    hb = h.astype(BF16)
    c = cos_ref[...]
    s = sin_ref[...]
    pq = _dot(w_ref[0:A_WIDTH, :], hb)
    for hh in range(A_HEADS):
        r = _rope_rows(pq[hh * 64:(hh + 1) * 64], c, s) * (A_HEAD_DIM ** -0.5 * LOG2E)
        qa_ref[hh * 64:(hh + 1) * 64, :] = r.astype(BF16)
    pv = _dot(w_ref[A_WIDTH:2 * A_WIDTH, :], hb)
    for cb in range(v_ref.shape[0]):
        v_ref[cb] = pv[:, cb * DSA_KB:(cb + 1) * DSA_KB].astype(BF16)
    pi = _dot(w_ref[2 * A_WIDTH:2 * A_WIDTH + 256, :], hb)
    for hh in range(IDX_HEADS):
        q = _rope_rows(pi[hh * 64:(hh + 1) * 64], c, s) * (IDX_DIM ** -0.5)
        q_hi, q_lo = _split_bf16(q)
        qi_ref[hh * IDX_K:(hh + 1) * IDX_K, :] = jnp.concatenate(
            [q_hi, q_hi, q_lo, jnp.zeros_like(q_hi)], axis=0).astype(BF16)
    pw = _dot(w_ref[2 * A_WIDTH + 256:2 * A_WIDTH + 264, :], hb)
    wt_ref[...] = pw * (IDX_HEADS ** -0.5)


def _inproj_t(xt, g_col, sc_col, sh_col, w_t, cos_t, sin_t, seq):
    d, t = xt.shape
    tn = 512
    per_b = seq // tn
    rows = w_t.shape[0]
    return pl.pallas_call(
        _inproj_t_kernel,
        out_shape=(jax.ShapeDtypeStruct((A_WIDTH, t), BF16),
                   jax.ShapeDtypeStruct((t // DSA_KB, A_WIDTH, DSA_KB), BF16),
                   jax.ShapeDtypeStruct((IDX_HEADS * IDX_K, t), BF16),
                   jax.ShapeDtypeStruct((SUBLANES, t), F32)),
        grid=(t // tn,),
        in_specs=[pl.BlockSpec((d, tn), lambda i: (0, i)),
                  pl.BlockSpec((d, 1), lambda i: (0, 0)),
                  pl.BlockSpec((1, d, 1), lambda i: (i // per_b, 0, 0)),
                  pl.BlockSpec((1, d, 1), lambda i: (i // per_b, 0, 0)),
                  pl.BlockSpec((rows, d), lambda i: (0, 0)),
                  pl.BlockSpec((ROT_HALF, tn), lambda i: (0, i)),
                  pl.BlockSpec((ROT_HALF, tn), lambda i: (0, i))],
        out_specs=(pl.BlockSpec((A_WIDTH, tn), lambda i: (0, i)),
                   pl.BlockSpec((tn // DSA_KB, A_WIDTH, DSA_KB), lambda i: (i, 0, 0)),
                   pl.BlockSpec((IDX_HEADS * IDX_K, tn), lambda i: (0, i)),
                   pl.BlockSpec((SUBLANES, tn), lambda i: (0, i))),
        compiler_params=pltpu.CompilerParams(vmem_limit_bytes=VMEM_LIMIT),
        name="inproj_t",
    )(xt, g_col, sc_col, sh_col, w_t, cos_t, sin_t)


def _inproj_r_kernel(x_ref, g_ref, sc_ref, sh_ref, w_ref, c_ref, s_ref,
                     ka_ref, ki_ref, qb_ref, fb_ref, ib_ref, gb_ref):
    x = x_ref[...]
    ms = jnp.mean(x * x, axis=-1, keepdims=True)
    h = x * lax.rsqrt(ms + RMS_EPS) * g_ref[...]
    h = h * (1.0 + sc_ref[0]) + sh_ref[0]
    hb = h.astype(BF16)
    c = c_ref[...]
    s = s_ref[...]
    lane = lax.broadcasted_iota(I32, c.shape, 1)
    first = (lane % A_HEAD_DIM) < ROT_HALF

    def rope(p):
        partner = jnp.where(first, pltpu.roll(p, LANES - ROT_HALF, 1), pltpu.roll(p, ROT_HALF, 1))
        return p * c + partner * s

    for j in range(A_WIDTH // LANES):
        p = _dot(hb, w_ref[:, j * LANES:(j + 1) * LANES])
        ka_ref[:, j * LANES:(j + 1) * LANES] = rope(p).astype(BF16)
    o = A_WIDTH
    k_hi, k_lo = _split_bf16(rope(_dot(hb, w_ref[:, o:o + LANES])))
    ki_ref[:, 0:LANES] = (k_hi + pltpu.roll(k_lo, IDX_DIM, 1)).astype(BF16)
    ki_ref[:, LANES:2 * LANES] = k_hi.astype(BF16)
    o += LANES
    qb_ref[...] = _dot(hb, w_ref[:, o:o + 256])
    o += 256
    fb_ref[...] = _dot(hb, w_ref[:, o:o + 256])
    o += 256
    ib_ref[...] = _dot(hb, w_ref[:, o:o + 512])
    o += 512
    gb_ref[...] = _dot(hb, w_ref[:, o:o + 512])


def _inproj_r(x2, g_row, sc_row, sh_row, w_r, c_tab, s_tab, seq):
    t, d = x2.shape
    tm = 512
    per_b = seq // tm
    cols = w_r.shape[1]
    widths = (A_WIDTH, IDX_K, 256, 256, 512, 512)
    dts = (BF16, BF16, F32, F32, F32, F32)
    return pl.pallas_call(
        _inproj_r_kernel,
        out_shape=tuple(jax.ShapeDtypeStruct((t, w), dt) for w, dt in zip(widths, dts)),
        grid=(t // tm,),
        in_specs=[pl.BlockSpec((tm, d), lambda i: (i, 0)),
                  pl.BlockSpec((1, d), lambda i: (0, 0)),
                  pl.BlockSpec((1, 1, d), lambda i: (i // per_b, 0, 0)),
                  pl.BlockSpec((1, 1, d), lambda i: (i // per_b, 0, 0)),
                  pl.BlockSpec((d, cols), lambda i: (0, 0)),
                  pl.BlockSpec((tm, LANES), lambda i: (i, 0)),
                  pl.BlockSpec((tm, LANES), lambda i: (i, 0))],
        out_specs=tuple(pl.BlockSpec((tm, w), lambda i: (i, 0)) for w in widths),
        compiler_params=pltpu.CompilerParams(vmem_limit_bytes=VMEM_LIMIT),
        name="inproj_r",
    )(x2, g_row, sc_row, sh_row, w_r, c_tab, s_tab)


def _dsa_kernel(qi_ref, wt_ref, ki_ref, qa_ref, ka_ref, v_ref, o_ref,
                keys_ref, m_ref, l_ref, acc_ref, s_ref, p_ref):
    j = pl.program_id(1)
    nkb = j + 1
    tq = DSA_TQ
    kb_rows = DSA_KB
    row = lax.broadcasted_iota(I32, (kb_rows, tq), 0)
    col = lax.broadcasted_iota(I32, (kb_rows, tq), 1)
    q_chunk = (j * tq + col) // CHUNK

    def f32_to_key(x):
        bits = pltpu.bitcast(x, I32)
        return jnp.where(bits < 0, bits ^ 0x7FFFFFFF, bits)

    def key_to_f32(k):
        return pltpu.bitcast(jnp.where(k < 0, k ^ 0x7FFFFFFF, k), F32)

    def score_keys(kb):
        r0 = pl.multiple_of(kb * kb_rows, kb_rows)
        ki = ki_ref[pl.ds(r0, kb_rows), :]
        for h in range(IDX_HEADS):
            s_ref[h] = _dot(ki, qi_ref[h * IDX_K:(h + 1) * IDX_K, :])
        sc = jnp.zeros((kb_rows, tq), F32)
        for h in range(IDX_HEADS):
            sc = sc + wt_ref[h:h + 1, :] * jnp.maximum(s_ref[h], 0.0)
        return f32_to_key(jnp.where(sc == 0.0, 0.0, sc))

    def score_block(kb, carry):
        kmin, kmax = carry
        key = score_keys(kb)
        keys_ref[kb] = key
        return (jnp.minimum(kmin, jnp.min(key, axis=0, keepdims=True)),
                jnp.maximum(kmax, jnp.max(key, axis=0, keepdims=True)))

    kmin, kmax = lax.fori_loop(0, j, score_block,
                               (jnp.full((1, tq), INT_MAX, I32), jnp.full((1, tq), INT_MIN, I32)))
    adm = ((j * kb_rows + row) // CHUNK) <= q_chunk
    key = score_keys(j)
    keys_ref[j] = jnp.where(adm, key, INT_MIN)
    kmin = jnp.minimum(kmin, jnp.min(jnp.where(adm, key, INT_MAX), axis=0, keepdims=True))
    kmax = jnp.maximum(kmax, jnp.max(jnp.where(adm, key, INT_MIN), axis=0, keepdims=True))

    def count(pred_fn):
        def body(kb, acc):
            return acc + jnp.sum(jnp.where(pred_fn(keys_ref[kb], kb), 1, 0), axis=0, keepdims=True)
        return lax.fori_loop(0, nkb, body, jnp.zeros((1, tq), I32))

    n_adm = ((j * tq + col[0:1, :]) // CHUNK + 1) * CHUNK
    searching = n_adm > TOPK_MAX

    def closed(lo, hi, c_lo):
        return jnp.logical_or(hi == lo + 1, c_lo == TOPK_MAX)

    def search_step(st):
        lo, hi, c_lo, c_hi, it, _ = st
        done = closed(lo, hi, c_lo)
        mid = (lo >> 1) + (hi >> 1) + (lo & hi & 1)
        f_lo = key_to_f32(lo)
        frac = ((c_lo - TOPK_MAX).astype(F32) + 0.5) / (c_lo - c_hi).astype(F32)
        guess = f32_to_key(f_lo + (key_to_f32(hi) - f_lo) * frac)
        cand = jnp.where(it % 2 == 1, mid, guess)
        cand = jnp.minimum(jnp.maximum(cand, lo + 1), hi - 1)
        cand = jnp.where(done, lo, cand)
        cnt = count(lambda k, kb: k >= cand)
        ok = cnt >= TOPK_MAX
        lo, c_lo = jnp.where(ok, cand, lo), jnp.where(ok, cnt, c_lo)
        hi, c_hi = jnp.where(ok, hi, cand), jnp.where(ok, c_hi, cnt)
        active = jnp.max(jnp.where(closed(lo, hi, c_lo), 0, 1))
        return lo, hi, c_lo, c_hi, it + 1, active

    lo0 = jnp.where(searching, kmin, INT_MIN)
    hi0 = jnp.where(searching, kmax + 1, INT_MIN + 1)
    thr = lax.while_loop(
        lambda st: jnp.logical_and(st[5] > 0, st[4] < 2 * 32 + 2), search_step,
        (lo0, hi0, n_adm, jnp.zeros((1, tq), I32), jnp.int32(0), jnp.int32(1)))[0]
    n_gt = count(lambda k, kb: k > thr)
    n_eq = count(lambda k, kb: k == thr)
    need = TOPK_MAX - n_gt
    live = thr != INT_MIN
    excess = jnp.logical_and(n_gt + n_eq > TOPK_MAX, live)
    any_excess = jnp.max(jnp.where(excess, 1, 0)) > 0

    def idx_search():
        def step(i, p):
            cand = p | lax.shift_left(jnp.int32(1), 11 - i)
            cnt = count(lambda k, kb: jnp.logical_and(k == thr, kb * kb_rows + row < cand))
            return jnp.where(cnt < need, cand, p)
        return lax.fori_loop(0, 12, step, jnp.zeros((1, tq), I32))

    p_full = jnp.full((1, tq), 4095, I32)
    p_idx = lax.cond(any_excess, idx_search, lambda: p_full)
    p_idx = jnp.where(live, p_idx, -1)

    m_ref[...] = jnp.full(m_ref.shape, NEG_BIG, F32)
    l_ref[...] = jnp.zeros(l_ref.shape, F32)
    acc_ref[...] = jnp.zeros(acc_ref.shape, F32)

    def attn_block(kb, carry):
        r0 = pl.multiple_of(kb * kb_rows, kb_rows)
        key = keys_ref[kb]
        sel = jnp.logical_or(key > thr,
                             jnp.logical_and(key == thr, kb * kb_rows + row <= p_idx))
        bias = jnp.where(sel, 0.0, NEG_BIG)
        heads = [slice(h * A_HEAD_DIM, (h + 1) * A_HEAD_DIM) for h in range(A_HEADS)]
        for h, hs in enumerate(heads):
            s_ref[h] = _dot(ka_ref[pl.ds(r0, kb_rows), hs], qa_ref[hs, :])
        alphas = []
        for h, hs in enumerate(heads):
            s = s_ref[h] + bias
            m_old = m_ref[h][0:1, :]
            m_new = jnp.maximum(m_old, jnp.max(s, axis=0, keepdims=True))
            alpha = jnp.exp2(m_old - m_new)
            p = jnp.exp2(s - m_new)
            l_new = alpha * l_ref[h][0:1, :] + jnp.sum(p, axis=0, keepdims=True)
            p_ref[h] = p.astype(BF16)
            m_ref[h] = jnp.broadcast_to(m_new, (SUBLANES, tq))
            l_ref[h] = jnp.broadcast_to(l_new, (SUBLANES, tq))
            alphas.append(alpha)
        for h, hs in enumerate(heads):
            acc_ref[hs, :] = alphas[h] * acc_ref[hs, :] + _dot(v_ref[kb, hs, :], p_ref[h])
        return carry

    lax.fori_loop(0, nkb, attn_block, 0)

    for h in range(A_HEADS):
        hs = slice(h * A_HEAD_DIM, (h + 1) * A_HEAD_DIM)
        acc_ref[hs, :] = acc_ref[hs, :] / l_ref[h][0:1, :]
    o_ref[...] = acc_ref[...].T


def _dsa(qi_t, w_t, ki, qa_t, ka, v_blk, batch, seq):
    t = ka.shape[0]
    nqb = seq // DSA_TQ
    nkb = seq // DSA_KB
    return pl.pallas_call(
        _dsa_kernel,
        out_shape=jax.ShapeDtypeStruct((t, A_WIDTH), F32),
        grid=(batch, nqb),
        in_specs=[pl.BlockSpec((IDX_HEADS * IDX_K, DSA_TQ), lambda b, j: (0, b * nqb + j)),
                  pl.BlockSpec((SUBLANES, DSA_TQ), lambda b, j: (0, b * nqb + j)),
                  pl.BlockSpec((seq, IDX_K), lambda b, j: (b, 0)),
                  pl.BlockSpec((A_WIDTH, DSA_TQ), lambda b, j: (0, b * nqb + j)),
                  pl.BlockSpec((seq, A_WIDTH), lambda b, j: (b, 0)),
                  pl.BlockSpec((nkb, A_WIDTH, DSA_KB), lambda b, j: (b, 0, 0))],
        out_specs=pl.BlockSpec((DSA_TQ, A_WIDTH), lambda b, j: (b * nqb + j, 0)),
        scratch_shapes=[pltpu.VMEM((nkb, DSA_KB, DSA_TQ), I32),
                        pltpu.VMEM((A_HEADS, SUBLANES, DSA_TQ), F32),
                        pltpu.VMEM((A_HEADS, SUBLANES, DSA_TQ), F32),
                        pltpu.VMEM((A_WIDTH, DSA_TQ), F32),
                        pltpu.VMEM((A_HEADS, DSA_KB, DSA_TQ), F32),
                        pltpu.VMEM((A_HEADS, DSA_KB, DSA_TQ), BF16)],
        compiler_params=pltpu.CompilerParams(vmem_limit_bytes=VMEM_LIMIT),
        name="dsa",
    )(qi_t, w_t, ki, qa_t, ka, v_blk)


def _hgrn_constants():
    c = CHUNK
    t = np.arange(c)[:, None]
    u = np.arange(c)[None, :]
    blocks = [(u <= t), (u > t)]
    masks = []
    for half in HG_LEVELS:
        mid = (t // (2 * half)) * (2 * half) + half - 1
        right = ((t // half) % 2) == 1
        blocks.append(right & (u > mid) & (u <= t))
        blocks.append((~right) & (u > t) & (u <= mid))
        tt, ss = t, u
        masks.append(((tt // (2 * half)) == (ss // (2 * half)))
                     & ((((tt // half) % 2) == 1) & (((ss // half) % 2) == 0)))
    masks.append(t == u)
    cm = np.concatenate(blocks, axis=0).astype(np.float32)
    cm3 = np.concatenate([cm, cm, cm], axis=1)
    masks4 = np.tile(np.stack(masks).astype(np.float32), (1, 1, B_HEADS))
    head_of = lambda n, per: np.arange(n) // per
    bd_k = (head_of(B_HEADS * c, c)[:, None] == head_of(B_HEADS * B_KEY_DIM, B_KEY_DIM)[None, :])
    bd_v = (head_of(B_HEADS * c, c)[:, None] == head_of(B_WIDTH, B_VAL_DIM)[None, :])
    bd_s = (head_of(B_WIDTH, B_VAL_DIM)[:, None] == head_of(B_HEADS * B_KEY_DIM, B_KEY_DIM)[None, :])
    return cm3, masks4, bd_k.astype(np.float32), bd_v.astype(np.float32), bd_s.astype(np.float32)


def _hgrn_kernel(qb_ref, fb_ref, ib_ref, gb_ref, lb_ref, hg_ref, cm_ref, mask_ref, bdk_ref, bdv_ref,
                 bds_ref, eye_ref, o_ref, state_ref, lv_ref):
    @pl.when(pl.program_id(1) == 0)
    def _():
        state_ref[...] = jnp.zeros(state_ref.shape, F32)

    lb = lb_ref[...]
    c = CHUNK
    nlev = len(HG_LEVELS)

    def stack_heads(x, bd_ref):
        return (jnp.concatenate([x] * B_HEADS, axis=0) * bd_ref[...]).astype(BF16)

    def chunk_step(ci, carry):
        r0 = pl.multiple_of(ci * c, c)
        f = lb + (1.0 - lb) * _sigmoid(fb_ref[pl.ds(r0, c), :])
        lf = jnp.log(f)
        kin = 1.0 - f
        hi = lf.astype(BF16)
        r1 = lf - hi.astype(F32)
        mid = r1.astype(BF16)
        lo = (r1 - mid.astype(F32)).astype(BF16)
        e = jnp.exp(_dot(cm_ref[...], jnp.concatenate([hi, mid, lo], axis=0)))
        q = qb_ref[pl.ds(r0, c), :]
        q_in = (q * e[0:c]).astype(BF16)
        k_out = (kin * e[c:2 * c]).astype(BF16)
        e_last = e[c - 1:c, :]
        for i in range(nlev + 1):
            if i < nlev:
                q_i = q * e[(2 + 2 * i) * c:(3 + 2 * i) * c]
                k_i = kin * e[(3 + 2 * i) * c:(4 + 2 * i) * c]
            else:
                q_i, k_i = q, kin
            lv_ref[i] = _dot_nt(q_i.astype(BF16), stack_heads(k_i, bdk_ref))
        attn = mask_ref[0] * lv_ref[0]
        for i in range(1, nlev + 1):
            attn = attn + mask_ref[i] * lv_ref[i]
        v = ib_ref[pl.ds(r0, c), :]
        st = state_ref[...]
        o_all = _dot_nt(q_in, st.astype(BF16)) + _dot(attn.astype(BF16), stack_heads(v, bdv_ref))
        v_t = _dot_nt(eye_ref[...], v.astype(BF16)).astype(BF16)
        state_ref[...] = st * e_last + bds_ref[...] * _dot(v_t, k_out)
        for h in range(B_HEADS):
            vs = slice(h * B_VAL_DIM, (h + 1) * B_VAL_DIM)
            o = o_all[:, vs]
            ms = jnp.mean(o * o, axis=-1, keepdims=True)
            y = o * lax.rsqrt(ms + RMS_EPS) * hg_ref[...]
            g = gb_ref[pl.ds(r0, c), vs]
            o_ref[pl.ds(r0, c), vs] = y * (g * _sigmoid(g))
        return carry

    lax.fori_loop(0, HG_STEP_CHUNKS, chunk_step, 0)


def _hgrn(qb, fb, ib, gb, lb_row, hg_row, batch, seq):
    t = qb.shape[0]
    tm = HG_STEP_CHUNKS * CHUNK
    per_b = seq // tm
    cm3, masks4, bd_k, bd_v, bd_s = _hgrn_constants()
    consts = (jnp.asarray(cm3, BF16), jnp.asarray(masks4, F32), jnp.asarray(bd_k, F32),
              jnp.asarray(bd_v, F32), jnp.asarray(bd_s, F32), jnp.eye(B_WIDTH, dtype=BF16))
    kd = B_HEADS * B_KEY_DIM
    tok_spec = lambda w: pl.BlockSpec((tm, w), lambda b, i: (b * per_b + i, 0))
    full = lambda a: pl.BlockSpec(a.shape, lambda b, i: (0,) * a.ndim)
    return pl.pallas_call(
        _hgrn_kernel,
        out_shape=jax.ShapeDtypeStruct((t, B_WIDTH), F32),
        grid=(batch, per_b),
        in_specs=[tok_spec(kd), tok_spec(kd), tok_spec(B_WIDTH), tok_spec(B_WIDTH),
                  full(lb_row), full(hg_row)] + [full(a) for a in consts],
        out_specs=tok_spec(B_WIDTH),
        scratch_shapes=[pltpu.VMEM((B_WIDTH, kd), F32),
                        pltpu.VMEM((len(HG_LEVELS) + 1, CHUNK, B_HEADS * CHUNK), F32)],
        compiler_params=pltpu.CompilerParams(vmem_limit_bytes=VMEM_LIMIT),
        name="hgrn",
    )(qb, fb, ib, gb, lb_row, hg_row, *consts)


def _outproj_kernel(oa_ref, ob_ref, x_ref, g1_ref, wo_ref, n2_ref, sc_ref, sh_ref,
                    rw_ref, rb_ref, tri_ref,
                    x1_ref, h2_ref, idx_ref, gate_ref, rank_ref, cnt_ref, run_ref):
    @pl.when(pl.program_id(0) == 0)
    def _():
        run_ref[...] = jnp.zeros(run_ref.shape, F32)

    half = A_WIDTH
    mix = _dot(oa_ref[...].astype(BF16), wo_ref[0:half, :]) + _dot(ob_ref[...].astype(BF16), wo_ref[half:, :])
    x1 = x_ref[...] + g1_ref[0] * mix
    x1_ref[...] = x1
    ms = jnp.mean(x1 * x1, axis=-1, keepdims=True)
    h2 = x1 * lax.rsqrt(ms + RMS_EPS) * n2_ref[...]
    h2 = h2 * (1.0 + sc_ref[0]) + sh_ref[0]
    _store_token_tiles(h2_ref, h2)
    logits = _dot(h2, rw_ref[...], HIGHEST) + rb_ref[...]
    tm = logits.shape[0]
    lane = lax.broadcasted_iota(I32, (tm, LANES), 1)
    work = logits
    vals, idxs = [], []
    for _ in range(TOP_K_EXPERTS):
        m = jnp.max(work, axis=-1, keepdims=True)
        ix = jnp.min(jnp.where(work == m, lane, LANES), axis=-1, keepdims=True)
        vals.append(m)
        idxs.append(ix)
        work = jnp.where(lane == ix, -jnp.inf, work)
    es = [jnp.exp(v - vals[0]) for v in vals]
    tot = es[0] + es[1] + es[2] + es[3]
    onehot = jnp.zeros((tm, LANES), F32)
    idx_out = jnp.zeros((tm, LANES), I32)
    gate_out = jnp.zeros((tm, LANES), F32)
    for k in range(TOP_K_EXPERTS):
        onehot = onehot + jnp.where(lane == idxs[k], 1.0, 0.0)
        idx_out = jnp.where(lane == k, idxs[k], idx_out)
        gate_out = jnp.where(lane == k, es[k] / tot, gate_out)
    idx_ref[...] = idx_out
    gate_ref[...] = gate_out
    before = _dot(tri_ref[...], onehot.astype(BF16)) + run_ref[...]
    rank_out = jnp.zeros((tm, LANES), F32)
    for k in range(TOP_K_EXPERTS):
        rk = jnp.sum(jnp.where(lane == idxs[k], before, 0.0), axis=-1, keepdims=True)
        rank_out = jnp.where(lane == k, rk, rank_out)
    rank_ref[...] = rank_out.astype(I32)
    run = run_ref[...] + jnp.sum(onehot, axis=0, keepdims=True)
    run_ref[...] = run
    cnt_ref[...] = run.astype(I32)


def _outproj(oa, ob, x2, g1, wo, n2_row, sc2, sh2, rw, rb, seq):
    t, d = x2.shape
    tm = 512
    per_b = seq // tm
    tri = jnp.asarray(np.tril(np.ones((tm, tm), np.float32), -1), BF16)
    row_spec = lambda w: pl.BlockSpec((tm, w), lambda i: (i, 0))
    mod_spec = pl.BlockSpec((1, 1, d), lambda i: (i // per_b, 0, 0))
    full = lambda a: pl.BlockSpec(a.shape, lambda i: (0,) * a.ndim)
    return pl.pallas_call(
        _outproj_kernel,
        out_shape=(jax.ShapeDtypeStruct((t, d), F32), jax.ShapeDtypeStruct((t, ROW_TILES, LANES), F32),
                   jax.ShapeDtypeStruct((t, LANES), I32), jax.ShapeDtypeStruct((t, LANES), F32),
                   jax.ShapeDtypeStruct((t, LANES), I32), jax.ShapeDtypeStruct((1, LANES), I32)),
        grid=(t // tm,),
        in_specs=[row_spec(A_WIDTH), row_spec(B_WIDTH), row_spec(d), mod_spec, full(wo),
                  full(n2_row), mod_spec, mod_spec, full(rw), full(rb), full(tri)],
        out_specs=(row_spec(d), pl.BlockSpec((tm, ROW_TILES, LANES), lambda i: (i, 0, 0)),
                   row_spec(LANES), row_spec(LANES), row_spec(LANES),
                   pl.BlockSpec((1, LANES), lambda i: (0, 0))),
        scratch_shapes=[pltpu.VMEM((1, LANES), F32)],
        compiler_params=pltpu.CompilerParams(dimension_semantics=("arbitrary",),
                                             vmem_limit_bytes=VMEM_LIMIT),
        name="outproj",
    )(oa, ob, x2, g1, wo, n2_row, sc2, sh2, rw, rb, tri)


DISPATCH_TOKENS = 256
WAIT_UNROLL = 16


def _drain(make_copy, n):
    def body(g, carry):
        for _ in range(WAIT_UNROLL):
            make_copy().wait()
        return carry

    lax.fori_loop(0, n // WAIT_UNROLL, body, 0)


def _dispatch_kernel(rows_ref, h2_ref, zero_hbm, out_hbm, sem):
    del zero_hbm
    n = DISPATCH_TOKENS * TOP_K_EXPERTS

    def issue(t, carry):
        for k in range(TOP_K_EXPERTS):
            pltpu.make_async_copy(h2_ref.at[t], out_hbm.at[rows_ref[t * TOP_K_EXPERTS + k]],
                                  sem).start(priority=k % 2)
        return carry

    lax.fori_loop(0, DISPATCH_TOKENS, issue, 0)
    _drain(lambda: pltpu.make_async_copy(h2_ref.at[0], out_hbm.at[0], sem), n)


def _dispatch(rows_flat, h2, n_rows):
    t = h2.shape[0]
    n = DISPATCH_TOKENS * TOP_K_EXPERTS
    zeros = jnp.zeros((n_rows, ROW_TILES, LANES), F32)
    return pl.pallas_call(
        _dispatch_kernel,
        out_shape=jax.ShapeDtypeStruct((n_rows, ROW_TILES, LANES), F32),
        grid=(t // DISPATCH_TOKENS,),
        in_specs=[pl.BlockSpec((n,), lambda i: (i,), memory_space=pltpu.SMEM),
                  pl.BlockSpec((DISPATCH_TOKENS, ROW_TILES, LANES), lambda i: (i, 0, 0)),
                  pl.BlockSpec(memory_space=pl.ANY)],
        out_specs=pl.BlockSpec(memory_space=pl.ANY),
        scratch_shapes=[pltpu.SemaphoreType.DMA(())],
        input_output_aliases={2: 0},
        name="dispatch",
    )(rows_flat, h2, zeros)


def _experts_kernel(be_ref, nu_ref, x_ref, w1_ref, b1_ref, w2_ref, b2_ref, y_ref, w1b_ref, w2b_ref):
    i = pl.program_id(0)
    prev = be_ref[jnp.maximum(i - 1, 0)]
    changed = jnp.logical_or(i == 0, be_ref[i] != prev)
    used = i < nu_ref[0]

    @pl.when(jnp.logical_and(changed, used))
    def _():
        w1b_ref[...] = w1_ref[0].astype(BF16)
        w2b_ref[...] = w2_ref[0].astype(BF16)

    @pl.when(used)
    def _():
        f = w2b_ref.shape[0]
        hg = _dot(_load_token_tiles(x_ref).astype(BF16), w1b_ref[...]) + b1_ref[0]
        glu = jnp.minimum(hg[:, :f], SWIGLU_LIMIT)
        lin = jnp.clip(hg[:, f:], -SWIGLU_LIMIT, SWIGLU_LIMIT)
        act = glu * _sigmoid(SWIGLU_ALPHA * glu) * (lin + 1.0)
        _store_token_tiles(y_ref, _dot(act.astype(BF16), w2b_ref[...]) + b2_ref[0])

    @pl.when(jnp.logical_not(used))
    def _():
        y_ref[...] = jnp.zeros(y_ref.shape, F32)


def _experts(block_expert, n_used, x_rows, w1, b1, w2, b2):
    n_rows = x_rows.shape[0]
    e, d, f2 = w1.shape
    f = w2.shape[1]
    nb = n_rows // EXPERT_BLOCK
    tok_spec = pl.BlockSpec((EXPERT_BLOCK, ROW_TILES, LANES), lambda i, be, nu: (i, 0, 0))
    return pl.pallas_call(
        _experts_kernel,
        out_shape=jax.ShapeDtypeStruct((n_rows, ROW_TILES, LANES), F32),
        grid_spec=pltpu.PrefetchScalarGridSpec(
            num_scalar_prefetch=2,
            grid=(nb,),
            in_specs=[tok_spec,
                      pl.BlockSpec((1, d, f2), lambda i, be, nu: (be[i], 0, 0)),
                      pl.BlockSpec((1, 1, f2), lambda i, be, nu: (be[i], 0, 0)),
                      pl.BlockSpec((1, f, d), lambda i, be, nu: (be[i], 0, 0)),
                      pl.BlockSpec((1, 1, d), lambda i, be, nu: (be[i], 0, 0))],
            out_specs=tok_spec,
            scratch_shapes=[pltpu.VMEM((d, f2), BF16), pltpu.VMEM((f, d), BF16)]),
        compiler_params=pltpu.CompilerParams(dimension_semantics=("arbitrary",),
                                             vmem_limit_bytes=VMEM_LIMIT),
        name="experts",
    )(block_expert, n_used, x_rows, w1, b1.reshape(e, 1, f2), w2, b2.reshape(e, 1, d))


COMBINE_TOKENS = 256


def _combine_kernel(rows_ref, y_hbm, gate_ref, x1_ref, g2_ref, fg_ref, o_ref, buf_ref, sem):
    n = COMBINE_TOKENS * TOP_K_EXPERTS

    def issue(t, carry):
        for k in range(TOP_K_EXPERTS):
            pltpu.make_async_copy(y_hbm.at[rows_ref[t * TOP_K_EXPERTS + k]], buf_ref.at[k, t],
                                  sem).start(priority=k % 2)
        return carry

    lax.fori_loop(0, COMBINE_TOKENS, issue, 0)
    _drain(lambda: pltpu.make_async_copy(y_hbm.at[0], buf_ref.at[0, 0], sem), n)

    gates = gate_ref[...]
    moe = gates[:, 0:1] * _load_token_tiles(buf_ref.at[0])
    for k in range(1, TOP_K_EXPERTS):
        moe = moe + gates[:, k:k + 1] * _load_token_tiles(buf_ref.at[k])
    x2 = x1_ref[...] + g2_ref[0] * moe
    ms = jnp.mean(x2 * x2, axis=-1, keepdims=True)
    o_ref[...] = x2 * lax.rsqrt(ms + RMS_EPS) * fg_ref[...]


def _combine(rows_flat, y_rows, gates, x1, g2, fg_row, seq):
    t, d = x1.shape
    tm = COMBINE_TOKENS
    per_b = seq // tm
    n = tm * TOP_K_EXPERTS
    return pl.pallas_call(
        _combine_kernel,
        out_shape=jax.ShapeDtypeStruct((t, d), F32),
        grid=(t // tm,),
        in_specs=[pl.BlockSpec((n,), lambda i: (i,), memory_space=pltpu.SMEM),
                  pl.BlockSpec(memory_space=pl.ANY),
                  pl.BlockSpec((tm, LANES), lambda i: (i, 0)),
                  pl.BlockSpec((tm, d), lambda i: (i, 0)),
                  pl.BlockSpec((1, 1, d), lambda i: (i // per_b, 0, 0)),
                  pl.BlockSpec((1, d), lambda i: (0, 0))],
        out_specs=pl.BlockSpec((tm, d), lambda i: (i, 0)),
        scratch_shapes=[pltpu.VMEM((TOP_K_EXPERTS, tm, ROW_TILES, LANES), F32), pltpu.SemaphoreType.DMA(())],
        compiler_params=pltpu.CompilerParams(vmem_limit_bytes=VMEM_LIMIT),
        name="combine",
    )(rows_flat, y_rows, gates, x1, g2, fg_row)


def kernel(x, c, positions, ada_w, ada_b, norm1_g, w_in, hg_norm_g, lb_logits, w_out, norm2_g,
           router_w, router_b, moe_w1, moe_b1, moe_w2, moe_b2, final_g):
    batch, seq, d = x.shape
    t = batch * seq
    layer = 0
    x2 = x.reshape(t, d)

    c_pad = jnp.concatenate([c, jnp.zeros((SUBLANES - batch, d), F32)], axis=0)
    mod = _adaln(c_pad, ada_w[layer], ada_b[layer][None, :])[:batch]
    shift1, scale1, gate1, shift2, scale2, gate2 = jnp.split(mod, 6, axis=-1)
    row3 = lambda m: m[:, None, :]
    col3 = lambda m: m[:, :, None]

    inv_freq = ROPE_THETA ** (-(jnp.arange(0, ROT_DIM, 2, dtype=F32) / ROT_DIM))
    cos_t, sin_t = _trig(positions.reshape(1, t).astype(F32), inv_freq[:, None])
    ones = jnp.ones((t, A_HEAD_DIM - ROT_DIM), F32)
    c64 = jnp.concatenate([cos_t.T, cos_t.T, ones], axis=1)
    s64 = jnp.concatenate([-sin_t.T, sin_t.T, 0.0 * ones], axis=1)
    c_tab = jnp.concatenate([c64, c64], axis=1)
    s_tab = jnp.concatenate([s64, s64], axis=1)

    wl = w_in[layer]
    sp = np.cumsum((A_WIDTH, A_WIDTH, A_WIDTH, IDX_HEADS * IDX_DIM, IDX_DIM, IDX_HEADS,
                    B_HEADS * B_KEY_DIM, B_HEADS * B_KEY_DIM, B_WIDTH))
    w_qa, w_ka, w_va, w_qi, w_ki, w_wi, w_qb, w_fb, w_ib, w_gb = jnp.split(wl, [int(v) for v in sp], axis=1)
    w_t = jnp.concatenate([w_qa, w_va, w_qi, w_wi, jnp.zeros((d, SUBLANES - IDX_HEADS), F32)], axis=1)
    w_t = w_t.T.astype(BF16)
    w_r = jnp.concatenate([w_ka, w_ki, jnp.zeros((d, LANES - IDX_DIM), F32), w_qb, w_fb, w_ib, w_gb],
                          axis=1).astype(BF16)

    g1n = norm1_g[layer]
    qa_t, v_blk, qi_t, wi_t = _inproj_t(x2.T, g1n[:, None], col3(scale1), col3(shift1),
                                        w_t, cos_t, sin_t, seq)
    ka, ki, qb, fb, ib, gb = _inproj_r(x2, g1n[None, :], row3(scale1), row3(shift1),
                                       w_r, c_tab, s_tab, seq)

    out_a = _dsa(qi_t, wi_t, ki, qa_t, ka, v_blk, batch, seq)

    lower = jnp.cumsum(jax.nn.softmax(lb_logits.astype(F32), axis=0), axis=0)[layer]
    out_b = _hgrn(qb, fb, ib, gb, lower[None, :], hg_norm_g[layer][None, :], batch, seq)

    rw = jnp.concatenate([router_w[layer], jnp.zeros((d, LANES - N_EXPERTS), F32)], axis=1)
    rb = jnp.concatenate([router_b[layer], jnp.full((LANES - N_EXPERTS,), NEG_BIG, F32)])[None, :]
    x1, h2, idx, gates, rank, counts = _outproj(
        out_a, out_b, x2, row3(gate1), w_out[layer].astype(BF16), norm2_g[layer][None, :],
        row3(scale2), row3(shift2), rw, rb, seq)

    counts = counts[0, :N_EXPERTS]
    padded = (counts + EXPERT_BLOCK - 1) // EXPERT_BLOCK * EXPERT_BLOCK
    pad_end = jnp.cumsum(padded)
    pad_start = pad_end - padded
    n_assign = t * TOP_K_EXPERTS
    n_blocks = -(-n_assign // EXPERT_BLOCK) + N_EXPERTS
    n_rows = n_blocks * EXPERT_BLOCK
    block_start = jnp.arange(n_blocks, dtype=I32) * EXPERT_BLOCK
    block_expert = jnp.minimum(jnp.sum(pad_end[None, :] <= block_start[:, None], axis=1),
                               N_EXPERTS - 1).astype(I32)
    n_used = (pad_end[-1:] // EXPERT_BLOCK).astype(I32)
    rows_flat = (pad_start[idx[:, :TOP_K_EXPERTS]] + rank[:, :TOP_K_EXPERTS]).astype(I32).reshape(-1)

    x_rows = _dispatch(rows_flat, h2, n_rows)
    y_rows = _experts(block_expert, n_used, x_rows, moe_w1[layer], moe_b1[layer],
                      moe_w2[layer], moe_b2[layer])
    out = _combine(rows_flat, y_rows, gates, x1, row3(gate2), final_g[None, :], seq)
    return out.reshape(batch, seq, d)
```

```python
import functools

import numpy as np
import jax
import jax.numpy as jnp
from jax import lax
from jax.experimental import pallas as pl
from jax.experimental.pallas import tpu as pltpu

F32 = jnp.float32
BF16 = jnp.bfloat16
I32 = jnp.int32
I16 = jnp.int16
HALF_OFFSET = 2 ** 15
HIGHEST = lax.Precision.HIGHEST

D_MODEL = 1024
CHUNK = 64
A_HEADS = 8
A_HEAD_DIM = 64
A_WIDTH = A_HEADS * A_HEAD_DIM
IDX_HEADS = 4
IDX_DIM = 64
TOPK_MAX = 256
B_HEADS = 4
B_KEY_DIM = 64
B_VAL_DIM = 128
B_WIDTH = B_HEADS * B_VAL_DIM
ROPE_THETA = 500000.0
ROT_DIM = A_HEAD_DIM // 4
ROT_HALF = ROT_DIM // 2
N_EXPERTS = 32
TOP_K_EXPERTS = 4
SWIGLU_LIMIT = 7.0
SWIGLU_ALPHA = 1.702
EXPERT_BLOCK = 256
RMS_EPS = 1e-6

LANES = 128
SUBLANES = 8
VMEM_LIMIT = 56 * 1024 * 1024

NEG_BIG = -1e30
INT_MIN = -(2 ** 31)
INT_MAX = 2 ** 31 - 1
LOG2E = 1.4426950408889634
IDX_K = 4 * IDX_DIM

DSA_TQ = 256
DSA_KB = 256
HG_STEP_CHUNKS = 8
HG_LEVELS = (32, 16, 8, 4, 2, 1)


def _dot(a, b, precision=None):
    return jnp.dot(a, b, preferred_element_type=F32, precision=precision)


def _dot_nt(a, b, precision=None):
    return lax.dot_general(a, b, (((1,), (1,)), ((), ())),
                           preferred_element_type=F32, precision=precision)


def _sigmoid(x):
    return 1.0 / (1.0 + jnp.exp(-x))


def _split_bf16(x):
    hi = x.astype(BF16).astype(F32)
    lo = (x - hi).astype(BF16).astype(F32)
    return hi, lo


ROW_TILES = D_MODEL // LANES


def _load_token_tiles(ref3):
    return jnp.concatenate([ref3[:, s, :] for s in range(ROW_TILES)], axis=1)


def _store_token_tiles(ref3, val):
    for s in range(ROW_TILES):
        ref3[:, s, :] = val[:, s * LANES:(s + 1) * LANES]


def _adaln_kernel(c_ref, w_ref, b_ref, o_ref):
    c = c_ref[...]
    o_ref[...] = _dot(c * _sigmoid(c), w_ref[...], HIGHEST) + b_ref[...]


def _adaln(c_pad, ada_w, ada_b):
    d = c_pad.shape[1]
    n = ada_w.shape[1]
    bn = 1024
    return pl.pallas_call(
        _adaln_kernel,
        out_shape=jax.ShapeDtypeStruct((c_pad.shape[0], n), F32),
        grid=(n // bn,),
        in_specs=[pl.BlockSpec((c_pad.shape[0], d), lambda j: (0, 0)),
                  pl.BlockSpec((d, bn), lambda j: (0, j)),
                  pl.BlockSpec((1, bn), lambda j: (0, j))],
        out_specs=pl.BlockSpec((c_pad.shape[0], bn), lambda j: (0, j)),
        name="adaln",
    )(c_pad, ada_w, ada_b)


def _trig_kernel(pos_ref, freq_ref, cos_ref, sin_ref):
    ang = pos_ref[...] * freq_ref[...]
    cos_ref[...] = jnp.cos(ang)
    sin_ref[...] = jnp.sin(ang)


def _trig(pos_row, freq_col):
    t = pos_row.shape[1]
    bt = 2048
    return pl.pallas_call(
        _trig_kernel,
        out_shape=(jax.ShapeDtypeStruct((ROT_HALF, t), F32),) * 2,
        grid=(t // bt,),
        in_specs=[pl.BlockSpec((1, bt), lambda i: (0, i)),
                  pl.BlockSpec((ROT_HALF, 1), lambda i: (0, 0))],
        out_specs=(pl.BlockSpec((ROT_HALF, bt), lambda i: (0, i)),) * 2,
        name="trig",
    )(pos_row, freq_col)


def _rope_rows(p, c, s):
    x1 = p[0:ROT_HALF]
    x2 = p[ROT_HALF:ROT_DIM]
    return jnp.concatenate([x1 * c - x2 * s, x2 * c + x1 * s, p[ROT_DIM:]], axis=0)


def _inproj_t_kernel(xt_ref, g_ref, sc_ref, sh_ref, w_ref, cos_ref, sin_ref,
                     qa_ref, v_ref, qi_ref, wt_ref):
    xt = xt_ref[...]
    ms = jnp.mean(xt * xt, axis=0, keepdims=True)
    h = xt * lax.rsqrt(ms + RMS_EPS) * g_ref[...]
    h = h * (1.0 + sc_ref[0]) + sh_ref[0]
    hb = h.astype(BF16)
    c = cos_ref[...]
    s = sin_ref[...]
    pq = _dot(w_ref[0:A_WIDTH, :], hb)
    for hh in range(A_HEADS):
        r = _rope_rows(pq[hh * 64:(hh + 1) * 64], c, s) * (A_HEAD_DIM ** -0.5 * LOG2E)
        qa_ref[hh * 64:(hh + 1) * 64, :] = r.astype(BF16)
    pv = _dot(w_ref[A_WIDTH:2 * A_WIDTH, :], hb)
    for cb in range(v_ref.shape[0]):
        v_ref[cb] = pv[:, cb * DSA_KB:(cb + 1) * DSA_KB].astype(BF16)
    pi = _dot(w_ref[2 * A_WIDTH:2 * A_WIDTH + 256, :], hb)
    for hh in range(IDX_HEADS):
        q = _rope_rows(pi[hh * 64:(hh + 1) * 64], c, s) * (IDX_DIM ** -0.5)
        q_hi, q_lo = _split_bf16(q)
        qi_ref[hh * IDX_K:(hh + 1) * IDX_K, :] = jnp.concatenate(
            [q_hi, q_hi, q_lo, jnp.zeros_like(q_hi)], axis=0).astype(BF16)
    pw = _dot(w_ref[2 * A_WIDTH + 256:2 * A_WIDTH + 264, :], hb)
    wt_ref[...] = pw * (IDX_HEADS ** -0.5)


def _inproj_t(xt, g_col, sc_col, sh_col, w_t, cos_t, sin_t, seq):
    d, t = xt.shape
    tn = 512
    per_b = seq // tn
    rows = w_t.shape[0]
    return pl.pallas_call(
        _inproj_t_kernel,
        out_shape=(jax.ShapeDtypeStruct((A_WIDTH, t), BF16),
                   jax.ShapeDtypeStruct((t // DSA_KB, A_WIDTH, DSA_KB), BF16),
                   jax.ShapeDtypeStruct((IDX_HEADS * IDX_K, t), BF16),
                   jax.ShapeDtypeStruct((SUBLANES, t), F32)),
        grid=(t // tn,),
        in_specs=[pl.BlockSpec((d, tn), lambda i: (0, i)),
                  pl.BlockSpec((d, 1), lambda i: (0, 0)),
                  pl.BlockSpec((1, d, 1), lambda i: (i // per_b, 0, 0)),
                  pl.BlockSpec((1, d, 1), lambda i: (i // per_b, 0, 0)),
                  pl.BlockSpec((rows, d), lambda i: (0, 0)),
                  pl.BlockSpec((ROT_HALF, tn), lambda i: (0, i)),
                  pl.BlockSpec((ROT_HALF, tn), lambda i: (0, i))],
        out_specs=(pl.BlockSpec((A_WIDTH, tn), lambda i: (0, i)),
                   pl.BlockSpec((tn // DSA_KB, A_WIDTH, DSA_KB), lambda i: (i, 0, 0)),
                   pl.BlockSpec((IDX_HEADS * IDX_K, tn), lambda i: (0, i)),
                   pl.BlockSpec((SUBLANES, tn), lambda i: (0, i))),
        compiler_params=pltpu.CompilerParams(vmem_limit_bytes=VMEM_LIMIT),
        name="inproj_t",
    )(xt, g_col, sc_col, sh_col, w_t, cos_t, sin_t)


def _inproj_r_kernel(x_ref, g_ref, sc_ref, sh_ref, w_ref, c_ref, s_ref,
                     ka_ref, ki_ref, qb_ref, fb_ref, ib_ref, gb_ref):
    x = x_ref[...]
    ms = jnp.mean(x * x, axis=-1, keepdims=True)
    h = x * lax.rsqrt(ms + RMS_EPS) * g_ref[...]
    h = h * (1.0 + sc_ref[0]) + sh_ref[0]
    hb = h.astype(BF16)
    c = c_ref[...]
    s = s_ref[...]
    lane = lax.broadcasted_iota(I32, c.shape, 1)
    first = (lane % A_HEAD_DIM) < ROT_HALF

    def rope(p):
        partner = jnp.where(first, pltpu.roll(p, LANES - ROT_HALF, 1), pltpu.roll(p, ROT_HALF, 1))
        return p * c + partner * s

    for j in range(A_WIDTH // LANES):
        p = _dot(hb, w_ref[:, j * LANES:(j + 1) * LANES])
        ka_ref[:, j * LANES:(j + 1) * LANES] = rope(p).astype(BF16)
    o = A_WIDTH
    k_hi, k_lo = _split_bf16(rope(_dot(hb, w_ref[:, o:o + LANES])))
    ki_ref[:, 0:LANES] = (k_hi + pltpu.roll(k_lo, IDX_DIM, 1)).astype(BF16)
    ki_ref[:, LANES:2 * LANES] = k_hi.astype(BF16)
    o += LANES
    qb_ref[...] = _dot(hb, w_ref[:, o:o + 256])
    o += 256
    fb_ref[...] = _dot(hb, w_ref[:, o:o + 256])
    o += 256
    ib_ref[...] = _dot(hb, w_ref[:, o:o + 512])
    o += 512
    gb_ref[...] = _dot(hb, w_ref[:, o:o + 512])


def _inproj_r(x2, g_row, sc_row, sh_row, w_r, c_tab, s_tab, seq):
    t, d = x2.shape
    tm = 512
    per_b = seq // tm
    cols = w_r.shape[1]
    widths = (A_WIDTH, IDX_K, 256, 256, 512, 512)
    dts = (BF16, BF16, F32, F32, F32, F32)
    return pl.pallas_call(
        _inproj_r_kernel,
        out_shape=tuple(jax.ShapeDtypeStruct((t, w), dt) for w, dt in zip(widths, dts)),
        grid=(t // tm,),
        in_specs=[pl.BlockSpec((tm, d), lambda i: (i, 0)),
                  pl.BlockSpec((1, d), lambda i: (0, 0)),
                  pl.BlockSpec((1, 1, d), lambda i: (i // per_b, 0, 0)),
                  pl.BlockSpec((1, 1, d), lambda i: (i // per_b, 0, 0)),
                  pl.BlockSpec((d, cols), lambda i: (0, 0)),
                  pl.BlockSpec((tm, LANES), lambda i: (i, 0)),
                  pl.BlockSpec((tm, LANES), lambda i: (i, 0))],
        out_specs=tuple(pl.BlockSpec((tm, w), lambda i: (i, 0)) for w in widths),
        compiler_params=pltpu.CompilerParams(vmem_limit_bytes=VMEM_LIMIT),
        name="inproj_r",
    )(x2, g_row, sc_row, sh_row, w_r, c_tab, s_tab)


def _dsa_kernel(qi_ref, wt_ref, ki_ref, qa_ref, ka_ref, v_ref, o_ref,
                keys_ref, m_ref, l_ref, acc_ref, s_ref, p_ref, kh_ref, kl_ref):
    j = pl.program_id(1)
    nkb = j + 1
    tq = DSA_TQ
    kb_rows = DSA_KB
    row = lax.broadcasted_iota(I32, (kb_rows, tq), 0)
    col = lax.broadcasted_iota(I32, (kb_rows, tq), 1)
    q_chunk = (j * tq + col) // CHUNK

    def score_keys(kb):
        r0 = pl.multiple_of(kb * kb_rows, kb_rows)
        ki = ki_ref[pl.ds(r0, kb_rows), :]
        sc = jnp.zeros((kb_rows, tq), F32)
        for h in range(IDX_HEADS):
            lg = _dot(ki, qi_ref[h * IDX_K:(h + 1) * IDX_K, :])
            sc = sc + wt_ref[h:h + 1, :] * jnp.maximum(lg, 0.0)
        sc = jnp.where(sc == 0.0, 0.0, sc)
        bits = pltpu.bitcast(sc, I32)
        return jnp.where(bits < 0, bits ^ 0x7FFFFFFF, bits)

    def store_keys(kb, key):
        keys_ref[kb] = key
        kh_ref[kb] = (key >> 16).astype(I16)
        kl_ref[kb] = ((key & 0xFFFF) - HALF_OFFSET).astype(I16)

    def score_block(kb, carry):
        store_keys(kb, score_keys(kb))
        return carry

    lax.fori_loop(0, j, score_block, 0)
    adm = ((j * kb_rows + row) // CHUNK) <= q_chunk
    store_keys(j, jnp.where(adm, score_keys(j), INT_MIN))

    def count(pred_fn):
        def body(kb, acc):
            hit = jnp.where(pred_fn(keys_ref[kb], kb), 1, 0)
            return acc + jnp.sum(hit.reshape(kb_rows // SUBLANES, SUBLANES, tq), axis=0)
        acc = lax.fori_loop(0, nkb, body, jnp.zeros((SUBLANES, tq), I32))
        return jnp.sum(acc, axis=0, keepdims=True)

    def count16(ref, pred_fn):
        rows16 = 2 * SUBLANES

        def body(kb, acc):
            hit = jnp.where(pred_fn(ref[kb]), jnp.int16(1), jnp.int16(0))
            for r in range(kb_rows // rows16):
                acc = acc + hit[r * rows16:(r + 1) * rows16]
            return acc
        acc = lax.fori_loop(0, nkb, body, jnp.zeros((rows16, tq), I16))
        return jnp.sum(acc.astype(I32), axis=0, keepdims=True)

    def search16(ref, need):
        def step(i, u):
            cand = u | lax.shift_left(jnp.int32(1), 15 - i)
            cand16 = (cand - HALF_OFFSET).astype(I16)
            cnt = count16(ref, lambda k: k >= cand16)
            return jnp.where(cnt >= need, cand, u)
        return lax.fori_loop(0, 16, step, jnp.zeros((1, tq), I32))

    u_hi = search16(kh_ref, TOPK_MAX)
    t_hi = (u_hi - HALF_OFFSET).astype(I16)
    above = count16(kh_ref, lambda k: k > t_hi)

    def mask_low(kb, carry):
        kl_ref[kb] = jnp.where(kh_ref[kb] == t_hi, kl_ref[kb], jnp.int16(-HALF_OFFSET))
        return carry

    lax.fori_loop(0, nkb, mask_low, 0)
    u_lo = search16(kl_ref, TOPK_MAX - above)
    thr = (u_hi - HALF_OFFSET) * (2 * HALF_OFFSET) + u_lo
    n_gt = count(lambda k, kb: k > thr)
    n_eq = count(lambda k, kb: k == thr)
    need = TOPK_MAX - n_gt
    live = thr != INT_MIN
    excess = jnp.logical_and(n_gt + n_eq > TOPK_MAX, live)
    any_excess = jnp.max(jnp.where(excess, 1, 0)) > 0

    def idx_search():
        def step(i, p):
            cand = p | lax.shift_left(jnp.int32(1), 11 - i)
            cnt = count(lambda k, kb: jnp.logical_and(k == thr, kb * kb_rows + row < cand))
            return jnp.where(cnt < need, cand, p)
        return lax.fori_loop(0, 12, step, jnp.zeros((1, tq), I32))

    p_full = jnp.full((1, tq), 4095, I32)
    p_idx = lax.cond(any_excess, idx_search, lambda: p_full)
    p_idx = jnp.where(live, p_idx, -1)

    m_ref[...] = jnp.full(m_ref.shape, NEG_BIG, F32)
    l_ref[...] = jnp.zeros(l_ref.shape, F32)
    acc_ref[...] = jnp.zeros(acc_ref.shape, F32)

    def attn_block(kb, carry):
        r0 = pl.multiple_of(kb * kb_rows, kb_rows)
        key = keys_ref[kb]
        sel = jnp.logical_or(key > thr,
                             jnp.logical_and(key == thr, kb * kb_rows + row <= p_idx))
        bias = jnp.where(sel, 0.0, NEG_BIG)
        heads = [slice(h * A_HEAD_DIM, (h + 1) * A_HEAD_DIM) for h in range(A_HEADS)]
        for h, hs in enumerate(heads):
            s_ref[h] = _dot(ka_ref[pl.ds(r0, kb_rows), hs], qa_ref[hs, :])
        alphas = []
        for h, hs in enumerate(heads):
            s = s_ref[h] + bias
            m_old = m_ref[h][0:1, :]
            m_new = jnp.maximum(m_old, jnp.max(s, axis=0, keepdims=True))
            alpha = jnp.exp2(m_old - m_new)
            p = jnp.exp2(s - m_new)
            l_new = alpha * l_ref[h][0:1, :] + jnp.sum(p, axis=0, keepdims=True)
            p_ref[h] = p.astype(BF16)
            m_ref[h] = jnp.broadcast_to(m_new, (SUBLANES, tq))
            l_ref[h] = jnp.broadcast_to(l_new, (SUBLANES, tq))
            alphas.append(alpha)
        for h, hs in enumerate(heads):
            acc_ref[hs, :] = alphas[h] * acc_ref[hs, :] + _dot(v_ref[kb, hs, :], p_ref[h])
        return carry

    lax.fori_loop(0, nkb, attn_block, 0)

    for h in range(A_HEADS):
        hs = slice(h * A_HEAD_DIM, (h + 1) * A_HEAD_DIM)
        acc_ref[hs, :] = acc_ref[hs, :] / l_ref[h][0:1, :]
    o_ref[...] = acc_ref[...].T


def _dsa(qi_t, w_t, ki, qa_t, ka, v_blk, batch, seq):
    t = ka.shape[0]
    nqb = seq // DSA_TQ
    nkb = seq // DSA_KB
    return pl.pallas_call(
        _dsa_kernel,
        out_shape=jax.ShapeDtypeStruct((t, A_WIDTH), F32),
        grid=(batch, nqb),
        in_specs=[pl.BlockSpec((IDX_HEADS * IDX_K, DSA_TQ), lambda b, j: (0, b * nqb + j)),
                  pl.BlockSpec((SUBLANES, DSA_TQ), lambda b, j: (0, b * nqb + j)),
                  pl.BlockSpec((seq, IDX_K), lambda b, j: (b, 0)),
                  pl.BlockSpec((A_WIDTH, DSA_TQ), lambda b, j: (0, b * nqb + j)),
                  pl.BlockSpec((seq, A_WIDTH), lambda b, j: (b, 0)),
                  pl.BlockSpec((nkb, A_WIDTH, DSA_KB), lambda b, j: (b, 0, 0))],
        out_specs=pl.BlockSpec((DSA_TQ, A_WIDTH), lambda b, j: (b * nqb + j, 0)),
        scratch_shapes=[pltpu.VMEM((nkb, DSA_KB, DSA_TQ), I32),
                        pltpu.VMEM((A_HEADS, SUBLANES, DSA_TQ), F32),
                        pltpu.VMEM((A_HEADS, SUBLANES, DSA_TQ), F32),
                        pltpu.VMEM((A_WIDTH, DSA_TQ), F32),
                        pltpu.VMEM((A_HEADS, DSA_KB, DSA_TQ), F32),
                        pltpu.VMEM((A_HEADS, DSA_KB, DSA_TQ), BF16),
                        pltpu.VMEM((nkb, DSA_KB, DSA_TQ), I16),
                        pltpu.VMEM((nkb, DSA_KB, DSA_TQ), I16)],
        compiler_params=pltpu.CompilerParams(vmem_limit_bytes=VMEM_LIMIT),
        name="dsa",
    )(qi_t, w_t, ki, qa_t, ka, v_blk)


def _hgrn_constants():
    c = CHUNK
    t = np.arange(c)[:, None]
    u = np.arange(c)[None, :]
    blocks = [(u <= t), (u > t)]
    masks = []
    for half in HG_LEVELS:
        mid = (t // (2 * half)) * (2 * half) + half - 1
        right = ((t // half) % 2) == 1
        blocks.append(right & (u > mid) & (u <= t))
        blocks.append((~right) & (u > t) & (u <= mid))
        tt, ss = t, u
        masks.append(((tt // (2 * half)) == (ss // (2 * half)))
                     & ((((tt // half) % 2) == 1) & (((ss // half) % 2) == 0)))
    masks.append(t == u)
    cm = np.concatenate(blocks, axis=0).astype(np.float32)
    cm3 = np.concatenate([cm, cm, cm], axis=1)
    masks4 = np.tile(np.stack(masks).astype(np.float32), (1, 1, B_HEADS))
    head_of = lambda n, per: np.arange(n) // per
    bd_k = (head_of(B_HEADS * c, c)[:, None] == head_of(B_HEADS * B_KEY_DIM, B_KEY_DIM)[None, :])
    bd_v = (head_of(B_HEADS * c, c)[:, None] == head_of(B_WIDTH, B_VAL_DIM)[None, :])
    bd_s = (head_of(B_WIDTH, B_VAL_DIM)[:, None] == head_of(B_HEADS * B_KEY_DIM, B_KEY_DIM)[None, :])
    return cm3, masks4, bd_k.astype(np.float32), bd_v.astype(np.float32), bd_s.astype(np.float32)


def _hgrn_kernel(qb_ref, fb_ref, ib_ref, gb_ref, lb_ref, hg_ref, cm_ref, mask_ref, bdk_ref, bdv_ref,
                 bds_ref, eye_ref, o_ref, state_ref, lv_ref):
    @pl.when(pl.program_id(1) == 0)
    def _():
        state_ref[...] = jnp.zeros(state_ref.shape, F32)

    lb = lb_ref[...]
    c = CHUNK
    nlev = len(HG_LEVELS)

    def stack_heads(x, bd_ref):
        return (jnp.concatenate([x] * B_HEADS, axis=0) * bd_ref[...]).astype(BF16)

    def chunk_step(ci, carry):
        r0 = pl.multiple_of(ci * c, c)
        f = lb + (1.0 - lb) * _sigmoid(fb_ref[pl.ds(r0, c), :])
        lf = jnp.log(f)
        kin = 1.0 - f
        hi = lf.astype(BF16)
        r1 = lf - hi.astype(F32)
        mid = r1.astype(BF16)
        lo = (r1 - mid.astype(F32)).astype(BF16)
        e = jnp.exp(_dot(cm_ref[...], jnp.concatenate([hi, mid, lo], axis=0)))
        q = qb_ref[pl.ds(r0, c), :]
        q_in = (q * e[0:c]).astype(BF16)
        k_out = (kin * e[c:2 * c]).astype(BF16)
        e_last = e[c - 1:c, :]
        for i in range(nlev + 1):
            if i < nlev:
                q_i = q * e[(2 + 2 * i) * c:(3 + 2 * i) * c]
                k_i = kin * e[(3 + 2 * i) * c:(4 + 2 * i) * c]
            else:
                q_i, k_i = q, kin
            lv_ref[i] = _dot_nt(q_i.astype(BF16), stack_heads(k_i, bdk_ref))
        attn = mask_ref[0] * lv_ref[0]
        for i in range(1, nlev + 1):
            attn = attn + mask_ref[i] * lv_ref[i]
        v = ib_ref[pl.ds(r0, c), :]
        st = state_ref[...]
        o_all = _dot_nt(q_in, st.astype(BF16)) + _dot(attn.astype(BF16), stack_heads(v, bdv_ref))
        v_t = _dot_nt(eye_ref[...], v.astype(BF16)).astype(BF16)
        state_ref[...] = st * e_last + bds_ref[...] * _dot(v_t, k_out)
        for h in range(B_HEADS):
            vs = slice(h * B_VAL_DIM, (h + 1) * B_VAL_DIM)
            o = o_all[:, vs]
            ms = jnp.mean(o * o, axis=-1, keepdims=True)
            y = o * lax.rsqrt(ms + RMS_EPS) * hg_ref[...]
            g = gb_ref[pl.ds(r0, c), vs]
            o_ref[pl.ds(r0, c), vs] = y * (g * _sigmoid(g))
        return carry

    lax.fori_loop(0, HG_STEP_CHUNKS, chunk_step, 0)


def _hgrn(qb, fb, ib, gb, lb_row, hg_row, batch, seq):
    t = qb.shape[0]
    tm = HG_STEP_CHUNKS * CHUNK
    per_b = seq // tm
    cm3, masks4, bd_k, bd_v, bd_s = _hgrn_constants()
    consts = (jnp.asarray(cm3, BF16), jnp.asarray(masks4, F32), jnp.asarray(bd_k, F32),
              jnp.asarray(bd_v, F32), jnp.asarray(bd_s, F32), jnp.eye(B_WIDTH, dtype=BF16))
    kd = B_HEADS * B_KEY_DIM
    tok_spec = lambda w: pl.BlockSpec((tm, w), lambda b, i: (b * per_b + i, 0))
    full = lambda a: pl.BlockSpec(a.shape, lambda b, i: (0,) * a.ndim)
    return pl.pallas_call(
        _hgrn_kernel,
        out_shape=jax.ShapeDtypeStruct((t, B_WIDTH), F32),
        grid=(batch, per_b),
        in_specs=[tok_spec(kd), tok_spec(kd), tok_spec(B_WIDTH), tok_spec(B_WIDTH),
                  full(lb_row), full(hg_row)] + [full(a) for a in consts],
        out_specs=tok_spec(B_WIDTH),
        scratch_shapes=[pltpu.VMEM((B_WIDTH, kd), F32),
                        pltpu.VMEM((len(HG_LEVELS) + 1, CHUNK, B_HEADS * CHUNK), F32)],
        compiler_params=pltpu.CompilerParams(vmem_limit_bytes=VMEM_LIMIT),
        name="hgrn",
    )(qb, fb, ib, gb, lb_row, hg_row, *consts)


def _outproj_kernel(oa_ref, ob_ref, x_ref, g1_ref, wo_ref, n2_ref, sc_ref, sh_ref,
                    rw_ref, rb_ref, tri_ref,
                    x1_ref, h2_ref, idx_ref, gate_ref, rank_ref, cnt_ref, run_ref):
    @pl.when(pl.program_id(0) == 0)
    def _():
        run_ref[...] = jnp.zeros(run_ref.shape, F32)

    half = A_WIDTH
    mix = _dot(oa_ref[...].astype(BF16), wo_ref[0:half, :]) + _dot(ob_ref[...].astype(BF16), wo_ref[half:, :])
    x1 = x_ref[...] + g1_ref[0] * mix
    x1_ref[...] = x1
    ms = jnp.mean(x1 * x1, axis=-1, keepdims=True)
    h2 = x1 * lax.rsqrt(ms + RMS_EPS) * n2_ref[...]
    h2 = h2 * (1.0 + sc_ref[0]) + sh_ref[0]
    _store_token_tiles(h2_ref, h2)
    h_hi, h_lo = _split_bf16(h2)
    logits = _dot(jnp.concatenate([h_hi, h_hi, h_lo], axis=1).astype(BF16), rw_ref[...]) + rb_ref[...]
    tm = logits.shape[0]
    lane = lax.broadcasted_iota(I32, (tm, LANES), 1)
    work = logits
    vals, idxs = [], []
    for _ in range(TOP_K_EXPERTS):
        m = jnp.max(work, axis=-1, keepdims=True)
        ix = jnp.min(jnp.where(work == m, lane, LANES), axis=-1, keepdims=True)
        vals.append(m)
        idxs.append(ix)
        work = jnp.where(lane == ix, -jnp.inf, work)
    es = [jnp.exp(v - vals[0]) for v in vals]
    tot = es[0] + es[1] + es[2] + es[3]
    onehot = jnp.zeros((tm, LANES), F32)
    idx_out = jnp.zeros((tm, LANES), I32)
    gate_out = jnp.zeros((tm, LANES), F32)
    for k in range(TOP_K_EXPERTS):
        onehot = onehot + jnp.where(lane == idxs[k], 1.0, 0.0)
        idx_out = jnp.where(lane == k, idxs[k], idx_out)
        gate_out = jnp.where(lane == k, es[k] / tot, gate_out)
    idx_ref[...] = idx_out
    gate_ref[...] = gate_out
    before = _dot(tri_ref[...], onehot.astype(BF16)) + run_ref[...]
    rank_out = jnp.zeros((tm, LANES), F32)
    for k in range(TOP_K_EXPERTS):
        rk = jnp.sum(jnp.where(lane == idxs[k], before, 0.0), axis=-1, keepdims=True)
        rank_out = jnp.where(lane == k, rk, rank_out)
    rank_ref[...] = rank_out.astype(I32)
    run = run_ref[...] + jnp.sum(onehot, axis=0, keepdims=True)
    run_ref[...] = run
    cnt_ref[...] = run.astype(I32)


def _outproj(oa, ob, x2, g1, wo, n2_row, sc2, sh2, rw, rb, seq):
    t, d = x2.shape
    tm = 512
    per_b = seq // tm
    tri = jnp.asarray(np.tril(np.ones((tm, tm), np.float32), -1), BF16)
    row_spec = lambda w: pl.BlockSpec((tm, w), lambda i: (i, 0))
    mod_spec = pl.BlockSpec((1, 1, d), lambda i: (i // per_b, 0, 0))
    full = lambda a: pl.BlockSpec(a.shape, lambda i: (0,) * a.ndim)
    return pl.pallas_call(
        _outproj_kernel,
        out_shape=(jax.ShapeDtypeStruct((t, d), F32), jax.ShapeDtypeStruct((t, ROW_TILES, LANES), F32),
                   jax.ShapeDtypeStruct((t, LANES), I32), jax.ShapeDtypeStruct((t, LANES), F32),
                   jax.ShapeDtypeStruct((t, LANES), I32), jax.ShapeDtypeStruct((1, LANES), I32)),
        grid=(t // tm,),
        in_specs=[row_spec(A_WIDTH), row_spec(B_WIDTH), row_spec(d), mod_spec, full(wo),
                  full(n2_row), mod_spec, mod_spec, full(rw), full(rb), full(tri)],
        out_specs=(row_spec(d), pl.BlockSpec((tm, ROW_TILES, LANES), lambda i: (i, 0, 0)),
                   row_spec(LANES), row_spec(LANES), row_spec(LANES),
                   pl.BlockSpec((1, LANES), lambda i: (0, 0))),
        scratch_shapes=[pltpu.VMEM((1, LANES), F32)],
        compiler_params=pltpu.CompilerParams(dimension_semantics=("arbitrary",),
                                             vmem_limit_bytes=VMEM_LIMIT),
        name="outproj",
    )(oa, ob, x2, g1, wo, n2_row, sc2, sh2, rw, rb, tri)


DISPATCH_TOKENS = 256
WAIT_UNROLL = 16


def _drain(make_copy, n):
    def body(g, carry):
        for _ in range(WAIT_UNROLL):
            make_copy().wait()
        return carry

    lax.fori_loop(0, n // WAIT_UNROLL, body, 0)


def _dispatch_kernel(rows_ref, pad_end_ref, h2_ref, out_hbm, zero_ref, sem):
    n = DISPATCH_TOKENS * TOP_K_EXPERTS

    @pl.when(pl.program_id(0) == 0)
    def _():
        zero_ref[...] = jnp.zeros(zero_ref.shape, F32)

        def last_block(e):
            return out_hbm.at[pl.ds(pl.multiple_of(pad_end_ref[e + 1] - EXPERT_BLOCK, EXPERT_BLOCK),
                                    EXPERT_BLOCK)]

        def has_rows(e):
            return pad_end_ref[e + 1] > pad_end_ref[e]

        n_blocks = out_hbm.shape[0] // EXPERT_BLOCK
        min_blocks = n_blocks - N_EXPERTS
        tail = [(b, out_hbm.at[pl.ds(b * EXPERT_BLOCK, EXPERT_BLOCK)]) for b in range(min_blocks, n_blocks)]

        def unused(b):
            return b * EXPERT_BLOCK >= pad_end_ref[N_EXPERTS]

        for e in range(N_EXPERTS):
            @pl.when(has_rows(e))
            def _():
                pltpu.make_async_copy(zero_ref, last_block(e), sem).start()
        for b, dst in tail:
            @pl.when(unused(b))
            def _():
                pltpu.make_async_copy(zero_ref, dst, sem).start()
        for e in range(N_EXPERTS):
            @pl.when(has_rows(e))
            def _():
                pltpu.make_async_copy(zero_ref, last_block(e), sem).wait()
        for b, dst in tail:
            @pl.when(unused(b))
            def _():
                pltpu.make_async_copy(zero_ref, dst, sem).wait()

    def issue(t, carry):
        for k in range(TOP_K_EXPERTS):
            pltpu.make_async_copy(h2_ref.at[t], out_hbm.at[rows_ref[t * TOP_K_EXPERTS + k]],
                                  sem).start(priority=k % 2)
        return carry

    lax.fori_loop(0, DISPATCH_TOKENS, issue, 0)
    _drain(lambda: pltpu.make_async_copy(h2_ref.at[0], out_hbm.at[0], sem), n)


def _dispatch(rows_flat, pad_end0, h2, n_rows):
    t = h2.shape[0]
    n = DISPATCH_TOKENS * TOP_K_EXPERTS
    return pl.pallas_call(
        _dispatch_kernel,
        out_shape=jax.ShapeDtypeStruct((n_rows, ROW_TILES, LANES), F32),
        grid=(t // DISPATCH_TOKENS,),
        in_specs=[pl.BlockSpec((n,), lambda i: (i,), memory_space=pltpu.SMEM),
                  pl.BlockSpec(memory_space=pltpu.SMEM),
                  pl.BlockSpec((DISPATCH_TOKENS, ROW_TILES, LANES), lambda i: (i, 0, 0))],
        out_specs=pl.BlockSpec(memory_space=pl.ANY),
        scratch_shapes=[pltpu.VMEM((EXPERT_BLOCK, ROW_TILES, LANES), F32),
                        pltpu.SemaphoreType.DMA(())],
        compiler_params=pltpu.CompilerParams(dimension_semantics=("arbitrary",)),
        name="dispatch",
    )(rows_flat, pad_end0, h2)


def _experts_kernel(be_ref, nu_ref, x_ref, w1_ref, b1_ref, w2_ref, b2_ref, y_ref, w1b_ref, w2b_ref):
    i = pl.program_id(0)
    prev = be_ref[jnp.maximum(i - 1, 0)]
    changed = jnp.logical_or(i == 0, be_ref[i] != prev)
    used = i < nu_ref[0]

    @pl.when(jnp.logical_and(changed, used))
    def _():
        w1b_ref[...] = w1_ref[0].astype(BF16)
        w2b_ref[...] = w2_ref[0].astype(BF16)

    @pl.when(used)
    def _():
        f = w2b_ref.shape[0]
        hg = _dot(_load_token_tiles(x_ref).astype(BF16), w1b_ref[...]) + b1_ref[0]
        glu = jnp.minimum(hg[:, :f], SWIGLU_LIMIT)
        lin = jnp.clip(hg[:, f:], -SWIGLU_LIMIT, SWIGLU_LIMIT)
        act = glu * _sigmoid(SWIGLU_ALPHA * glu) * (lin + 1.0)
        _store_token_tiles(y_ref, _dot(act.astype(BF16), w2b_ref[...]) + b2_ref[0])

    @pl.when(jnp.logical_not(used))
    def _():
        y_ref[...] = jnp.zeros(y_ref.shape, F32)


def _experts(block_expert, n_used, x_rows, w1, b1, w2, b2):
    n_rows = x_rows.shape[0]
    e, d, f2 = w1.shape
    f = w2.shape[1]
    nb = n_rows // EXPERT_BLOCK
    tok_spec = pl.BlockSpec((EXPERT_BLOCK, ROW_TILES, LANES), lambda i, be, nu: (i, 0, 0))
    return pl.pallas_call(
        _experts_kernel,
        out_shape=jax.ShapeDtypeStruct((n_rows, ROW_TILES, LANES), F32),
        grid_spec=pltpu.PrefetchScalarGridSpec(
            num_scalar_prefetch=2,
            grid=(nb,),
            in_specs=[tok_spec,
                      pl.BlockSpec((1, d, f2), lambda i, be, nu: (be[i], 0, 0)),
                      pl.BlockSpec((1, 1, f2), lambda i, be, nu: (be[i], 0, 0)),
                      pl.BlockSpec((1, f, d), lambda i, be, nu: (be[i], 0, 0)),
                      pl.BlockSpec((1, 1, d), lambda i, be, nu: (be[i], 0, 0))],
            out_specs=tok_spec,
            scratch_shapes=[pltpu.VMEM((d, f2), BF16), pltpu.VMEM((f, d), BF16)]),
        compiler_params=pltpu.CompilerParams(dimension_semantics=("arbitrary",),
                                             vmem_limit_bytes=VMEM_LIMIT),
        name="experts",
    )(block_expert, n_used, x_rows, w1, b1.reshape(e, 1, f2), w2, b2.reshape(e, 1, d))


COMBINE_TOKENS = 256


def _combine_kernel(rows_ref, next_rows_ref, y_hbm, gate_ref, x1_ref, g2_ref, fg_ref, o_ref, buf_ref, sems):
    n = COMBINE_TOKENS * TOP_K_EXPERTS
    i = pl.program_id(0)
    slot = i % 2

    def gather(rows, s):
        def issue(t, carry):
            for k in range(TOP_K_EXPERTS):
                pltpu.make_async_copy(y_hbm.at[rows[t * TOP_K_EXPERTS + k]], buf_ref.at[s, k, t],
                                      sems.at[s]).start(priority=k % 2)
            return carry

        lax.fori_loop(0, COMBINE_TOKENS, issue, 0)

    @pl.when(i == 0)
    def _():
        gather(rows_ref, 0)

    @pl.when(i + 1 < pl.num_programs(0))
    def _():
        gather(next_rows_ref, 1 - slot)

    _drain(lambda: pltpu.make_async_copy(y_hbm.at[0], buf_ref.at[slot, 0, 0], sems.at[slot]), n)

    gates = gate_ref[...]
    moe = gates[:, 0:1] * _load_token_tiles(buf_ref.at[slot, 0])
    for k in range(1, TOP_K_EXPERTS):
        moe = moe + gates[:, k:k + 1] * _load_token_tiles(buf_ref.at[slot, k])
    x2 = x1_ref[...] + g2_ref[0] * moe
    ms = jnp.mean(x2 * x2, axis=-1, keepdims=True)
    o_ref[...] = x2 * lax.rsqrt(ms + RMS_EPS) * fg_ref[...]


def _combine(rows_flat, y_rows, gates, x1, g2, fg_row, seq):
    t, d = x1.shape
    tm = COMBINE_TOKENS
    per_b = seq // tm
    n = tm * TOP_K_EXPERTS
    steps = t // tm
    return pl.pallas_call(
        _combine_kernel,
        out_shape=jax.ShapeDtypeStruct((t, d), F32),
        grid=(steps,),
        in_specs=[pl.BlockSpec((n,), lambda i: (i,), memory_space=pltpu.SMEM),
                  pl.BlockSpec((n,), lambda i: (jnp.minimum(i + 1, steps - 1),), memory_space=pltpu.SMEM),
                  pl.BlockSpec(memory_space=pl.ANY),
                  pl.BlockSpec((tm, LANES), lambda i: (i, 0)),
                  pl.BlockSpec((tm, d), lambda i: (i, 0)),
                  pl.BlockSpec((1, 1, d), lambda i: (i // per_b, 0, 0)),
                  pl.BlockSpec((1, d), lambda i: (0, 0))],
        out_specs=pl.BlockSpec((tm, d), lambda i: (i, 0)),
        scratch_shapes=[pltpu.VMEM((2, TOP_K_EXPERTS, tm, ROW_TILES, LANES), F32),
                        pltpu.SemaphoreType.DMA((2,))],
        compiler_params=pltpu.CompilerParams(dimension_semantics=("arbitrary",),
                                             vmem_limit_bytes=VMEM_LIMIT),
        name="combine",
    )(rows_flat, rows_flat, y_rows, gates, x1, g2, fg_row)


def kernel(x, c, positions, ada_w, ada_b, norm1_g, w_in, hg_norm_g, lb_logits, w_out, norm2_g,
           router_w, router_b, moe_w1, moe_b1, moe_w2, moe_b2, final_g):
    batch, seq, d = x.shape
    t = batch * seq
    layer = 0
    x2 = x.reshape(t, d)

    c_pad = jnp.concatenate([c, jnp.zeros((SUBLANES - batch, d), F32)], axis=0)
    mod = _adaln(c_pad, ada_w[layer], ada_b[layer][None, :])[:batch]
    shift1, scale1, gate1, shift2, scale2, gate2 = jnp.split(mod, 6, axis=-1)
    row3 = lambda m: m[:, None, :]
    col3 = lambda m: m[:, :, None]

    inv_freq = ROPE_THETA ** (-(jnp.arange(0, ROT_DIM, 2, dtype=F32) / ROT_DIM))
    cos_t, sin_t = _trig(positions.reshape(1, t).astype(F32), inv_freq[:, None])
    ones = jnp.ones((t, A_HEAD_DIM - ROT_DIM), F32)
    c64 = jnp.concatenate([cos_t.T, cos_t.T, ones], axis=1)
    s64 = jnp.concatenate([-sin_t.T, sin_t.T, 0.0 * ones], axis=1)
    c_tab = jnp.concatenate([c64, c64], axis=1)
    s_tab = jnp.concatenate([s64, s64], axis=1)

    wl = w_in[layer]
    sp = np.cumsum((A_WIDTH, A_WIDTH, A_WIDTH, IDX_HEADS * IDX_DIM, IDX_DIM, IDX_HEADS,
                    B_HEADS * B_KEY_DIM, B_HEADS * B_KEY_DIM, B_WIDTH))
    w_qa, w_ka, w_va, w_qi, w_ki, w_wi, w_qb, w_fb, w_ib, w_gb = jnp.split(wl, [int(v) for v in sp], axis=1)
    w_t = jnp.concatenate([w_qa, w_va, w_qi, w_wi, jnp.zeros((d, SUBLANES - IDX_HEADS), F32)], axis=1)
    w_t = w_t.T.astype(BF16)
    w_r = jnp.concatenate([w_ka, w_ki, jnp.zeros((d, LANES - IDX_DIM), F32), w_qb, w_fb, w_ib, w_gb],
                          axis=1).astype(BF16)

    g1n = norm1_g[layer]
    qa_t, v_blk, qi_t, wi_t = _inproj_t(x2.T, g1n[:, None], col3(scale1), col3(shift1),
                                        w_t, cos_t, sin_t, seq)
    ka, ki, qb, fb, ib, gb = _inproj_r(x2, g1n[None, :], row3(scale1), row3(shift1),
                                       w_r, c_tab, s_tab, seq)

    out_a = _dsa(qi_t, wi_t, ki, qa_t, ka, v_blk, batch, seq)

    lower = jnp.cumsum(jax.nn.softmax(lb_logits.astype(F32), axis=0), axis=0)[layer]
    out_b = _hgrn(qb, fb, ib, gb, lower[None, :], hg_norm_g[layer][None, :], batch, seq)

    rw = jnp.concatenate([router_w[layer], jnp.zeros((d, LANES - N_EXPERTS), F32)], axis=1)
    rw_hi = rw.astype(BF16)
    rw_lo = (rw - rw_hi.astype(F32)).astype(BF16)
    rw = jnp.concatenate([rw_hi, rw_lo, rw_hi], axis=0)
    rb = jnp.concatenate([router_b[layer], jnp.full((LANES - N_EXPERTS,), NEG_BIG, F32)])[None, :]
    x1, h2, idx, gates, rank, counts = _outproj(
        out_a, out_b, x2, row3(gate1), w_out[layer].astype(BF16), norm2_g[layer][None, :],
        row3(scale2), row3(shift2), rw, rb, seq)

    counts = counts[0, :N_EXPERTS]
    padded = (counts + EXPERT_BLOCK - 1) // EXPERT_BLOCK * EXPERT_BLOCK
    pad_end = jnp.cumsum(padded)
    pad_start = pad_end - padded
    n_assign = t * TOP_K_EXPERTS
    n_blocks = -(-n_assign // EXPERT_BLOCK) + N_EXPERTS
    n_rows = n_blocks * EXPERT_BLOCK
    block_start = jnp.arange(n_blocks, dtype=I32) * EXPERT_BLOCK
    block_expert = jnp.minimum(jnp.sum(pad_end[None, :] <= block_start[:, None], axis=1),
                               N_EXPERTS - 1).astype(I32)
    n_used = (pad_end[-1:] // EXPERT_BLOCK).astype(I32)
    rows_flat = (pad_start[idx[:, :TOP_K_EXPERTS]] + rank[:, :TOP_K_EXPERTS]).astype(I32).reshape(-1)

    pad_end0 = jnp.concatenate([jnp.zeros((1,), I32), pad_end.astype(I32)])
    x_rows = _dispatch(rows_flat, pad_end0, h2, n_rows)
    y_rows = _experts(block_expert, n_used, x_rows, moe_w1[layer], moe_b1[layer],
                      moe_w2[layer], moe_b2[layer])
    out = _combine(rows_flat, y_rows, gates, x1, row3(gate2), final_g[None, :], seq)
    return out.reshape(batch, seq, d)
```

```python
import functools

import numpy as np
import jax
import jax.numpy as jnp
from jax import lax
from jax.experimental import pallas as pl
from jax.experimental.pallas import tpu as pltpu

F32 = jnp.float32
BF16 = jnp.bfloat16
I32 = jnp.int32
I16 = jnp.int16
HALF_OFFSET = 2 ** 15
HIGHEST = lax.Precision.HIGHEST

D_MODEL = 1024
CHUNK = 64
A_HEADS = 8
A_HEAD_DIM = 64
A_WIDTH = A_HEADS * A_HEAD_DIM
IDX_HEADS = 4
IDX_DIM = 64
TOPK_MAX = 256
B_HEADS = 4
B_KEY_DIM = 64
B_VAL_DIM = 128
B_WIDTH = B_HEADS * B_VAL_DIM
ROPE_THETA = 500000.0
ROT_DIM = A_HEAD_DIM // 4
ROT_HALF = ROT_DIM // 2
N_EXPERTS = 32
TOP_K_EXPERTS = 4
SWIGLU_LIMIT = 7.0
SWIGLU_ALPHA = 1.702
EXPERT_BLOCK = 256
RMS_EPS = 1e-6

LANES = 128
SUBLANES = 8
VMEM_LIMIT = 56 * 1024 * 1024

NEG_BIG = -1e30
INT_MIN = -(2 ** 31)
INT_MAX = 2 ** 31 - 1
LOG2E = 1.4426950408889634
IDX_K = 4 * IDX_DIM

DSA_TQ = 256
DSA_KB = 256
HG_STEP_CHUNKS = 8
HG_LEVELS = (32, 16, 8, 4, 2, 1)


def _dot(a, b, precision=None):
    return jnp.dot(a, b, preferred_element_type=F32, precision=precision)


def _dot_nt(a, b, precision=None):
    return lax.dot_general(a, b, (((1,), (1,)), ((), ())),
                           preferred_element_type=F32, precision=precision)


def _sigmoid(x):
    return 1.0 / (1.0 + jnp.exp(-x))


def _split_bf16(x):
    hi = x.astype(BF16).astype(F32)
    lo = (x - hi).astype(BF16).astype(F32)
    return hi, lo


ROW_TILES = D_MODEL // LANES


def _load_token_tiles(ref3):
    return jnp.concatenate([ref3[:, s, :] for s in range(ROW_TILES)], axis=1)


def _store_token_tiles(ref3, val):
    for s in range(ROW_TILES):
        ref3[:, s, :] = val[:, s * LANES:(s + 1) * LANES]


def _adaln_kernel(c_ref, w_ref, b_ref, o_ref):
    c = c_ref[...]
    o_ref[...] = _dot(c * _sigmoid(c), w_ref[...], HIGHEST) + b_ref[...]


def _adaln(c_pad, ada_w, ada_b):
    d = c_pad.shape[1]
    n = ada_w.shape[1]
    bn = 1024
    return pl.pallas_call(
        _adaln_kernel,
        out_shape=jax.ShapeDtypeStruct((c_pad.shape[0], n), F32),
        grid=(n // bn,),
        in_specs=[pl.BlockSpec((c_pad.shape[0], d), lambda j: (0, 0)),
                  pl.BlockSpec((d, bn), lambda j: (0, j)),
                  pl.BlockSpec((1, bn), lambda j: (0, j))],
        out_specs=pl.BlockSpec((c_pad.shape[0], bn), lambda j: (0, j)),
        name="adaln",
    )(c_pad, ada_w, ada_b)


def _trig_kernel(pos_ref, freq_ref, cos_ref, sin_ref):
    ang = pos_ref[...] * freq_ref[...]
    cos_ref[...] = jnp.cos(ang)
    sin_ref[...] = jnp.sin(ang)


def _trig(pos_row, freq_col):
    t = pos_row.shape[1]
    bt = 2048
    return pl.pallas_call(
        _trig_kernel,
        out_shape=(jax.ShapeDtypeStruct((ROT_HALF, t), F32),) * 2,
        grid=(t // bt,),
        in_specs=[pl.BlockSpec((1, bt), lambda i: (0, i)),
                  pl.BlockSpec((ROT_HALF, 1), lambda i: (0, 0))],
        out_specs=(pl.BlockSpec((ROT_HALF, bt), lambda i: (0, i)),) * 2,
        name="trig",
    )(pos_row, freq_col)


def _rope_rows(p, c, s):
    x1 = p[0:ROT_HALF]
    x2 = p[ROT_HALF:ROT_DIM]
    return jnp.concatenate([x1 * c - x2 * s, x2 * c + x1 * s, p[ROT_DIM:]], axis=0)


def _inproj_kernel(x_ref, g_ref, sc_ref, sh_ref, wr_ref, w_ref, cos_ref, sin_ref, cosr_ref, sinr_ref,
                   ec_ref, es_ref,
                   ka_ref, ki_ref, qb_ref, fb_ref, ib_ref, gb_ref, qa_ref, v_ref, qi_ref, wt_ref):
    x = x_ref[...]
    ms = jnp.mean(x * x, axis=-1, keepdims=True)
    h = x * lax.rsqrt(ms + RMS_EPS) * g_ref[...]
    h = h * (1.0 + sc_ref[0]) + sh_ref[0]
    hb = h.astype(BF16)
    _inproj_rows(hb, wr_ref, cosr_ref, sinr_ref, ec_ref, es_ref,
                 ka_ref, ki_ref, qb_ref, fb_ref, ib_ref, gb_ref)
    c = cos_ref[...]
    s = sin_ref[...]
    pq = _dot_nt(w_ref[0:A_WIDTH, :], hb)
    for hh in range(A_HEADS):
        r = _rope_rows(pq[hh * 64:(hh + 1) * 64], c, s) * (A_HEAD_DIM ** -0.5 * LOG2E)
        qa_ref[hh * 64:(hh + 1) * 64, :] = r.astype(BF16)
    pv = _dot_nt(w_ref[A_WIDTH:2 * A_WIDTH, :], hb)
    for cb in range(v_ref.shape[0]):
        v_ref[cb] = pv[:, cb * DSA_KB:(cb + 1) * DSA_KB].astype(BF16)
    pi = _dot_nt(w_ref[2 * A_WIDTH:2 * A_WIDTH + 256, :], hb)
    for hh in range(IDX_HEADS):
        q = _rope_rows(pi[hh * 64:(hh + 1) * 64], c, s) * (IDX_DIM ** -0.5)
        q_hi, q_lo = _split_bf16(q)
        qi_ref[hh * IDX_K:(hh + 1) * IDX_K, :] = jnp.concatenate(
            [q_hi, q_hi, q_lo, jnp.zeros_like(q_hi)], axis=0).astype(BF16)
    pw = _dot_nt(w_ref[2 * A_WIDTH + 256:2 * A_WIDTH + 264, :], hb)
    wt_ref[...] = pw * (IDX_HEADS ** -0.5)


def _rope_lane_tables():
    lane = np.arange(LANES) % A_HEAD_DIM
    i = np.arange(ROT_HALF)[:, None]
    first, second = lane[None, :] == i, lane[None, :] == i + ROT_HALF
    ec = (first | second).astype(np.float32)
    es = second.astype(np.float32) - first.astype(np.float32)
    ones = (lane >= ROT_DIM).astype(np.float32)[None, :]
    return ec, es, ones


def _inproj_rows(hb, w_ref, cosr_ref, sinr_ref, ec_ref, es_ref,
                 ka_ref, ki_ref, qb_ref, fb_ref, ib_ref, gb_ref):
    c = _dot(cosr_ref[...], ec_ref[0:ROT_HALF, :], HIGHEST) + ec_ref[ROT_HALF:ROT_HALF + 1, :]
    s = _dot(sinr_ref[...], es_ref[...], HIGHEST)
    lane = lax.broadcasted_iota(I32, c.shape, 1)
    first = (lane % A_HEAD_DIM) < ROT_HALF

    def rope(p):
        partner = jnp.where(first, pltpu.roll(p, LANES - ROT_HALF, 1), pltpu.roll(p, ROT_HALF, 1))
        return p * c + partner * s

    for j in range(A_WIDTH // LANES):
        p = _dot(hb, w_ref[:, j * LANES:(j + 1) * LANES])
        ka_ref[:, j * LANES:(j + 1) * LANES] = rope(p).astype(BF16)
    o = A_WIDTH
    k_hi, k_lo = _split_bf16(rope(_dot(hb, w_ref[:, o:o + LANES])))
    ki_ref[:, 0:LANES] = (k_hi + pltpu.roll(k_lo, IDX_DIM, 1)).astype(BF16)
    ki_ref[:, LANES:2 * LANES] = k_hi.astype(BF16)
    o += LANES
    qb_ref[...] = _dot(hb, w_ref[:, o:o + 256])
    o += 256
    fb_ref[...] = _dot(hb, w_ref[:, o:o + 256])
    o += 256
    ib_ref[...] = _dot(hb, w_ref[:, o:o + 512])
    o += 512
    gb_ref[...] = _dot(hb, w_ref[:, o:o + 512])


def _inproj(x2, g_row, sc_row, sh_row, w_r, w_t, cos_t, sin_t, seq):
    t, d = x2.shape
    tm = 512
    per_b = seq // tm
    ec, es, ones = _rope_lane_tables()
    ec = jnp.asarray(np.concatenate([ec, ones, np.zeros((SUBLANES - 1, LANES), np.float32)]))
    es = jnp.asarray(es)
    widths = (A_WIDTH, IDX_K, 256, 256, 512, 512)
    dts = (BF16, BF16, F32, F32, F32, F32)
    row_outs = tuple(jax.ShapeDtypeStruct((t, w), dt) for w, dt in zip(widths, dts))
    col_outs = (jax.ShapeDtypeStruct((A_WIDTH, t), BF16),
                jax.ShapeDtypeStruct((t // DSA_KB, A_WIDTH, DSA_KB), BF16),
                jax.ShapeDtypeStruct((IDX_HEADS * IDX_K, t), BF16),
                jax.ShapeDtypeStruct((SUBLANES, t), F32))
    full = lambda a: pl.BlockSpec(a.shape, lambda i: (0,) * a.ndim)
    mod_spec = pl.BlockSpec((1, 1, d), lambda i: (i // per_b, 0, 0))
    return pl.pallas_call(
        _inproj_kernel,
        out_shape=row_outs + col_outs,
        grid=(t // tm,),
        in_specs=[pl.BlockSpec((tm, d), lambda i: (i, 0)), full(g_row), mod_spec, mod_spec,
                  full(w_r), full(w_t),
                  pl.BlockSpec((ROT_HALF, tm), lambda i: (0, i)),
                  pl.BlockSpec((ROT_HALF, tm), lambda i: (0, i)),
                  pl.BlockSpec((tm, ROT_HALF), lambda i: (i, 0)),
                  pl.BlockSpec((tm, ROT_HALF), lambda i: (i, 0)),
                  full(ec), full(es)],
        out_specs=tuple(pl.BlockSpec((tm, w), lambda i: (i, 0)) for w in widths)
        + (pl.BlockSpec((A_WIDTH, tm), lambda i: (0, i)),
           pl.BlockSpec((tm // DSA_KB, A_WIDTH, DSA_KB), lambda i: (i, 0, 0)),
           pl.BlockSpec((IDX_HEADS * IDX_K, tm), lambda i: (0, i)),
           pl.BlockSpec((SUBLANES, tm), lambda i: (0, i))),
        compiler_params=pltpu.CompilerParams(vmem_limit_bytes=VMEM_LIMIT),
        name="inproj",
    )(x2, g_row, sc_row, sh_row, w_r, w_t, cos_t, sin_t, cos_t.T, sin_t.T, ec, es)


def _dsa_kernel(qi_ref, wt_ref, ki_ref, qa_ref, ka_ref, v_ref, o_ref,
                keys_ref, m_ref, l_ref, acc_ref, s_ref, p_ref, kh_ref, kl_ref):
    j = pl.program_id(1)
    nkb = j + 1
    tq = DSA_TQ
    kb_rows = DSA_KB
    row = lax.broadcasted_iota(I32, (kb_rows, tq), 0)
    col = lax.broadcasted_iota(I32, (kb_rows, tq), 1)
    q_chunk = (j * tq + col) // CHUNK

    def score_keys(kb):
        r0 = pl.multiple_of(kb * kb_rows, kb_rows)
        ki = ki_ref[pl.ds(r0, kb_rows), :]
        sc = jnp.zeros((kb_rows, tq), F32)
        for h in range(IDX_HEADS):
            lg = _dot(ki, qi_ref[h * IDX_K:(h + 1) * IDX_K, :])
            sc = sc + wt_ref[h:h + 1, :] * jnp.maximum(lg, 0.0)
        sc = jnp.where(sc == 0.0, 0.0, sc)
        bits = pltpu.bitcast(sc, I32)
        return jnp.where(bits < 0, bits ^ 0x7FFFFFFF, bits)

    def store_keys(kb, key):
        keys_ref[kb] = key
        kh_ref[kb] = (key >> 16).astype(I16)
        kl_ref[kb] = ((key & 0xFFFF) - HALF_OFFSET).astype(I16)

    def score_block(kb, carry):
        store_keys(kb, score_keys(kb))
        return carry

    lax.fori_loop(0, j, score_block, 0)
    adm = ((j * kb_rows + row) // CHUNK) <= q_chunk
    store_keys(j, jnp.where(adm, score_keys(j), INT_MIN))

    def count(pred_fn):
        def body(kb, acc):
            hit = jnp.where(pred_fn(keys_ref[kb], kb), 1, 0)
            return acc + jnp.sum(hit.reshape(kb_rows // SUBLANES, SUBLANES, tq), axis=0)
        acc = lax.fori_loop(0, nkb, body, jnp.zeros((SUBLANES, tq), I32))
        return jnp.sum(acc, axis=0, keepdims=True)

    def count16(ref, pred_fn):
        rows16 = 2 * SUBLANES

        def body(kb, acc):
            hit = jnp.where(pred_fn(ref[kb]), jnp.int16(1), jnp.int16(0))
            for r in range(kb_rows // rows16):
                acc = acc + hit[r * rows16:(r + 1) * rows16]
            return acc
        acc = lax.fori_loop(0, nkb, body, jnp.zeros((rows16, tq), I16))
        return jnp.sum(acc.astype(I32), axis=0, keepdims=True)

    def search16(ref, need):
        def step(i, u):
            cand = u | lax.shift_left(jnp.int32(1), 15 - i)
            cand16 = (cand - HALF_OFFSET).astype(I16)
            cnt = count16(ref, lambda k: k >= cand16)
            return jnp.where(cnt >= need, cand, u)
        return lax.fori_loop(0, 16, step, jnp.zeros((1, tq), I32))

    u_hi = search16(kh_ref, TOPK_MAX)
    t_hi = (u_hi - HALF_OFFSET).astype(I16)
    above = count16(kh_ref, lambda k: k > t_hi)

    def mask_low(kb, carry):
        kl_ref[kb] = jnp.where(kh_ref[kb] == t_hi, kl_ref[kb], jnp.int16(-HALF_OFFSET))
        return carry

    lax.fori_loop(0, nkb, mask_low, 0)
    u_lo = search16(kl_ref, TOPK_MAX - above)
    thr = (u_hi - HALF_OFFSET) * (2 * HALF_OFFSET) + u_lo
    n_gt = count(lambda k, kb: k > thr)
    n_eq = count(lambda k, kb: k == thr)
    need = TOPK_MAX - n_gt
    live = thr != INT_MIN
    excess = jnp.logical_and(n_gt + n_eq > TOPK_MAX, live)
    any_excess = jnp.max(jnp.where(excess, 1, 0)) > 0

    def idx_search():
        def step(i, p):
            cand = p | lax.shift_left(jnp.int32(1), 11 - i)
            cnt = count(lambda k, kb: jnp.logical_and(k == thr, kb * kb_rows + row < cand))
            return jnp.where(cnt < need, cand, p)
        return lax.fori_loop(0, 12, step, jnp.zeros((1, tq), I32))

    p_full = jnp.full((1, tq), 4095, I32)
    p_idx = lax.cond(any_excess, idx_search, lambda: p_full)
    p_idx = jnp.where(live, p_idx, -1)

    m_ref[...] = jnp.full(m_ref.shape, NEG_BIG, F32)
    l_ref[...] = jnp.zeros(l_ref.shape, F32)
    acc_ref[...] = jnp.zeros(acc_ref.shape, F32)

    def attn_block(kb, carry):
        r0 = pl.multiple_of(kb * kb_rows, kb_rows)
        key = keys_ref[kb]
        sel = jnp.logical_or(key > thr,
                             jnp.logical_and(key == thr, kb * kb_rows + row <= p_idx))
        bias = jnp.where(sel, 0.0, NEG_BIG)
        heads = [slice(h * A_HEAD_DIM, (h + 1) * A_HEAD_DIM) for h in range(A_HEADS)]
        for h, hs in enumerate(heads):
            s_ref[h] = _dot(ka_ref[pl.ds(r0, kb_rows), hs], qa_ref[hs, :])
        alphas = []
        for h, hs in enumerate(heads):
            s = s_ref[h] + bias
            m_old = m_ref[h][0:1, :]
            m_new = jnp.maximum(m_old, jnp.max(s, axis=0, keepdims=True))
            alpha = jnp.exp2(m_old - m_new)
            p = jnp.exp2(s - m_new)
            l_new = alpha * l_ref[h][0:1, :] + jnp.sum(p, axis=0, keepdims=True)
            p_ref[h] = p.astype(BF16)
            m_ref[h] = jnp.broadcast_to(m_new, (SUBLANES, tq))
            l_ref[h] = jnp.broadcast_to(l_new, (SUBLANES, tq))
            alphas.append(alpha)
        for h, hs in enumerate(heads):
            acc_ref[hs, :] = alphas[h] * acc_ref[hs, :] + _dot(v_ref[kb, hs, :], p_ref[h])
        return carry

    lax.fori_loop(0, nkb, attn_block, 0)

    for h in range(A_HEADS):
        hs = slice(h * A_HEAD_DIM, (h + 1) * A_HEAD_DIM)
        acc_ref[hs, :] = acc_ref[hs, :] / l_ref[h][0:1, :]
    o_ref[...] = acc_ref[...].T


def _dsa(qi_t, w_t, ki, qa_t, ka, v_blk, batch, seq):
    t = ka.shape[0]
    nqb = seq // DSA_TQ
    nkb = seq // DSA_KB
    return pl.pallas_call(
        _dsa_kernel,
        out_shape=jax.ShapeDtypeStruct((t, A_WIDTH), F32),
        grid=(batch, nqb),
        in_specs=[pl.BlockSpec((IDX_HEADS * IDX_K, DSA_TQ), lambda b, j: (0, b * nqb + j)),
                  pl.BlockSpec((SUBLANES, DSA_TQ), lambda b, j: (0, b * nqb + j)),
                  pl.BlockSpec((seq, IDX_K), lambda b, j: (b, 0)),
                  pl.BlockSpec((A_WIDTH, DSA_TQ), lambda b, j: (0, b * nqb + j)),
                  pl.BlockSpec((seq, A_WIDTH), lambda b, j: (b, 0)),
                  pl.BlockSpec((nkb, A_WIDTH, DSA_KB), lambda b, j: (b, 0, 0))],
        out_specs=pl.BlockSpec((DSA_TQ, A_WIDTH), lambda b, j: (b * nqb + j, 0)),
        scratch_shapes=[pltpu.VMEM((nkb, DSA_KB, DSA_TQ), I32),
                        pltpu.VMEM((A_HEADS, SUBLANES, DSA_TQ), F32),
                        pltpu.VMEM((A_HEADS, SUBLANES, DSA_TQ), F32),
                        pltpu.VMEM((A_WIDTH, DSA_TQ), F32),
                        pltpu.VMEM((A_HEADS, DSA_KB, DSA_TQ), F32),
                        pltpu.VMEM((A_HEADS, DSA_KB, DSA_TQ), BF16),
                        pltpu.VMEM((nkb, DSA_KB, DSA_TQ), I16),
                        pltpu.VMEM((nkb, DSA_KB, DSA_TQ), I16)],
        compiler_params=pltpu.CompilerParams(vmem_limit_bytes=VMEM_LIMIT),
        name="dsa",
    )(qi_t, w_t, ki, qa_t, ka, v_blk)


def _hgrn_constants():
    c = CHUNK
    t = np.arange(c)[:, None]
    u = np.arange(c)[None, :]
    blocks = [(u <= t), (u > t)]
    masks = []
    for half in HG_LEVELS:
        mid = (t // (2 * half)) * (2 * half) + half - 1
        right = ((t // half) % 2) == 1
        blocks.append(right & (u > mid) & (u <= t))
        blocks.append((~right) & (u > t) & (u <= mid))
        tt, ss = t, u
        masks.append(((tt // (2 * half)) == (ss // (2 * half)))
                     & ((((tt // half) % 2) == 1) & (((ss // half) % 2) == 0)))
    masks.append(t == u)
    cm = np.concatenate(blocks, axis=0).astype(np.float32)
    cm3 = np.concatenate([cm, cm, cm], axis=1)
    masks4 = np.tile(np.stack(masks).astype(np.float32), (1, 1, B_HEADS))
    head_of = lambda n, per: np.arange(n) // per
    bd_k = (head_of(B_HEADS * c, c)[:, None] == head_of(B_HEADS * B_KEY_DIM, B_KEY_DIM)[None, :])
    bd_v = (head_of(B_HEADS * c, c)[:, None] == head_of(B_WIDTH, B_VAL_DIM)[None, :])
    bd_s = (head_of(B_WIDTH, B_VAL_DIM)[:, None] == head_of(B_HEADS * B_KEY_DIM, B_KEY_DIM)[None, :])
    return cm3, masks4, bd_k.astype(np.float32), bd_v.astype(np.float32), bd_s.astype(np.float32)


def _hgrn_kernel(qb_ref, fb_ref, ib_ref, gb_ref, lb_ref, hg_ref, cm_ref, mask_ref, bdk_ref, bdv_ref,
                 bds_ref, eye_ref, o_ref, state_ref, lv_ref):
    @pl.when(pl.program_id(1) == 0)
    def _():
        state_ref[...] = jnp.zeros(state_ref.shape, F32)

    lb = lb_ref[...]
    c = CHUNK
    nlev = len(HG_LEVELS)

    def stack_heads(x, bd_ref):
        return (jnp.concatenate([x] * B_HEADS, axis=0) * bd_ref[...]).astype(BF16)

    def chunk_step(ci, carry):
        r0 = pl.multiple_of(ci * c, c)
        f = lb + (1.0 - lb) * _sigmoid(fb_ref[pl.ds(r0, c), :])
        lf = jnp.log(f)
        kin = 1.0 - f
        hi = lf.astype(BF16)
        r1 = lf - hi.astype(F32)
        mid = r1.astype(BF16)
        lo = (r1 - mid.astype(F32)).astype(BF16)
        e = jnp.exp(_dot(cm_ref[...], jnp.concatenate([hi, mid, lo], axis=0)))
        q = qb_ref[pl.ds(r0, c), :]
        q_in = (q * e[0:c]).astype(BF16)
        k_out = (kin * e[c:2 * c]).astype(BF16)
        e_last = e[c - 1:c, :]
        for i in range(nlev + 1):
            if i < nlev:
                q_i = q * e[(2 + 2 * i) * c:(3 + 2 * i) * c]
                k_i = kin * e[(3 + 2 * i) * c:(4 + 2 * i) * c]
            else:
                q_i, k_i = q, kin
            lv_ref[i] = _dot_nt(q_i.astype(BF16), stack_heads(k_i, bdk_ref))
        attn = mask_ref[0] * lv_ref[0]
        for i in range(1, nlev + 1):
            attn = attn + mask_ref[i] * lv_ref[i]
        v = ib_ref[pl.ds(r0, c), :]
        st = state_ref[...]
        o_all = _dot_nt(q_in, st.astype(BF16)) + _dot(attn.astype(BF16), stack_heads(v, bdv_ref))
        v_t = _dot_nt(eye_ref[...], v.astype(BF16)).astype(BF16)
        state_ref[...] = st * e_last + bds_ref[...] * _dot(v_t, k_out)
        for h in range(B_HEADS):
            vs = slice(h * B_VAL_DIM, (h + 1) * B_VAL_DIM)
            o = o_all[:, vs]
            ms = jnp.mean(o * o, axis=-1, keepdims=True)
            y = o * lax.rsqrt(ms + RMS_EPS) * hg_ref[...]
            g = gb_ref[pl.ds(r0, c), vs]
            o_ref[pl.ds(r0, c), vs] = y * (g * _sigmoid(g))
        return carry

    lax.fori_loop(0, HG_STEP_CHUNKS, chunk_step, 0)


def _hgrn(qb, fb, ib, gb, lb_row, hg_row, batch, seq):
    t = qb.shape[0]
    tm = HG_STEP_CHUNKS * CHUNK
    per_b = seq // tm
    cm3, masks4, bd_k, bd_v, bd_s = _hgrn_constants()
    consts = (jnp.asarray(cm3, BF16), jnp.asarray(masks4, F32), jnp.asarray(bd_k, F32),
              jnp.asarray(bd_v, F32), jnp.asarray(bd_s, F32), jnp.eye(B_WIDTH, dtype=BF16))
    kd = B_HEADS * B_KEY_DIM
    tok_spec = lambda w: pl.BlockSpec((tm, w), lambda b, i: (b * per_b + i, 0))
    full = lambda a: pl.BlockSpec(a.shape, lambda b, i: (0,) * a.ndim)
    return pl.pallas_call(
        _hgrn_kernel,
        out_shape=jax.ShapeDtypeStruct((t, B_WIDTH), F32),
        grid=(batch, per_b),
        in_specs=[tok_spec(kd), tok_spec(kd), tok_spec(B_WIDTH), tok_spec(B_WIDTH),
                  full(lb_row), full(hg_row)] + [full(a) for a in consts],
        out_specs=tok_spec(B_WIDTH),
        scratch_shapes=[pltpu.VMEM((B_WIDTH, kd), F32),
                        pltpu.VMEM((len(HG_LEVELS) + 1, CHUNK, B_HEADS * CHUNK), F32)],
        compiler_params=pltpu.CompilerParams(vmem_limit_bytes=VMEM_LIMIT),
        name="hgrn",
    )(qb, fb, ib, gb, lb_row, hg_row, *consts)


def _outproj_kernel(oa_ref, ob_ref, x_ref, g1_ref, wo_ref, n2_ref, sc_ref, sh_ref,
                    rw_ref, rb_ref, tri_ref,
                    x1_ref, h2_ref, idx_ref, gate_ref, rank_ref, cnt_ref, run_ref):
    @pl.when(pl.program_id(0) == 0)
    def _():
        run_ref[...] = jnp.zeros(run_ref.shape, F32)

    half = A_WIDTH
    mix = _dot(oa_ref[...].astype(BF16), wo_ref[0:half, :]) + _dot(ob_ref[...].astype(BF16), wo_ref[half:, :])
    x1 = x_ref[...] + g1_ref[0] * mix
    x1_ref[...] = x1
    ms = jnp.mean(x1 * x1, axis=-1, keepdims=True)
    h2 = x1 * lax.rsqrt(ms + RMS_EPS) * n2_ref[...]
    h2 = h2 * (1.0 + sc_ref[0]) + sh_ref[0]
    _store_token_tiles(h2_ref, h2)
    h_hi, h_lo = _split_bf16(h2)
    logits = _dot(jnp.concatenate([h_hi, h_hi, h_lo], axis=1).astype(BF16), rw_ref[...]) + rb_ref[...]
    tm = logits.shape[0]
    lane = lax.broadcasted_iota(I32, (tm, LANES), 1)
    work = logits
    vals, idxs = [], []
    for _ in range(TOP_K_EXPERTS):
        m = jnp.max(work, axis=-1, keepdims=True)
        ix = jnp.min(jnp.where(work == m, lane, LANES), axis=-1, keepdims=True)
        vals.append(m)
        idxs.append(ix)
        work = jnp.where(lane == ix, -jnp.inf, work)
    es = [jnp.exp(v - vals[0]) for v in vals]
    tot = es[0] + es[1] + es[2] + es[3]
    onehot = jnp.zeros((tm, LANES), F32)
    idx_out = jnp.zeros((tm, LANES), I32)
    gate_out = jnp.zeros((tm, LANES), F32)
    for k in range(TOP_K_EXPERTS):
        onehot = onehot + jnp.where(lane == idxs[k], 1.0, 0.0)
        idx_out = jnp.where(lane == k, idxs[k], idx_out)
        gate_out = jnp.where(lane == k, es[k] / tot, gate_out)
    idx_ref[...] = idx_out
    gate_ref[...] = gate_out
    before = _dot(tri_ref[...], onehot.astype(BF16)) + run_ref[...]
    rank_out = jnp.zeros((tm, LANES), F32)
    for k in range(TOP_K_EXPERTS):
        rk = jnp.sum(jnp.where(lane == idxs[k], before, 0.0), axis=-1, keepdims=True)
        rank_out = jnp.where(lane == k, rk, rank_out)
    rank_ref[...] = rank_out.astype(I32)
    run = run_ref[...] + jnp.sum(onehot, axis=0, keepdims=True)
    run_ref[...] = run
    cnt_ref[...] = run.astype(I32)


def _outproj(oa, ob, x2, g1, wo, n2_row, sc2, sh2, rw, rb, seq):
    t, d = x2.shape
    tm = 512
    per_b = seq // tm
    tri = jnp.asarray(np.tril(np.ones((tm, tm), np.float32), -1), BF16)
    row_spec = lambda w: pl.BlockSpec((tm, w), lambda i: (i, 0))
    mod_spec = pl.BlockSpec((1, 1, d), lambda i: (i // per_b, 0, 0))
    full = lambda a: pl.BlockSpec(a.shape, lambda i: (0,) * a.ndim)
    return pl.pallas_call(
        _outproj_kernel,
        out_shape=(jax.ShapeDtypeStruct((t, d), F32), jax.ShapeDtypeStruct((t, ROW_TILES, LANES), F32),
                   jax.ShapeDtypeStruct((t, LANES), I32), jax.ShapeDtypeStruct((t, LANES), F32),
                   jax.ShapeDtypeStruct((t, LANES), I32), jax.ShapeDtypeStruct((1, LANES), I32)),
        grid=(t // tm,),
        in_specs=[row_spec(A_WIDTH), row_spec(B_WIDTH), row_spec(d), mod_spec, full(wo),
                  full(n2_row), mod_spec, mod_spec, full(rw), full(rb), full(tri)],
        out_specs=(row_spec(d), pl.BlockSpec((tm, ROW_TILES, LANES), lambda i: (i, 0, 0)),
                   row_spec(LANES), row_spec(LANES), row_spec(LANES),
                   pl.BlockSpec((1, LANES), lambda i: (0, 0))),
        scratch_shapes=[pltpu.VMEM((1, LANES), F32)],
        compiler_params=pltpu.CompilerParams(dimension_semantics=("arbitrary",),
                                             vmem_limit_bytes=VMEM_LIMIT),
        name="outproj",
    )(oa, ob, x2, g1, wo, n2_row, sc2, sh2, rw, rb, tri)


DISPATCH_TOKENS = 256
WAIT_UNROLL = 16


def _drain(make_copy, n):
    def body(g, carry):
        for _ in range(WAIT_UNROLL):
            make_copy().wait()
        return carry

    lax.fori_loop(0, n // WAIT_UNROLL, body, 0)


def _dispatch_kernel(rows_ref, pad_end_ref, h2_ref, out_hbm, zero_ref, sem):
    n = DISPATCH_TOKENS * TOP_K_EXPERTS

    @pl.when(pl.program_id(0) == 0)
    def _():
        zero_ref[...] = jnp.zeros(zero_ref.shape, F32)

        def last_block(e):
            return out_hbm.at[pl.ds(pl.multiple_of(pad_end_ref[e + 1] - EXPERT_BLOCK, EXPERT_BLOCK),
                                    EXPERT_BLOCK)]

        def has_rows(e):
            return pad_end_ref[e + 1] > pad_end_ref[e]

        n_blocks = out_hbm.shape[0] // EXPERT_BLOCK
        min_blocks = n_blocks - N_EXPERTS
        tail = [(b, out_hbm.at[pl.ds(b * EXPERT_BLOCK, EXPERT_BLOCK)]) for b in range(min_blocks, n_blocks)]

        def unused(b):
            return b * EXPERT_BLOCK >= pad_end_ref[N_EXPERTS]

        for e in range(N_EXPERTS):
            @pl.when(has_rows(e))
            def _():
                pltpu.make_async_copy(zero_ref, last_block(e), sem).start()
        for b, dst in tail:
            @pl.when(unused(b))
            def _():
                pltpu.make_async_copy(zero_ref, dst, sem).start()
        for e in range(N_EXPERTS):
            @pl.when(has_rows(e))
            def _():
                pltpu.make_async_copy(zero_ref, last_block(e), sem).wait()
        for b, dst in tail:
            @pl.when(unused(b))
            def _():
                pltpu.make_async_copy(zero_ref, dst, sem).wait()

    def issue(t, carry):
        for k in range(TOP_K_EXPERTS):
            pltpu.make_async_copy(h2_ref.at[t], out_hbm.at[rows_ref[t * TOP_K_EXPERTS + k]],
                                  sem).start(priority=k % 2)
        return carry

    lax.fori_loop(0, DISPATCH_TOKENS, issue, 0)
    _drain(lambda: pltpu.make_async_copy(h2_ref.at[0], out_hbm.at[0], sem), n)


def _dispatch(rows_flat, pad_end0, h2, n_rows):
    t = h2.shape[0]
    n = DISPATCH_TOKENS * TOP_K_EXPERTS
    return pl.pallas_call(
        _dispatch_kernel,
        out_shape=jax.ShapeDtypeStruct((n_rows, ROW_TILES, LANES), F32),
        grid=(t // DISPATCH_TOKENS,),
        in_specs=[pl.BlockSpec((n,), lambda i: (i,), memory_space=pltpu.SMEM),
                  pl.BlockSpec(memory_space=pltpu.SMEM),
                  pl.BlockSpec((DISPATCH_TOKENS, ROW_TILES, LANES), lambda i: (i, 0, 0))],
        out_specs=pl.BlockSpec(memory_space=pl.ANY),
        scratch_shapes=[pltpu.VMEM((EXPERT_BLOCK, ROW_TILES, LANES), F32),
                        pltpu.SemaphoreType.DMA(())],
        compiler_params=pltpu.CompilerParams(dimension_semantics=("arbitrary",)),
        name="dispatch",
    )(rows_flat, pad_end0, h2)


def _experts_kernel(be_ref, nu_ref, x_ref, w1_ref, b1_ref, w2_ref, b2_ref, y_ref, w1b_ref, w2b_ref):
    i = pl.program_id(0)
    prev = be_ref[jnp.maximum(i - 1, 0)]
    changed = jnp.logical_or(i == 0, be_ref[i] != prev)
    used = i < nu_ref[0]

    @pl.when(jnp.logical_and(changed, used))
    def _():
        w1b_ref[...] = w1_ref[0].astype(BF16)
        w2b_ref[...] = w2_ref[0].astype(BF16)

    @pl.when(used)
    def _():
        f = w2b_ref.shape[0]
        hg = _dot(_load_token_tiles(x_ref).astype(BF16), w1b_ref[...]) + b1_ref[0]
        glu = jnp.minimum(hg[:, :f], SWIGLU_LIMIT)
        lin = jnp.clip(hg[:, f:], -SWIGLU_LIMIT, SWIGLU_LIMIT)
        act = glu * _sigmoid(SWIGLU_ALPHA * glu) * (lin + 1.0)
        _store_token_tiles(y_ref, _dot(act.astype(BF16), w2b_ref[...]) + b2_ref[0])

    @pl.when(jnp.logical_not(used))
    def _():
        y_ref[...] = jnp.zeros(y_ref.shape, F32)


def _experts(block_expert, n_used, x_rows, w1, b1, w2, b2):
    n_rows = x_rows.shape[0]
    e, d, f2 = w1.shape
    f = w2.shape[1]
    nb = n_rows // EXPERT_BLOCK
    tok_spec = pl.BlockSpec((EXPERT_BLOCK, ROW_TILES, LANES), lambda i, be, nu: (i, 0, 0))
    return pl.pallas_call(
        _experts_kernel,
        out_shape=jax.ShapeDtypeStruct((n_rows, ROW_TILES, LANES), F32),
        grid_spec=pltpu.PrefetchScalarGridSpec(
            num_scalar_prefetch=2,
            grid=(nb,),
            in_specs=[tok_spec,
                      pl.BlockSpec((1, d, f2), lambda i, be, nu: (be[i], 0, 0)),
                      pl.BlockSpec((1, 1, f2), lambda i, be, nu: (be[i], 0, 0)),
                      pl.BlockSpec((1, f, d), lambda i, be, nu: (be[i], 0, 0)),
                      pl.BlockSpec((1, 1, d), lambda i, be, nu: (be[i], 0, 0))],
            out_specs=tok_spec,
            scratch_shapes=[pltpu.VMEM((d, f2), BF16), pltpu.VMEM((f, d), BF16)]),
        compiler_params=pltpu.CompilerParams(dimension_semantics=("arbitrary",),
                                             vmem_limit_bytes=VMEM_LIMIT),
        name="experts",
    )(block_expert, n_used, x_rows, w1, b1.reshape(e, 1, f2), w2, b2.reshape(e, 1, d))


COMBINE_TOKENS = 256


def _combine_kernel(rows_ref, next_rows_ref, y_hbm, gate_ref, x1_ref, g2_ref, fg_ref, o_ref, buf_ref, sems):
    n = COMBINE_TOKENS * TOP_K_EXPERTS
    i = pl.program_id(0)
    slot = i % 2

    def gather(rows, s):
        def issue(t, carry):
            for k in range(TOP_K_EXPERTS):
                pltpu.make_async_copy(y_hbm.at[rows[t * TOP_K_EXPERTS + k]], buf_ref.at[s, k, t],
                                      sems.at[s]).start(priority=k % 2)
            return carry

        lax.fori_loop(0, COMBINE_TOKENS, issue, 0)

    @pl.when(i == 0)
    def _():
        gather(rows_ref, 0)

    @pl.when(i + 1 < pl.num_programs(0))
    def _():
        gather(next_rows_ref, 1 - slot)

    _drain(lambda: pltpu.make_async_copy(y_hbm.at[0], buf_ref.at[slot, 0, 0], sems.at[slot]), n)

    gates = gate_ref[...]
    moe = gates[:, 0:1] * _load_token_tiles(buf_ref.at[slot, 0])
    for k in range(1, TOP_K_EXPERTS):
        moe = moe + gates[:, k:k + 1] * _load_token_tiles(buf_ref.at[slot, k])
    x2 = x1_ref[...] + g2_ref[0] * moe
    ms = jnp.mean(x2 * x2, axis=-1, keepdims=True)
    o_ref[...] = x2 * lax.rsqrt(ms + RMS_EPS) * fg_ref[...]


def _combine(rows_flat, y_rows, gates, x1, g2, fg_row, seq):
    t, d = x1.shape
    tm = COMBINE_TOKENS
    per_b = seq // tm
    n = tm * TOP_K_EXPERTS
    steps = t // tm
    return pl.pallas_call(
        _combine_kernel,
        out_shape=jax.ShapeDtypeStruct((t, d), F32),
        grid=(steps,),
        in_specs=[pl.BlockSpec((n,), lambda i: (i,), memory_space=pltpu.SMEM),
                  pl.BlockSpec((n,), lambda i: (jnp.minimum(i + 1, steps - 1),), memory_space=pltpu.SMEM),
                  pl.BlockSpec(memory_space=pl.ANY),
                  pl.BlockSpec((tm, LANES), lambda i: (i, 0)),
                  pl.BlockSpec((tm, d), lambda i: (i, 0)),
                  pl.BlockSpec((1, 1, d), lambda i: (i // per_b, 0, 0)),
                  pl.BlockSpec((1, d), lambda i: (0, 0))],
        out_specs=pl.BlockSpec((tm, d), lambda i: (i, 0)),
        scratch_shapes=[pltpu.VMEM((2, TOP_K_EXPERTS, tm, ROW_TILES, LANES), F32),
                        pltpu.SemaphoreType.DMA((2,))],
        compiler_params=pltpu.CompilerParams(dimension_semantics=("arbitrary",),
                                             vmem_limit_bytes=VMEM_LIMIT),
        name="combine",
    )(rows_flat, rows_flat, y_rows, gates, x1, g2, fg_row)


def kernel(x, c, positions, ada_w, ada_b, norm1_g, w_in, hg_norm_g, lb_logits, w_out, norm2_g,
           router_w, router_b, moe_w1, moe_b1, moe_w2, moe_b2, final_g):
    batch, seq, d = x.shape
    t = batch * seq
    layer = 0
    x2 = x.reshape(t, d)

    c_pad = jnp.concatenate([c, jnp.zeros((SUBLANES - batch, d), F32)], axis=0)
    mod = _adaln(c_pad, ada_w[layer], ada_b[layer][None, :])[:batch]
    shift1, scale1, gate1, shift2, scale2, gate2 = jnp.split(mod, 6, axis=-1)
    row3 = lambda m: m[:, None, :]

    inv_freq = ROPE_THETA ** (-(jnp.arange(0, ROT_DIM, 2, dtype=F32) / ROT_DIM))
    cos_t, sin_t = _trig(positions.reshape(1, t).astype(F32), inv_freq[:, None])

    wl = w_in[layer]
    sp = np.cumsum((A_WIDTH, A_WIDTH, A_WIDTH, IDX_HEADS * IDX_DIM, IDX_DIM, IDX_HEADS,
                    B_HEADS * B_KEY_DIM, B_HEADS * B_KEY_DIM, B_WIDTH))
    w_qa, w_ka, w_va, w_qi, w_ki, w_wi, w_qb, w_fb, w_ib, w_gb = jnp.split(wl, [int(v) for v in sp], axis=1)
    w_t = jnp.concatenate([w_qa, w_va, w_qi, w_wi, jnp.zeros((d, SUBLANES - IDX_HEADS), F32)], axis=1)
    w_t = w_t.astype(BF16).T
    w_r = jnp.concatenate([w_ka, w_ki, jnp.zeros((d, LANES - IDX_DIM), F32), w_qb, w_fb, w_ib, w_gb],
                          axis=1).astype(BF16)

    g1n = norm1_g[layer]
    ka, ki, qb, fb, ib, gb, qa_t, v_blk, qi_t, wi_t = _inproj(
        x2, g1n[None, :], row3(scale1), row3(shift1), w_r, w_t, cos_t, sin_t, seq)

    out_a = _dsa(qi_t, wi_t, ki, qa_t, ka, v_blk, batch, seq)

    lower = jnp.cumsum(jax.nn.softmax(lb_logits.astype(F32), axis=0), axis=0)[layer]
    out_b = _hgrn(qb, fb, ib, gb, lower[None, :], hg_norm_g[layer][None, :], batch, seq)

    rw = jnp.concatenate([router_w[layer], jnp.zeros((d, LANES - N_EXPERTS), F32)], axis=1)
    rw_hi = rw.astype(BF16)
    rw_lo = (rw - rw_hi.astype(F32)).astype(BF16)
    rw = jnp.concatenate([rw_hi, rw_lo, rw_hi], axis=0)
    rb = jnp.concatenate([router_b[layer], jnp.full((LANES - N_EXPERTS,), NEG_BIG, F32)])[None, :]
    x1, h2, idx, gates, rank, counts = _outproj(
        out_a, out_b, x2, row3(gate1), w_out[layer].astype(BF16), norm2_g[layer][None, :],
        row3(scale2), row3(shift2), rw, rb, seq)

    counts = counts[0, :N_EXPERTS]
    padded = (counts + EXPERT_BLOCK - 1) // EXPERT_BLOCK * EXPERT_BLOCK
    pad_end = jnp.cumsum(padded)
    pad_start = pad_end - padded
    n_assign = t * TOP_K_EXPERTS
    n_blocks = -(-n_assign // EXPERT_BLOCK) + N_EXPERTS
    n_rows = n_blocks * EXPERT_BLOCK
    block_start = jnp.arange(n_blocks, dtype=I32) * EXPERT_BLOCK
    block_expert = jnp.minimum(jnp.sum(pad_end[None, :] <= block_start[:, None], axis=1),
                               N_EXPERTS - 1).astype(I32)
    n_used = (pad_end[-1:] // EXPERT_BLOCK).astype(I32)
    idx4 = idx[:, :TOP_K_EXPERTS]
    start_of = jnp.sum(jnp.where(idx4[:, :, None] == jnp.arange(N_EXPERTS, dtype=I32), pad_start.astype(I32), 0),
                       axis=-1)
    rows_flat = (start_of + rank[:, :TOP_K_EXPERTS]).astype(I32).reshape(-1)

    pad_end0 = jnp.concatenate([jnp.zeros((1,), I32), pad_end.astype(I32)])
    x_rows = _dispatch(rows_flat, pad_end0, h2, n_rows)
    y_rows = _experts(block_expert, n_used, x_rows, moe_w1[layer], moe_b1[layer],
                      moe_w2[layer], moe_b2[layer])
    out = _combine(rows_flat, y_rows, gates, x1, row3(gate2), final_g[None, :], seq)
    return out.reshape(batch, seq, d)
```

```python
import functools

import numpy as np
import jax
import jax.numpy as jnp
from jax import lax
from jax.experimental import pallas as pl
from jax.experimental.pallas import tpu as pltpu

F32 = jnp.float32
BF16 = jnp.bfloat16
I32 = jnp.int32
I16 = jnp.int16
HALF_OFFSET = 2 ** 15
HIGHEST = lax.Precision.HIGHEST

D_MODEL = 1024
CHUNK = 64
A_HEADS = 8
A_HEAD_DIM = 64
A_WIDTH = A_HEADS * A_HEAD_DIM
IDX_HEADS = 4
IDX_DIM = 64
TOPK_MAX = 256
B_HEADS = 4
B_KEY_DIM = 64
B_VAL_DIM = 128
B_WIDTH = B_HEADS * B_VAL_DIM
ROPE_THETA = 500000.0
ROT_DIM = A_HEAD_DIM // 4
ROT_HALF = ROT_DIM // 2
N_EXPERTS = 32
TOP_K_EXPERTS = 4
SWIGLU_LIMIT = 7.0
SWIGLU_ALPHA = 1.702
EXPERT_BLOCK = 256
RMS_EPS = 1e-6

LANES = 128
SUBLANES = 8
VMEM_LIMIT = 56 * 1024 * 1024

NEG_BIG = -1e30
INT_MIN = -(2 ** 31)
INT_MAX = 2 ** 31 - 1
LOG2E = 1.4426950408889634
IDX_K = 4 * IDX_DIM

DSA_TQ = 256
DSA_KB = 256
HG_STEP_CHUNKS = 8
HG_GROUP = 4
HG_LEVELS = (32, 16, 8, 4, 2, 1)


def _dot(a, b, precision=None):
    return jnp.dot(a, b, preferred_element_type=F32, precision=precision)


def _dot_nt(a, b, precision=None):
    return lax.dot_general(a, b, (((1,), (1,)), ((), ())),
                           preferred_element_type=F32, precision=precision)


def _sigmoid(x):
    return 1.0 / (1.0 + jnp.exp(-x))


def _split_bf16(x):
    hi = x.astype(BF16).astype(F32)
    lo = (x - hi).astype(BF16).astype(F32)
    return hi, lo


ROW_TILES = D_MODEL // LANES


def _load_token_tiles(ref3):
    return jnp.concatenate([ref3[:, s, :] for s in range(ROW_TILES)], axis=1)


def _store_token_tiles(ref3, val):
    for s in range(ROW_TILES):
        ref3[:, s, :] = val[:, s * LANES:(s + 1) * LANES]


def _adaln_kernel(c_ref, w_ref, b_ref, o_ref):
    c = c_ref[...]
    o_ref[...] = _dot(c * _sigmoid(c), w_ref[...], HIGHEST) + b_ref[...]


def _adaln(c_pad, ada_w, ada_b):
    d = c_pad.shape[1]
    n = ada_w.shape[1]
    bn = 1024
    return pl.pallas_call(
        _adaln_kernel,
        out_shape=jax.ShapeDtypeStruct((c_pad.shape[0], n), F32),
        grid=(n // bn,),
        in_specs=[pl.BlockSpec((c_pad.shape[0], d), lambda j: (0, 0)),
                  pl.BlockSpec((d, bn), lambda j: (0, j)),
                  pl.BlockSpec((1, bn), lambda j: (0, j))],
        out_specs=pl.BlockSpec((c_pad.shape[0], bn), lambda j: (0, j)),
        name="adaln",
    )(c_pad, ada_w, ada_b)


def _trig_kernel(pos_ref, freq_ref, cos_ref, sin_ref):
    ang = pos_ref[...] * freq_ref[...]
    cos_ref[...] = jnp.cos(ang)
    sin_ref[...] = jnp.sin(ang)


def _trig(pos_row, freq_col):
    t = pos_row.shape[1]
    bt = 2048
    return pl.pallas_call(
        _trig_kernel,
        out_shape=(jax.ShapeDtypeStruct((ROT_HALF, t), F32),) * 2,
        grid=(t // bt,),
        in_specs=[pl.BlockSpec((1, bt), lambda i: (0, i)),
                  pl.BlockSpec((ROT_HALF, 1), lambda i: (0, 0))],
        out_specs=(pl.BlockSpec((ROT_HALF, bt), lambda i: (0, i)),) * 2,
        name="trig",
    )(pos_row, freq_col)


def _rope_rows(p, c, s):
    x1 = p[0:ROT_HALF]
    x2 = p[ROT_HALF:ROT_DIM]
    return jnp.concatenate([x1 * c - x2 * s, x2 * c + x1 * s, p[ROT_DIM:]], axis=0)


def _inproj_kernel(x_ref, g_ref, sc_ref, sh_ref, wr_ref, w_ref, cos_ref, sin_ref, cosr_ref, sinr_ref,
                   ec_ref, es_ref,
                   ka_ref, ki_ref, qb_ref, fb_ref, ib_ref, gb_ref, qa_ref, v_ref, qi_ref, wt_ref):
    x = x_ref[...]
    ms = jnp.mean(x * x, axis=-1, keepdims=True)
    h = x * lax.rsqrt(ms + RMS_EPS) * g_ref[...]
    h = h * (1.0 + sc_ref[0]) + sh_ref[0]
    hb = h.astype(BF16)
    _inproj_rows(hb, wr_ref, cosr_ref, sinr_ref, ec_ref, es_ref,
                 ka_ref, ki_ref, qb_ref, fb_ref, ib_ref, gb_ref)
    c = cos_ref[...]
    s = sin_ref[...]
    pq = _dot_nt(w_ref[0:A_WIDTH, :], hb)
    for hh in range(A_HEADS):
        r = _rope_rows(pq[hh * 64:(hh + 1) * 64], c, s) * (A_HEAD_DIM ** -0.5 * LOG2E)
        qa_ref[hh * 64:(hh + 1) * 64, :] = r.astype(BF16)
    pv = _dot_nt(w_ref[A_WIDTH:2 * A_WIDTH, :], hb)
    for cb in range(v_ref.shape[0]):
        v_ref[cb] = pv[:, cb * DSA_KB:(cb + 1) * DSA_KB].astype(BF16)
    pi = _dot_nt(w_ref[2 * A_WIDTH:2 * A_WIDTH + 256, :], hb)
    for hh in range(IDX_HEADS):
        q = _rope_rows(pi[hh * 64:(hh + 1) * 64], c, s) * (IDX_DIM ** -0.5)
        q_hi, q_lo = _split_bf16(q)
        qi_ref[hh * IDX_K:(hh + 1) * IDX_K, :] = jnp.concatenate(
            [q_hi, q_hi, q_lo, jnp.zeros_like(q_hi)], axis=0).astype(BF16)
    pw = _dot_nt(w_ref[2 * A_WIDTH + 256:2 * A_WIDTH + 264, :], hb)
    wt_ref[...] = pw * (IDX_HEADS ** -0.5)


def _rope_lane_tables():
    lane = np.arange(LANES) % A_HEAD_DIM
    i = np.arange(ROT_HALF)[:, None]
    first, second = lane[None, :] == i, lane[None, :] == i + ROT_HALF
    ec = (first | second).astype(np.float32)
    es = second.astype(np.float32) - first.astype(np.float32)
    ones = (lane >= ROT_DIM).astype(np.float32)[None, :]
    return ec, es, ones


def _inproj_rows(hb, w_ref, cosr_ref, sinr_ref, ec_ref, es_ref,
                 ka_ref, ki_ref, qb_ref, fb_ref, ib_ref, gb_ref):
    c = _dot(cosr_ref[...], ec_ref[0:ROT_HALF, :], HIGHEST) + ec_ref[ROT_HALF:ROT_HALF + 1, :]
    s = _dot(sinr_ref[...], es_ref[...], HIGHEST)
    lane = lax.broadcasted_iota(I32, c.shape, 1)
    first = (lane % A_HEAD_DIM) < ROT_HALF

    def rope(p):
        partner = jnp.where(first, pltpu.roll(p, LANES - ROT_HALF, 1), pltpu.roll(p, ROT_HALF, 1))
        return p * c + partner * s

    for j in range(A_WIDTH // LANES):
        p = _dot(hb, w_ref[:, j * LANES:(j + 1) * LANES])
        ka_ref[:, j * LANES:(j + 1) * LANES] = rope(p).astype(BF16)
    o = A_WIDTH
    k_hi, k_lo = _split_bf16(rope(_dot(hb, w_ref[:, o:o + LANES])))
    ki_ref[:, 0:LANES] = (k_hi + pltpu.roll(k_lo, IDX_DIM, 1)).astype(BF16)
    ki_ref[:, LANES:2 * LANES] = k_hi.astype(BF16)
    o += LANES
    qb_ref[...] = _dot(hb, w_ref[:, o:o + 256])
    o += 256
    fb_ref[...] = _dot(hb, w_ref[:, o:o + 256])
    o += 256
    ib_ref[...] = _dot(hb, w_ref[:, o:o + 512])
    o += 512
    gb_ref[...] = _dot(hb, w_ref[:, o:o + 512])


def _inproj(x2, g_row, sc_row, sh_row, w_r, w_t, cos_t, sin_t, seq):
    t, d = x2.shape
    tm = 512
    per_b = seq // tm
    ec, es, ones = _rope_lane_tables()
    ec = jnp.asarray(np.concatenate([ec, ones, np.zeros((SUBLANES - 1, LANES), np.float32)]))
    es = jnp.asarray(es)
    widths = (A_WIDTH, IDX_K, 256, 256, 512, 512)
    dts = (BF16, BF16, F32, F32, F32, F32)
    row_outs = tuple(jax.ShapeDtypeStruct((t, w), dt) for w, dt in zip(widths, dts))
    col_outs = (jax.ShapeDtypeStruct((A_WIDTH, t), BF16),
                jax.ShapeDtypeStruct((t // DSA_KB, A_WIDTH, DSA_KB), BF16),
                jax.ShapeDtypeStruct((IDX_HEADS * IDX_K, t), BF16),
                jax.ShapeDtypeStruct((SUBLANES, t), F32))
    full = lambda a: pl.BlockSpec(a.shape, lambda i: (0,) * a.ndim)
    mod_spec = pl.BlockSpec((1, 1, d), lambda i: (i // per_b, 0, 0))
    return pl.pallas_call(
        _inproj_kernel,
        out_shape=row_outs + col_outs,
        grid=(t // tm,),
        in_specs=[pl.BlockSpec((tm, d), lambda i: (i, 0)), full(g_row), mod_spec, mod_spec,
                  full(w_r), full(w_t),
                  pl.BlockSpec((ROT_HALF, tm), lambda i: (0, i)),
                  pl.BlockSpec((ROT_HALF, tm), lambda i: (0, i)),
                  pl.BlockSpec((tm, ROT_HALF), lambda i: (i, 0)),
                  pl.BlockSpec((tm, ROT_HALF), lambda i: (i, 0)),
                  full(ec), full(es)],
        out_specs=tuple(pl.BlockSpec((tm, w), lambda i: (i, 0)) for w in widths)
        + (pl.BlockSpec((A_WIDTH, tm), lambda i: (0, i)),
           pl.BlockSpec((tm // DSA_KB, A_WIDTH, DSA_KB), lambda i: (i, 0, 0)),
           pl.BlockSpec((IDX_HEADS * IDX_K, tm), lambda i: (0, i)),
           pl.BlockSpec((SUBLANES, tm), lambda i: (0, i))),
        compiler_params=pltpu.CompilerParams(vmem_limit_bytes=VMEM_LIMIT),
        name="inproj",
    )(x2, g_row, sc_row, sh_row, w_r, w_t, cos_t, sin_t, cos_t.T, sin_t.T, ec, es)


def _dsa_kernel(qi_ref, wt_ref, ki_ref, qa_ref, ka_ref, v_ref, o_ref,
                keys_ref, m_ref, l_ref, acc_ref, s_ref, p_ref, kh_ref, kl_ref):
    j = pl.program_id(1)
    nkb = j + 1
    tq = DSA_TQ
    kb_rows = DSA_KB
    row = lax.broadcasted_iota(I32, (kb_rows, tq), 0)
    col = lax.broadcasted_iota(I32, (kb_rows, tq), 1)
    q_chunk = (j * tq + col) // CHUNK

    def score_keys(kb):
        r0 = pl.multiple_of(kb * kb_rows, kb_rows)
        ki = ki_ref[pl.ds(r0, kb_rows), :]
        sc = jnp.zeros((kb_rows, tq), F32)
        for h in range(IDX_HEADS):
            lg = _dot(ki, qi_ref[h * IDX_K:(h + 1) * IDX_K, :])
            sc = sc + wt_ref[h:h + 1, :] * jnp.maximum(lg, 0.0)
        sc = jnp.where(sc == 0.0, 0.0, sc)
        bits = pltpu.bitcast(sc, I32)
        return jnp.where(bits < 0, bits ^ 0x7FFFFFFF, bits)

    def store_keys(kb, key):
        keys_ref[kb] = key
        kh_ref[kb] = (key >> 16).astype(I16)
        kl_ref[kb] = ((key & 0xFFFF) - HALF_OFFSET).astype(I16)

    def score_block(kb, carry):
        store_keys(kb, score_keys(kb))
        return carry

    lax.fori_loop(0, j, score_block, 0)
    adm = ((j * kb_rows + row) // CHUNK) <= q_chunk
    store_keys(j, jnp.where(adm, score_keys(j), INT_MIN))

    def count(pred_fn):
        def body(kb, acc):
            hit = jnp.where(pred_fn(keys_ref[kb], kb), 1, 0)
            return acc + jnp.sum(hit.reshape(kb_rows // SUBLANES, SUBLANES, tq), axis=0)
        acc = lax.fori_loop(0, nkb, body, jnp.zeros((SUBLANES, tq), I32))
        return jnp.sum(acc, axis=0, keepdims=True)

    def count16(ref, pred_fn):
        rows16 = 2 * SUBLANES

        def body(kb, acc):
            hit = jnp.where(pred_fn(ref[kb]), jnp.int16(1), jnp.int16(0))
            for r in range(kb_rows // rows16):
                acc = acc + hit[r * rows16:(r + 1) * rows16]
            return acc
        acc = lax.fori_loop(0, nkb, body, jnp.zeros((rows16, tq), I16))
        return jnp.sum(acc.astype(I32), axis=0, keepdims=True)

    def search16(ref, need):
        def step(i, u):
            cand = u | lax.shift_left(jnp.int32(1), 15 - i)
            cand16 = (cand - HALF_OFFSET).astype(I16)
            cnt = count16(ref, lambda k: k >= cand16)
            return jnp.where(cnt >= need, cand, u)
        return lax.fori_loop(0, 16, step, jnp.zeros((1, tq), I32))

    u_hi = search16(kh_ref, TOPK_MAX)
    t_hi = (u_hi - HALF_OFFSET).astype(I16)
    above = count16(kh_ref, lambda k: k > t_hi)

    def mask_low(kb, carry):
        kl_ref[kb] = jnp.where(kh_ref[kb] == t_hi, kl_ref[kb], jnp.int16(-HALF_OFFSET))
        return carry

    lax.fori_loop(0, nkb, mask_low, 0)
    u_lo = search16(kl_ref, TOPK_MAX - above)
    thr = (u_hi - HALF_OFFSET) * (2 * HALF_OFFSET) + u_lo
    n_gt = count(lambda k, kb: k > thr)
    n_eq = count(lambda k, kb: k == thr)
    need = TOPK_MAX - n_gt
    live = thr != INT_MIN
    excess = jnp.logical_and(n_gt + n_eq > TOPK_MAX, live)
    any_excess = jnp.max(jnp.where(excess, 1, 0)) > 0

    def idx_search():
        def step(i, p):
            cand = p | lax.shift_left(jnp.int32(1), 11 - i)
            cnt = count(lambda k, kb: jnp.logical_and(k == thr, kb * kb_rows + row < cand))
            return jnp.where(cnt < need, cand, p)
        return lax.fori_loop(0, 12, step, jnp.zeros((1, tq), I32))

    p_full = jnp.full((1, tq), 4095, I32)
    p_idx = lax.cond(any_excess, idx_search, lambda: p_full)
    p_idx = jnp.where(live, p_idx, -1)

    m_ref[...] = jnp.full(m_ref.shape, NEG_BIG, F32)
    l_ref[...] = jnp.zeros(l_ref.shape, F32)
    acc_ref[...] = jnp.zeros(acc_ref.shape, F32)

    def attn_block(kb, carry):
        r0 = pl.multiple_of(kb * kb_rows, kb_rows)
        key = keys_ref[kb]
        sel = jnp.logical_or(key > thr,
                             jnp.logical_and(key == thr, kb * kb_rows + row <= p_idx))
        bias = jnp.where(sel, 0.0, NEG_BIG)
        heads = [slice(h * A_HEAD_DIM, (h + 1) * A_HEAD_DIM) for h in range(A_HEADS)]
        for h, hs in enumerate(heads):
            s_ref[h] = _dot(ka_ref[pl.ds(r0, kb_rows), hs], qa_ref[hs, :])
        alphas = []
        for h, hs in enumerate(heads):
            s = s_ref[h] + bias
            m_old = m_ref[h][0:1, :]
            m_new = jnp.maximum(m_old, jnp.max(s, axis=0, keepdims=True))
            alpha = jnp.exp2(m_old - m_new)
            p = jnp.exp2(s - m_new)
            l_new = alpha * l_ref[h][0:1, :] + jnp.sum(p, axis=0, keepdims=True)
            p_ref[h] = p.astype(BF16)
            m_ref[h] = jnp.broadcast_to(m_new, (SUBLANES, tq))
            l_ref[h] = jnp.broadcast_to(l_new, (SUBLANES, tq))
            alphas.append(alpha)
        for h, hs in enumerate(heads):
            acc_ref[hs, :] = alphas[h] * acc_ref[hs, :] + _dot(v_ref[kb, hs, :], p_ref[h])
        return carry

    lax.fori_loop(0, nkb, attn_block, 0)

    for h in range(A_HEADS):
        hs = slice(h * A_HEAD_DIM, (h + 1) * A_HEAD_DIM)
        acc_ref[hs, :] = acc_ref[hs, :] / l_ref[h][0:1, :]
    o_ref[...] = acc_ref[...].T


def _dsa(qi_t, w_t, ki, qa_t, ka, v_blk, batch, seq):
    t = ka.shape[0]
    nqb = seq // DSA_TQ
    nkb = seq // DSA_KB
    return pl.pallas_call(
        _dsa_kernel,
        out_shape=jax.ShapeDtypeStruct((t, A_WIDTH), F32),
        grid=(batch, nqb),
        in_specs=[pl.BlockSpec((IDX_HEADS * IDX_K, DSA_TQ), lambda b, j: (0, b * nqb + j)),
                  pl.BlockSpec((SUBLANES, DSA_TQ), lambda b, j: (0, b * nqb + j)),
                  pl.BlockSpec((seq, IDX_K), lambda b, j: (b, 0)),
                  pl.BlockSpec((A_WIDTH, DSA_TQ), lambda b, j: (0, b * nqb + j)),
                  pl.BlockSpec((seq, A_WIDTH), lambda b, j: (b, 0)),
                  pl.BlockSpec((nkb, A_WIDTH, DSA_KB), lambda b, j: (b, 0, 0))],
        out_specs=pl.BlockSpec((DSA_TQ, A_WIDTH), lambda b, j: (b * nqb + j, 0)),
        scratch_shapes=[pltpu.VMEM((nkb, DSA_KB, DSA_TQ), I32),
                        pltpu.VMEM((A_HEADS, SUBLANES, DSA_TQ), F32),
                        pltpu.VMEM((A_HEADS, SUBLANES, DSA_TQ), F32),
                        pltpu.VMEM((A_WIDTH, DSA_TQ), F32),
                        pltpu.VMEM((A_HEADS, DSA_KB, DSA_TQ), F32),
                        pltpu.VMEM((A_HEADS, DSA_KB, DSA_TQ), BF16),
                        pltpu.VMEM((nkb, DSA_KB, DSA_TQ), I16),
                        pltpu.VMEM((nkb, DSA_KB, DSA_TQ), I16)],
        compiler_params=pltpu.CompilerParams(vmem_limit_bytes=VMEM_LIMIT),
        name="dsa",
    )(qi_t, w_t, ki, qa_t, ka, v_blk)


def _hgrn_constants():
    c = CHUNK
    t = np.arange(c)[:, None]
    u = np.arange(c)[None, :]
    blocks = [(u <= t), (u > t)]
    masks = []
    for half in HG_LEVELS:
        mid = (t // (2 * half)) * (2 * half) + half - 1
        right = ((t // half) % 2) == 1
        blocks.append(right & (u > mid) & (u <= t))
        blocks.append((~right) & (u > t) & (u <= mid))
        tt, ss = t, u
        masks.append(((tt // (2 * half)) == (ss // (2 * half)))
                     & ((((tt // half) % 2) == 1) & (((ss // half) % 2) == 0)))
    masks.append(t == u)
    cm = np.concatenate(blocks, axis=0).astype(np.float32)
    cm3 = np.concatenate([cm, cm, cm], axis=1)
    masks4 = np.tile(np.stack(masks).astype(np.float32), (1, 1, B_HEADS))
    head_of = lambda n, per: np.arange(n) // per
    bd_k = (head_of(B_HEADS * c, c)[:, None] == head_of(B_HEADS * B_KEY_DIM, B_KEY_DIM)[None, :])
    bd_v = (head_of(B_HEADS * c, c)[:, None] == head_of(B_WIDTH, B_VAL_DIM)[None, :])
    bd_s = (head_of(B_WIDTH, B_VAL_DIM)[:, None] == head_of(B_HEADS * B_KEY_DIM, B_KEY_DIM)[None, :])
    return cm3, masks4, bd_k.astype(np.float32), bd_v.astype(np.float32), bd_s.astype(np.float32)


def _hgrn_kernel(qb_ref, fb_ref, ib_ref, gb_ref, lb_ref, hg_ref, cm_ref, mask_ref, bdk_ref, bdv_ref,
                 bds_ref, eye_ref, o_ref, state_ref, lv_ref):
    @pl.when(pl.program_id(1) == 0)
    def _():
        state_ref[...] = jnp.zeros(state_ref.shape, F32)

    lb = lb_ref[...]
    c = CHUNK
    nlev = len(HG_LEVELS)
    seq_refs = (qb_ref, fb_ref, ib_ref, gb_ref, o_ref, state_ref, lv_ref)

    def stack_heads(x, bd_ref):
        return (jnp.concatenate([x] * B_HEADS, axis=0) * bd_ref[...]).astype(BF16)

    def chunk_step(ci, carry):
        for bi in range(qb_ref.shape[0]):
            one_chunk(bi, pl.multiple_of(ci * c, c))
        return carry

    def one_chunk(bi, r0):
        qb_ref, fb_ref, ib_ref, gb_ref, o_ref, state_ref, lv_ref = [r.at[bi] for r in seq_refs]
        f = lb + (1.0 - lb) * _sigmoid(fb_ref[pl.ds(r0, c), :])
        lf = jnp.log(f)
        kin = 1.0 - f
        hi = lf.astype(BF16)
        r1 = lf - hi.astype(F32)
        mid = r1.astype(BF16)
        lo = (r1 - mid.astype(F32)).astype(BF16)
        e = jnp.exp(_dot(cm_ref[...], jnp.concatenate([hi, mid, lo], axis=0)))
        q = qb_ref[pl.ds(r0, c), :]
        q_in = (q * e[0:c]).astype(BF16)
        k_out = (kin * e[c:2 * c]).astype(BF16)
        e_last = e[c - 1:c, :]
        for i in range(nlev + 1):
            if i < nlev:
                q_i = q * e[(2 + 2 * i) * c:(3 + 2 * i) * c]
                k_i = kin * e[(3 + 2 * i) * c:(4 + 2 * i) * c]
            else:
                q_i, k_i = q, kin
            lv_ref[i] = _dot_nt(q_i.astype(BF16), stack_heads(k_i, bdk_ref))
        attn = mask_ref[0] * lv_ref[0]
        for i in range(1, nlev + 1):
            attn = attn + mask_ref[i] * lv_ref[i]
        v = ib_ref[pl.ds(r0, c), :]
        st = state_ref[...]
        o_all = _dot_nt(q_in, st.astype(BF16)) + _dot(attn.astype(BF16), stack_heads(v, bdv_ref))
        v_t = _dot_nt(eye_ref[...], v.astype(BF16)).astype(BF16)
        state_ref[...] = st * e_last + bds_ref[...] * _dot(v_t, k_out)
        for h in range(B_HEADS):
            vs = slice(h * B_VAL_DIM, (h + 1) * B_VAL_DIM)
            o = o_all[:, vs]
            ms = jnp.mean(o * o, axis=-1, keepdims=True)
            y = o * lax.rsqrt(ms + RMS_EPS) * hg_ref[...]
            g = gb_ref[pl.ds(r0, c), vs]
            o_ref[pl.ds(r0, c), vs] = y * (g * _sigmoid(g))

    lax.fori_loop(0, HG_STEP_CHUNKS, chunk_step, 0)


def _hgrn(qb, fb, ib, gb, lb_row, hg_row, batch, seq):
    t = qb.shape[0]
    tm = HG_STEP_CHUNKS * CHUNK
    per_b = seq // tm
    cm3, masks4, bd_k, bd_v, bd_s = _hgrn_constants()
    consts = (jnp.asarray(cm3, BF16), jnp.asarray(masks4, F32), jnp.asarray(bd_k, F32),
              jnp.asarray(bd_v, F32), jnp.asarray(bd_s, F32), jnp.eye(B_WIDTH, dtype=BF16))
    kd = B_HEADS * B_KEY_DIM
    grp = HG_GROUP if batch % HG_GROUP == 0 else 1
    tok_spec = lambda w: pl.BlockSpec((grp, tm, w), lambda b, i: (b, i, 0))
    full = lambda a: pl.BlockSpec(a.shape, lambda b, i: (0,) * a.ndim)
    seqs = [a.reshape(batch, seq, a.shape[1]) for a in (qb, fb, ib, gb)]
    out = pl.pallas_call(
        _hgrn_kernel,
        out_shape=jax.ShapeDtypeStruct((batch, seq, B_WIDTH), F32),
        grid=(batch // grp, per_b),
        in_specs=[tok_spec(kd), tok_spec(kd), tok_spec(B_WIDTH), tok_spec(B_WIDTH),
                  full(lb_row), full(hg_row)] + [full(a) for a in consts],
        out_specs=tok_spec(B_WIDTH),
        scratch_shapes=[pltpu.VMEM((grp, B_WIDTH, kd), F32),
                        pltpu.VMEM((grp, len(HG_LEVELS) + 1, CHUNK, B_HEADS * CHUNK), F32)],
        compiler_params=pltpu.CompilerParams(vmem_limit_bytes=VMEM_LIMIT),
        name="hgrn",
    )(*seqs, lb_row, hg_row, *consts)
    return out.reshape(t, B_WIDTH)


def _outproj_kernel(oa_ref, ob_ref, x_ref, g1_ref, wo_ref, n2_ref, sc_ref, sh_ref,
                    rw_ref, rb_ref, tri_ref,
                    x1_ref, h2_ref, idx_ref, gate_ref, rank_ref, cnt_ref, run_ref):
    @pl.when(pl.program_id(0) == 0)
    def _():
        run_ref[...] = jnp.zeros(run_ref.shape, F32)

    half = A_WIDTH
    mix = _dot(oa_ref[...].astype(BF16), wo_ref[0:half, :]) + _dot(ob_ref[...].astype(BF16), wo_ref[half:, :])
    x1 = x_ref[...] + g1_ref[0] * mix
    x1_ref[...] = x1
    ms = jnp.mean(x1 * x1, axis=-1, keepdims=True)
    h2 = x1 * lax.rsqrt(ms + RMS_EPS) * n2_ref[...]
    h2 = h2 * (1.0 + sc_ref[0]) + sh_ref[0]
    _store_token_tiles(h2_ref, h2)
    h_hi, h_lo = _split_bf16(h2)
    logits = _dot(jnp.concatenate([h_hi, h_hi, h_lo], axis=1).astype(BF16), rw_ref[...]) + rb_ref[...]
    tm = logits.shape[0]
    lane = lax.broadcasted_iota(I32, (tm, LANES), 1)
    work = logits
    vals, idxs = [], []
    for _ in range(TOP_K_EXPERTS):
        m = jnp.max(work, axis=-1, keepdims=True)
        ix = jnp.min(jnp.where(work == m, lane, LANES), axis=-1, keepdims=True)
        vals.append(m)
        idxs.append(ix)
        work = jnp.where(lane == ix, -jnp.inf, work)
    es = [jnp.exp(v - vals[0]) for v in vals]
    tot = es[0] + es[1] + es[2] + es[3]
    onehot = jnp.zeros((tm, LANES), F32)
    idx_out = jnp.zeros((tm, LANES), I32)
    gate_out = jnp.zeros((tm, LANES), F32)
    for k in range(TOP_K_EXPERTS):
        onehot = onehot + jnp.where(lane == idxs[k], 1.0, 0.0)
        idx_out = jnp.where(lane == k, idxs[k], idx_out)
        gate_out = jnp.where(lane == k, es[k] / tot, gate_out)
    idx_ref[...] = idx_out
    gate_ref[...] = gate_out
    before = _dot(tri_ref[...], onehot.astype(BF16)) + run_ref[...]
    rank_out = jnp.zeros((tm, LANES), F32)
    for k in range(TOP_K_EXPERTS):
        rk = jnp.sum(jnp.where(lane == idxs[k], before, 0.0), axis=-1, keepdims=True)
        rank_out = jnp.where(lane == k, rk, rank_out)
    rank_ref[...] = rank_out.astype(I32)
    run = run_ref[...] + jnp.sum(onehot, axis=0, keepdims=True)
    run_ref[...] = run
    cnt_ref[...] = run.astype(I32)


def _outproj(oa, ob, x2, g1, wo, n2_row, sc2, sh2, rw, rb, seq):
    t, d = x2.shape
    tm = 512
    per_b = seq // tm
    tri = jnp.asarray(np.tril(np.ones((tm, tm), np.float32), -1), BF16)
    row_spec = lambda w: pl.BlockSpec((tm, w), lambda i: (i, 0))
    mod_spec = pl.BlockSpec((1, 1, d), lambda i: (i // per_b, 0, 0))
    full = lambda a: pl.BlockSpec(a.shape, lambda i: (0,) * a.ndim)
    return pl.pallas_call(
        _outproj_kernel,
        out_shape=(jax.ShapeDtypeStruct((t, d), F32), jax.ShapeDtypeStruct((t, ROW_TILES, LANES), F32),
                   jax.ShapeDtypeStruct((t, LANES), I32), jax.ShapeDtypeStruct((t, LANES), F32),
                   jax.ShapeDtypeStruct((t, LANES), I32), jax.ShapeDtypeStruct((1, LANES), I32)),
        grid=(t // tm,),
        in_specs=[row_spec(A_WIDTH), row_spec(B_WIDTH), row_spec(d), mod_spec, full(wo),
                  full(n2_row), mod_spec, mod_spec, full(rw), full(rb), full(tri)],
        out_specs=(row_spec(d), pl.BlockSpec((tm, ROW_TILES, LANES), lambda i: (i, 0, 0)),
                   row_spec(LANES), row_spec(LANES), row_spec(LANES),
                   pl.BlockSpec((1, LANES), lambda i: (0, 0))),
        scratch_shapes=[pltpu.VMEM((1, LANES), F32)],
        compiler_params=pltpu.CompilerParams(dimension_semantics=("arbitrary",),
                                             vmem_limit_bytes=VMEM_LIMIT),
        name="outproj",
    )(oa, ob, x2, g1, wo, n2_row, sc2, sh2, rw, rb, tri)


DISPATCH_TOKENS = 256
WAIT_UNROLL = 16


def _drain(make_copy, n):
    def body(g, carry):
        for _ in range(WAIT_UNROLL):
            make_copy().wait()
        return carry

    lax.fori_loop(0, n // WAIT_UNROLL, body, 0)


def _dispatch_kernel(rows_ref, pad_end_ref, h2_ref, out_hbm, zero_ref, sem):
    n = DISPATCH_TOKENS * TOP_K_EXPERTS

    @pl.when(pl.program_id(0) == 0)
    def _():
        zero_ref[...] = jnp.zeros(zero_ref.shape, F32)

        def last_block(e):
            return out_hbm.at[pl.ds(pl.multiple_of(pad_end_ref[e + 1] - EXPERT_BLOCK, EXPERT_BLOCK),
                                    EXPERT_BLOCK)]

        def has_rows(e):
            return pad_end_ref[e + 1] > pad_end_ref[e]

        n_blocks = out_hbm.shape[0] // EXPERT_BLOCK
        min_blocks = n_blocks - N_EXPERTS
        tail = [(b, out_hbm.at[pl.ds(b * EXPERT_BLOCK, EXPERT_BLOCK)]) for b in range(min_blocks, n_blocks)]

        def unused(b):
            return b * EXPERT_BLOCK >= pad_end_ref[N_EXPERTS]

        for e in range(N_EXPERTS):
            @pl.when(has_rows(e))
            def _():
                pltpu.make_async_copy(zero_ref, last_block(e), sem).start()
        for b, dst in tail:
            @pl.when(unused(b))
            def _():
                pltpu.make_async_copy(zero_ref, dst, sem).start()
        for e in range(N_EXPERTS):
            @pl.when(has_rows(e))
            def _():
                pltpu.make_async_copy(zero_ref, last_block(e), sem).wait()
        for b, dst in tail:
            @pl.when(unused(b))
            def _():
                pltpu.make_async_copy(zero_ref, dst, sem).wait()

    def issue(t, carry):
        for k in range(TOP_K_EXPERTS):
            pltpu.make_async_copy(h2_ref.at[t], out_hbm.at[rows_ref[t * TOP_K_EXPERTS + k]],
                                  sem).start(priority=k % 2)
        return carry

    lax.fori_loop(0, DISPATCH_TOKENS, issue, 0)
    _drain(lambda: pltpu.make_async_copy(h2_ref.at[0], out_hbm.at[0], sem), n)


def _dispatch(rows_flat, pad_end0, h2, n_rows):
    t = h2.shape[0]
    n = DISPATCH_TOKENS * TOP_K_EXPERTS
    return pl.pallas_call(
        _dispatch_kernel,
        out_shape=jax.ShapeDtypeStruct((n_rows, ROW_TILES, LANES), F32),
        grid=(t // DISPATCH_TOKENS,),
        in_specs=[pl.BlockSpec((n,), lambda i: (i,), memory_space=pltpu.SMEM),
                  pl.BlockSpec(memory_space=pltpu.SMEM),
                  pl.BlockSpec((DISPATCH_TOKENS, ROW_TILES, LANES), lambda i: (i, 0, 0))],
        out_specs=pl.BlockSpec(memory_space=pl.ANY),
        scratch_shapes=[pltpu.VMEM((EXPERT_BLOCK, ROW_TILES, LANES), F32),
                        pltpu.SemaphoreType.DMA(())],
        compiler_params=pltpu.CompilerParams(dimension_semantics=("arbitrary",)),
        name="dispatch",
    )(rows_flat, pad_end0, h2)


def _experts_kernel(be_ref, nu_ref, run_ref, nxt_ref, x_ref, w1_hbm, b1_ref, w2_hbm, b2_ref, y_ref,
                    w1f_ref, w2f_ref, w1b_ref, w2b_ref, sems):
    i = pl.program_id(0)
    e = be_ref[i]
    prev = be_ref[jnp.maximum(i - 1, 0)]
    changed = jnp.logical_or(i == 0, e != prev)
    used = i < nu_ref[0]
    slot = run_ref[i] % 2

    def weight_copies(expert, s):
        return (pltpu.make_async_copy(w1_hbm.at[expert], w1f_ref.at[s], sems.at[0, s]),
                pltpu.make_async_copy(w2_hbm.at[expert], w2f_ref.at[s], sems.at[1, s]))

    @pl.when(i == 0)
    def _():
        for cp in weight_copies(e, slot):
            cp.start()

    @pl.when(jnp.logical_and(changed, used))
    def _():
        for cp in weight_copies(e, slot):
            cp.wait()

        @pl.when(nxt_ref[i] >= 0)
        def _():
            for cp in weight_copies(nxt_ref[i], 1 - slot):
                cp.start()

        w1b_ref[...] = w1f_ref[slot].astype(BF16)
        w2b_ref[...] = w2f_ref[slot].astype(BF16)

    @pl.when(used)
    def _():
        f = w2b_ref.shape[0]
        hg = _dot(_load_token_tiles(x_ref).astype(BF16), w1b_ref[...]) + b1_ref[0]
        glu = jnp.minimum(hg[:, :f], SWIGLU_LIMIT)
        lin = jnp.clip(hg[:, f:], -SWIGLU_LIMIT, SWIGLU_LIMIT)
        act = glu * _sigmoid(SWIGLU_ALPHA * glu) * (lin + 1.0)
        _store_token_tiles(y_ref, _dot(act.astype(BF16), w2b_ref[...]) + b2_ref[0])

    @pl.when(jnp.logical_not(used))
    def _():
        y_ref[...] = jnp.zeros(y_ref.shape, F32)


def _experts(block_expert, n_used, x_rows, w1, b1, w2, b2):
    n_rows = x_rows.shape[0]
    e, d, f2 = w1.shape
    f = w2.shape[1]
    nb = n_rows // EXPERT_BLOCK
    used = jnp.arange(nb, dtype=I32) < n_used[0]
    starts = jnp.logical_and(used, jnp.concatenate([jnp.ones((1,), bool), block_expert[1:] != block_expert[:-1]]))
    run_id = (jnp.cumsum(starts.astype(I32)) - 1).astype(I32)
    later = jnp.logical_and(block_expert[None, :] > block_expert[:, None], used[None, :])
    nxt = jnp.min(jnp.where(later, block_expert[None, :], N_EXPERTS), axis=1)
    nxt = jnp.where(nxt < N_EXPERTS, nxt, -1).astype(I32)
    tok_spec = pl.BlockSpec((EXPERT_BLOCK, ROW_TILES, LANES), lambda i, *_: (i, 0, 0))
    bias_spec = lambda w: pl.BlockSpec((1, 1, w), lambda i, be, *_: (be[i], 0, 0))
    return pl.pallas_call(
        _experts_kernel,
        out_shape=jax.ShapeDtypeStruct((n_rows, ROW_TILES, LANES), F32),
        grid_spec=pltpu.PrefetchScalarGridSpec(
            num_scalar_prefetch=4,
            grid=(nb,),
            in_specs=[tok_spec, pl.BlockSpec(memory_space=pl.ANY), bias_spec(f2),
                      pl.BlockSpec(memory_space=pl.ANY), bias_spec(d)],
            out_specs=tok_spec,
            scratch_shapes=[pltpu.VMEM((2, d, f2), F32), pltpu.VMEM((2, f, d), F32),
                            pltpu.VMEM((d, f2), BF16), pltpu.VMEM((f, d), BF16),
                            pltpu.SemaphoreType.DMA((2, 2))]),
        compiler_params=pltpu.CompilerParams(dimension_semantics=("arbitrary",),
                                             vmem_limit_bytes=VMEM_LIMIT),
        name="experts",
    )(block_expert, n_used, run_id, nxt, x_rows, w1, b1.reshape(e, 1, f2), w2, b2.reshape(e, 1, d))


COMBINE_TOKENS = 256


def _combine_kernel(rows_ref, next_rows_ref, y_hbm, gate_ref, x1_ref, g2_ref, fg_ref, o_ref, buf_ref, sems):
    n = COMBINE_TOKENS * TOP_K_EXPERTS
    i = pl.program_id(0)
    slot = i % 2

    def gather(rows, s):
        def issue(t, carry):
            for k in range(TOP_K_EXPERTS):
                pltpu.make_async_copy(y_hbm.at[rows[t * TOP_K_EXPERTS + k]], buf_ref.at[s, k, t],
                                      sems.at[s]).start(priority=k % 2)
            return carry

        lax.fori_loop(0, COMBINE_TOKENS, issue, 0)

    @pl.when(i == 0)
    def _():
        gather(rows_ref, 0)

    @pl.when(i + 1 < pl.num_programs(0))
    def _():
        gather(next_rows_ref, 1 - slot)

    _drain(lambda: pltpu.make_async_copy(y_hbm.at[0], buf_ref.at[slot, 0, 0], sems.at[slot]), n)

    gates = gate_ref[...]
    moe = gates[:, 0:1] * _load_token_tiles(buf_ref.at[slot, 0])
    for k in range(1, TOP_K_EXPERTS):
        moe = moe + gates[:, k:k + 1] * _load_token_tiles(buf_ref.at[slot, k])
    x2 = x1_ref[...] + g2_ref[0] * moe
    ms = jnp.mean(x2 * x2, axis=-1, keepdims=True)
    o_ref[...] = x2 * lax.rsqrt(ms + RMS_EPS) * fg_ref[...]


def _combine(rows_flat, y_rows, gates, x1, g2, fg_row, seq):
    t, d = x1.shape
    tm = COMBINE_TOKENS
    per_b = seq // tm
    n = tm * TOP_K_EXPERTS
    steps = t // tm
    return pl.pallas_call(
        _combine_kernel,
        out_shape=jax.ShapeDtypeStruct((t, d), F32),
        grid=(steps,),
        in_specs=[pl.BlockSpec((n,), lambda i: (i,), memory_space=pltpu.SMEM),
                  pl.BlockSpec((n,), lambda i: (jnp.minimum(i + 1, steps - 1),), memory_space=pltpu.SMEM),
                  pl.BlockSpec(memory_space=pl.ANY),
                  pl.BlockSpec((tm, LANES), lambda i: (i, 0)),
                  pl.BlockSpec((tm, d), lambda i: (i, 0)),
                  pl.BlockSpec((1, 1, d), lambda i: (i // per_b, 0, 0)),
                  pl.BlockSpec((1, d), lambda i: (0, 0))],
        out_specs=pl.BlockSpec((tm, d), lambda i: (i, 0)),
        scratch_shapes=[pltpu.VMEM((2, TOP_K_EXPERTS, tm, ROW_TILES, LANES), F32),
                        pltpu.SemaphoreType.DMA((2,))],
        compiler_params=pltpu.CompilerParams(dimension_semantics=("arbitrary",),
                                             vmem_limit_bytes=VMEM_LIMIT),
        name="combine",
    )(rows_flat, rows_flat, y_rows, gates, x1, g2, fg_row)


def kernel(x, c, positions, ada_w, ada_b, norm1_g, w_in, hg_norm_g, lb_logits, w_out, norm2_g,
           router_w, router_b, moe_w1, moe_b1, moe_w2, moe_b2, final_g):
    batch, seq, d = x.shape
    t = batch * seq
    layer = 0
    x2 = x.reshape(t, d)

    c_pad = jnp.concatenate([c, jnp.zeros((SUBLANES - batch, d), F32)], axis=0)
    mod = _adaln(c_pad, ada_w[layer], ada_b[layer][None, :])[:batch]
    shift1, scale1, gate1, shift2, scale2, gate2 = jnp.split(mod, 6, axis=-1)
    row3 = lambda m: m[:, None, :]

    inv_freq = ROPE_THETA ** (-(jnp.arange(0, ROT_DIM, 2, dtype=F32) / ROT_DIM))
    cos_t, sin_t = _trig(positions.reshape(1, t).astype(F32), inv_freq[:, None])

    wl = w_in[layer]
    sp = np.cumsum((A_WIDTH, A_WIDTH, A_WIDTH, IDX_HEADS * IDX_DIM, IDX_DIM, IDX_HEADS,
                    B_HEADS * B_KEY_DIM, B_HEADS * B_KEY_DIM, B_WIDTH))
    w_qa, w_ka, w_va, w_qi, w_ki, w_wi, w_qb, w_fb, w_ib, w_gb = jnp.split(wl, [int(v) for v in sp], axis=1)
    w_t = jnp.concatenate([w_qa, w_va, w_qi, w_wi, jnp.zeros((d, SUBLANES - IDX_HEADS), F32)], axis=1)
    w_t = w_t.astype(BF16).T
    w_r = jnp.concatenate([w_ka, w_ki, jnp.zeros((d, LANES - IDX_DIM), F32), w_qb, w_fb, w_ib, w_gb],
                          axis=1).astype(BF16)

    g1n = norm1_g[layer]
    ka, ki, qb, fb, ib, gb, qa_t, v_blk, qi_t, wi_t = _inproj(
        x2, g1n[None, :], row3(scale1), row3(shift1), w_r, w_t, cos_t, sin_t, seq)

    out_a = _dsa(qi_t, wi_t, ki, qa_t, ka, v_blk, batch, seq)

    lower = jnp.cumsum(jax.nn.softmax(lb_logits.astype(F32), axis=0), axis=0)[layer]
    out_b = _hgrn(qb, fb, ib, gb, lower[None, :], hg_norm_g[layer][None, :], batch, seq)

    rw = jnp.concatenate([router_w[layer], jnp.zeros((d, LANES - N_EXPERTS), F32)], axis=1)
    rw_hi = rw.astype(BF16)
    rw_lo = (rw - rw_hi.astype(F32)).astype(BF16)
    rw = jnp.concatenate([rw_hi, rw_lo, rw_hi], axis=0)
    rb = jnp.concatenate([router_b[layer], jnp.full((LANES - N_EXPERTS,), NEG_BIG, F32)])[None, :]
    x1, h2, idx, gates, rank, counts = _outproj(
        out_a, out_b, x2, row3(gate1), w_out[layer].astype(BF16), norm2_g[layer][None, :],
        row3(scale2), row3(shift2), rw, rb, seq)

    counts = counts[0, :N_EXPERTS]
    padded = (counts + EXPERT_BLOCK - 1) // EXPERT_BLOCK * EXPERT_BLOCK
    pad_end = jnp.cumsum(padded)
    pad_start = pad_end - padded
    n_assign = t * TOP_K_EXPERTS
    n_blocks = -(-n_assign // EXPERT_BLOCK) + N_EXPERTS
    n_rows = n_blocks * EXPERT_BLOCK
    block_start = jnp.arange(n_blocks, dtype=I32) * EXPERT_BLOCK
    block_expert = jnp.minimum(jnp.sum(pad_end[None, :] <= block_start[:, None], axis=1),
                               N_EXPERTS - 1).astype(I32)
    n_used = (pad_end[-1:] // EXPERT_BLOCK).astype(I32)
    idx4 = idx[:, :TOP_K_EXPERTS]
    start_of = jnp.sum(jnp.where(idx4[:, :, None] == jnp.arange(N_EXPERTS, dtype=I32), pad_start.astype(I32), 0),
                       axis=-1)
    rows_flat = (start_of + rank[:, :TOP_K_EXPERTS]).astype(I32).reshape(-1)

    pad_end0 = jnp.concatenate([jnp.zeros((1,), I32), pad_end.astype(I32)])
    x_rows = _dispatch(rows_flat, pad_end0, h2, n_rows)
    y_rows = _experts(block_expert, n_used, x_rows, moe_w1[layer], moe_b1[layer],
                      moe_w2[layer], moe_b2[layer])
    out = _combine(rows_flat, y_rows, gates, x1, row3(gate2), final_g[None, :], seq)
    return out.reshape(batch, seq, d)
```

```python
import functools

import numpy as np
import jax
import jax.numpy as jnp
from jax import lax
from jax.experimental import pallas as pl
from jax.experimental.pallas import tpu as pltpu

F32 = jnp.float32
BF16 = jnp.bfloat16
I32 = jnp.int32
I16 = jnp.int16
HALF_OFFSET = 2 ** 15
HIGHEST = lax.Precision.HIGHEST

D_MODEL = 1024
CHUNK = 64
A_HEADS = 8
A_HEAD_DIM = 64
A_WIDTH = A_HEADS * A_HEAD_DIM
IDX_HEADS = 4
IDX_DIM = 64
TOPK_MAX = 256
B_HEADS = 4
B_KEY_DIM = 64
B_VAL_DIM = 128
B_WIDTH = B_HEADS * B_VAL_DIM
ROPE_THETA = 500000.0
ROT_DIM = A_HEAD_DIM // 4
ROT_HALF = ROT_DIM // 2
N_EXPERTS = 32
TOP_K_EXPERTS = 4
SWIGLU_LIMIT = 7.0
SWIGLU_ALPHA = 1.702
EXPERT_BLOCK = 256
RMS_EPS = 1e-6

LANES = 128
SUBLANES = 8
VMEM_LIMIT = 56 * 1024 * 1024

NEG_BIG = -1e30
INT_MIN = -(2 ** 31)
INT_MAX = 2 ** 31 - 1
LOG2E = 1.4426950408889634
IDX_K = 4 * IDX_DIM

DSA_TQ = 256
DSA_KB = 256
HG_STEP_CHUNKS = 8
HG_GROUP = 4
HG_LEVELS = (32, 16, 8, 4, 2, 1)


def _dot(a, b, precision=None):
    return jnp.dot(a, b, preferred_element_type=F32, precision=precision)


def _dot_nt(a, b, precision=None):
    return lax.dot_general(a, b, (((1,), (1,)), ((), ())),
                           preferred_element_type=F32, precision=precision)


def _sigmoid(x):
    return 1.0 / (1.0 + jnp.exp(-x))


def _split_bf16(x):
    hi = x.astype(BF16).astype(F32)
    lo = (x - hi).astype(BF16).astype(F32)
    return hi, lo


ROW_TILES = D_MODEL // LANES


def _load_token_tiles(ref3):
    return jnp.concatenate([ref3[:, s, :] for s in range(ROW_TILES)], axis=1)


def _store_token_tiles(ref3, val):
    for s in range(ROW_TILES):
        ref3[:, s, :] = val[:, s * LANES:(s + 1) * LANES]


def _adaln_kernel(c_ref, w_ref, b_ref, o_ref):
    c = c_ref[...]
    o_ref[...] = _dot(c * _sigmoid(c), w_ref[...], HIGHEST) + b_ref[...]


def _adaln(c_pad, ada_w, ada_b):
    d = c_pad.shape[1]
    n = ada_w.shape[1]
    bn = 1024
    return pl.pallas_call(
        _adaln_kernel,
        out_shape=jax.ShapeDtypeStruct((c_pad.shape[0], n), F32),
        grid=(n // bn,),
        in_specs=[pl.BlockSpec((c_pad.shape[0], d), lambda j: (0, 0)),
                  pl.BlockSpec((d, bn), lambda j: (0, j)),
                  pl.BlockSpec((1, bn), lambda j: (0, j))],
        out_specs=pl.BlockSpec((c_pad.shape[0], bn), lambda j: (0, j)),
        name="adaln",
    )(c_pad, ada_w, ada_b)


def _trig_kernel(pos_ref, freq_ref, cos_ref, sin_ref):
    ang = pos_ref[...] * freq_ref[...]
    cos_ref[...] = jnp.cos(ang)
    sin_ref[...] = jnp.sin(ang)


def _trig(pos_row, freq_col):
    t = pos_row.shape[1]
    bt = 2048
    return pl.pallas_call(
        _trig_kernel,
        out_shape=(jax.ShapeDtypeStruct((ROT_HALF, t), F32),) * 2,
        grid=(t // bt,),
        in_specs=[pl.BlockSpec((1, bt), lambda i: (0, i)),
                  pl.BlockSpec((ROT_HALF, 1), lambda i: (0, 0))],
        out_specs=(pl.BlockSpec((ROT_HALF, bt), lambda i: (0, i)),) * 2,
        name="trig",
    )(pos_row, freq_col)


def _rope_rows(p, c, s):
    x1 = p[0:ROT_HALF]
    x2 = p[ROT_HALF:ROT_DIM]
    return jnp.concatenate([x1 * c - x2 * s, x2 * c + x1 * s, p[ROT_DIM:]], axis=0)


def _inproj_kernel(x_ref, g_ref, sc_ref, sh_ref, wr_ref, w_ref, cos_ref, sin_ref, cosr_ref, sinr_ref,
                   ec_ref, es_ref,
                   ka_ref, ki_ref, qb_ref, fb_ref, ib_ref, gb_ref, qa_ref, v_ref, qi_ref, wt_ref):
    x = x_ref[...]
    ms = jnp.mean(x * x, axis=-1, keepdims=True)
    h = x * lax.rsqrt(ms + RMS_EPS) * g_ref[...]
    h = h * (1.0 + sc_ref[0]) + sh_ref[0]
    hb = h.astype(BF16)
    _inproj_rows(hb, wr_ref, cosr_ref, sinr_ref, ec_ref, es_ref,
                 ka_ref, ki_ref, qb_ref, fb_ref, ib_ref, gb_ref)
    c = cos_ref[...]
    s = sin_ref[...]
    pq = _dot_nt(w_ref[0:A_WIDTH, :], hb)
    for hh in range(A_HEADS):
        r = _rope_rows(pq[hh * 64:(hh + 1) * 64], c, s) * (A_HEAD_DIM ** -0.5 * LOG2E)
        qa_ref[hh * 64:(hh + 1) * 64, :] = r.astype(BF16)
    pv = _dot_nt(w_ref[A_WIDTH:2 * A_WIDTH, :], hb)
    for cb in range(v_ref.shape[0]):
        v_ref[cb] = pv[:, cb * DSA_KB:(cb + 1) * DSA_KB].astype(BF16)
    pi = _dot_nt(w_ref[2 * A_WIDTH:2 * A_WIDTH + 256, :], hb)
    for hh in range(IDX_HEADS):
        q = _rope_rows(pi[hh * 64:(hh + 1) * 64], c, s) * (IDX_DIM ** -0.5)
        q_hi, q_lo = _split_bf16(q)
        qi_ref[hh * IDX_K:(hh + 1) * IDX_K, :] = jnp.concatenate(
            [q_hi, q_hi, q_lo, jnp.zeros_like(q_hi)], axis=0).astype(BF16)
    pw = _dot_nt(w_ref[2 * A_WIDTH + 256:2 * A_WIDTH + 264, :], hb)
    wt_ref[...] = pw * (IDX_HEADS ** -0.5)


def _rope_lane_tables():
    lane = np.arange(LANES) % A_HEAD_DIM
    i = np.arange(ROT_HALF)[:, None]
    first, second = lane[None, :] == i, lane[None, :] == i + ROT_HALF
    ec = (first | second).astype(np.float32)
    es = second.astype(np.float32) - first.astype(np.float32)
    ones = (lane >= ROT_DIM).astype(np.float32)[None, :]
    return ec, es, ones


def _inproj_rows(hb, w_ref, cosr_ref, sinr_ref, ec_ref, es_ref,
                 ka_ref, ki_ref, qb_ref, fb_ref, ib_ref, gb_ref):
    c = _dot(cosr_ref[...], ec_ref[0:ROT_HALF, :], HIGHEST) + ec_ref[ROT_HALF:ROT_HALF + 1, :]
    s = _dot(sinr_ref[...], es_ref[...], HIGHEST)
    lane = lax.broadcasted_iota(I32, c.shape, 1)
    first = (lane % A_HEAD_DIM) < ROT_HALF

    def rope(p):
        partner = jnp.where(first, pltpu.roll(p, LANES - ROT_HALF, 1), pltpu.roll(p, ROT_HALF, 1))
        return p * c + partner * s

    for j in range(A_WIDTH // LANES):
        p = _dot(hb, w_ref[:, j * LANES:(j + 1) * LANES])
        ka_ref[:, j * LANES:(j + 1) * LANES] = rope(p).astype(BF16)
    o = A_WIDTH
    k_hi, k_lo = _split_bf16(rope(_dot(hb, w_ref[:, o:o + LANES])))
    ki_ref[:, 0:LANES] = (k_hi + pltpu.roll(k_lo, IDX_DIM, 1)).astype(BF16)
    ki_ref[:, LANES:2 * LANES] = k_hi.astype(BF16)
    o += LANES
    qb_ref[...] = _dot(hb, w_ref[:, o:o + 256])
    o += 256
    fb_ref[...] = _dot(hb, w_ref[:, o:o + 256])
    o += 256
    ib_ref[...] = _dot(hb, w_ref[:, o:o + 512])
    o += 512
    gb_ref[...] = _dot(hb, w_ref[:, o:o + 512])


def _inproj(x2, g_row, sc_row, sh_row, w_r, w_t, cos_t, sin_t, seq):
    t, d = x2.shape
    tm = 512
    per_b = seq // tm
    ec, es, ones = _rope_lane_tables()
    ec = jnp.asarray(np.concatenate([ec, ones, np.zeros((SUBLANES - 1, LANES), np.float32)]))
    es = jnp.asarray(es)
    widths = (A_WIDTH, IDX_K, 256, 256, 512, 512)
    dts = (BF16, BF16, F32, F32, F32, F32)
    row_outs = tuple(jax.ShapeDtypeStruct((t, w), dt) for w, dt in zip(widths, dts))
    col_outs = (jax.ShapeDtypeStruct((A_WIDTH, t), BF16),
                jax.ShapeDtypeStruct((t // DSA_KB, A_WIDTH, DSA_KB), BF16),
                jax.ShapeDtypeStruct((IDX_HEADS * IDX_K, t), BF16),
                jax.ShapeDtypeStruct((SUBLANES, t), F32))
    full = lambda a: pl.BlockSpec(a.shape, lambda i: (0,) * a.ndim)
    mod_spec = pl.BlockSpec((1, 1, d), lambda i: (i // per_b, 0, 0))
    return pl.pallas_call(
        _inproj_kernel,
        out_shape=row_outs + col_outs,
        grid=(t // tm,),
        in_specs=[pl.BlockSpec((tm, d), lambda i: (i, 0)), full(g_row), mod_spec, mod_spec,
                  full(w_r), full(w_t),
                  pl.BlockSpec((ROT_HALF, tm), lambda i: (0, i)),
                  pl.BlockSpec((ROT_HALF, tm), lambda i: (0, i)),
                  pl.BlockSpec((tm, ROT_HALF), lambda i: (i, 0)),
                  pl.BlockSpec((tm, ROT_HALF), lambda i: (i, 0)),
                  full(ec), full(es)],
        out_specs=tuple(pl.BlockSpec((tm, w), lambda i: (i, 0)) for w in widths)
        + (pl.BlockSpec((A_WIDTH, tm), lambda i: (0, i)),
           pl.BlockSpec((tm // DSA_KB, A_WIDTH, DSA_KB), lambda i: (i, 0, 0)),
           pl.BlockSpec((IDX_HEADS * IDX_K, tm), lambda i: (0, i)),
           pl.BlockSpec((SUBLANES, tm), lambda i: (0, i))),
        compiler_params=pltpu.CompilerParams(vmem_limit_bytes=VMEM_LIMIT),
        name="inproj",
    )(x2, g_row, sc_row, sh_row, w_r, w_t, cos_t, sin_t, cos_t.T, sin_t.T, ec, es)


def _dsa_kernel(qi_ref, wt_ref, ki_ref, qa_ref, ka_ref, v_ref, tri_ref, o_ref,
                keys_ref, m_ref, l_ref, acc_ref, s_ref, p_ref, kh_ref, kl_ref):
    j = pl.program_id(1)
    nkb = j + 1
    tq = DSA_TQ
    kb_rows = DSA_KB
    row = lax.broadcasted_iota(I32, (kb_rows, tq), 0)
    col = lax.broadcasted_iota(I32, (kb_rows, tq), 1)
    q_chunk = (j * tq + col) // CHUNK

    def score_keys(kb):
        r0 = pl.multiple_of(kb * kb_rows, kb_rows)
        ki = ki_ref[pl.ds(r0, kb_rows), :]
        sc = jnp.zeros((kb_rows, tq), F32)
        for h in range(IDX_HEADS):
            lg = _dot(ki, qi_ref[h * IDX_K:(h + 1) * IDX_K, :])
            sc = sc + wt_ref[h:h + 1, :] * jnp.maximum(lg, 0.0)
        sc = jnp.where(sc == 0.0, 0.0, sc)
        bits = pltpu.bitcast(sc, I32)
        return jnp.where(bits < 0, bits ^ 0x7FFFFFFF, bits)

    def store_keys(kb, key):
        keys_ref[kb] = key
        kh_ref[kb] = (key >> 16).astype(I16)
        kl_ref[kb] = ((key & 0xFFFF) - HALF_OFFSET).astype(I16)

    def score_block(kb, carry):
        store_keys(kb, score_keys(kb))
        return carry

    lax.fori_loop(0, j, score_block, 0)
    adm = ((j * kb_rows + row) // CHUNK) <= q_chunk
    store_keys(j, jnp.where(adm, score_keys(j), INT_MIN))

    def count16(ref, pred_fn):
        rows16 = 2 * SUBLANES

        def body(kb, acc):
            hit = jnp.where(pred_fn(ref[kb]), jnp.int16(1), jnp.int16(0))
            for r in range(kb_rows // rows16):
                acc = acc + hit[r * rows16:(r + 1) * rows16]
            return acc
        acc = lax.fori_loop(0, nkb, body, jnp.zeros((rows16, tq), I16))
        return jnp.sum(acc.astype(I32), axis=0, keepdims=True)

    def search16(ref, need):
        def step(i, u):
            cand = u | lax.shift_left(jnp.int32(1), 15 - i)
            cand16 = (cand - HALF_OFFSET).astype(I16)
            cnt = count16(ref, lambda k: k >= cand16)
            return jnp.where(cnt >= need, cand, u)
        return lax.fori_loop(0, 16, step, jnp.zeros((1, tq), I32))

    u_hi = search16(kh_ref, TOPK_MAX)
    t_hi = (u_hi - HALF_OFFSET).astype(I16)
    above = count16(kh_ref, lambda k: k > t_hi)

    def mask_low(kb, carry):
        kl_ref[kb] = jnp.where(kh_ref[kb] == t_hi, kl_ref[kb], jnp.int16(-HALF_OFFSET))
        return carry

    lax.fori_loop(0, nkb, mask_low, 0)
    u_lo = search16(kl_ref, TOPK_MAX - above)
    t_lo = (u_lo - HALF_OFFSET).astype(I16)
    thr = (u_hi - HALF_OFFSET) * (2 * HALF_OFFSET) + u_lo
    n_gt = above + count16(kl_ref, lambda k: k > t_lo)
    need = (TOPK_MAX - n_gt).astype(F32)

    def strike_ties(kb, run):
        key = keys_ref[kb]
        tie = key == thr
        tie_f = jnp.where(tie, 1.0, 0.0)
        rank = _dot(tri_ref[...], tie_f.astype(BF16)) + run
        keys_ref[kb] = jnp.where(jnp.logical_and(tie, rank > need), INT_MIN, key)
        return run + jnp.sum(tie_f, axis=0, keepdims=True)

    lax.fori_loop(0, nkb, strike_ties, jnp.zeros((1, tq), F32))
    thr_sel = jnp.maximum(thr, INT_MIN + 1)

    m_ref[...] = jnp.full(m_ref.shape, NEG_BIG, F32)
    l_ref[...] = jnp.zeros(l_ref.shape, F32)
    acc_ref[...] = jnp.zeros(acc_ref.shape, F32)

    def attn_block(kb, carry):
        r0 = pl.multiple_of(kb * kb_rows, kb_rows)
        bias = jnp.where(keys_ref[kb] >= thr_sel, 0.0, NEG_BIG)
        heads = [slice(h * A_HEAD_DIM, (h + 1) * A_HEAD_DIM) for h in range(A_HEADS)]
        for h, hs in enumerate(heads):
            s_ref[h] = _dot(ka_ref[pl.ds(r0, kb_rows), hs], qa_ref[hs, :])
        alphas = []
        for h, hs in enumerate(heads):
            s = s_ref[h] + bias
            m_old = m_ref[h][0:1, :]
            m_new = jnp.maximum(m_old, jnp.max(s, axis=0, keepdims=True))
            alpha = jnp.exp2(m_old - m_new)
            p = jnp.exp2(s - m_new)
            l_new = alpha * l_ref[h][0:1, :] + jnp.sum(p, axis=0, keepdims=True)
            p_ref[h] = p.astype(BF16)
            m_ref[h] = jnp.broadcast_to(m_new, (SUBLANES, tq))
            l_ref[h] = jnp.broadcast_to(l_new, (SUBLANES, tq))
            alphas.append(alpha)
        for h, hs in enumerate(heads):
            acc_ref[hs, :] = alphas[h] * acc_ref[hs, :] + _dot(v_ref[kb, hs, :], p_ref[h])
        return carry

    lax.fori_loop(0, nkb, attn_block, 0)

    for h in range(A_HEADS):
        hs = slice(h * A_HEAD_DIM, (h + 1) * A_HEAD_DIM)
        acc_ref[hs, :] = acc_ref[hs, :] / l_ref[h][0:1, :]
    o_ref[...] = acc_ref[...].T


def _dsa(qi_t, w_t, ki, qa_t, ka, v_blk, batch, seq):
    t = ka.shape[0]
    nqb = seq // DSA_TQ
    nkb = seq // DSA_KB
    return pl.pallas_call(
        _dsa_kernel,
        out_shape=jax.ShapeDtypeStruct((t, A_WIDTH), F32),
        grid=(batch, nqb),
        in_specs=[pl.BlockSpec((IDX_HEADS * IDX_K, DSA_TQ), lambda b, j: (0, b * nqb + j)),
                  pl.BlockSpec((SUBLANES, DSA_TQ), lambda b, j: (0, b * nqb + j)),
                  pl.BlockSpec((seq, IDX_K), lambda b, j: (b, 0)),
                  pl.BlockSpec((A_WIDTH, DSA_TQ), lambda b, j: (0, b * nqb + j)),
                  pl.BlockSpec((seq, A_WIDTH), lambda b, j: (b, 0)),
                  pl.BlockSpec((nkb, A_WIDTH, DSA_KB), lambda b, j: (b, 0, 0)),
                  pl.BlockSpec((DSA_KB, DSA_KB), lambda b, j: (0, 0))],
        out_specs=pl.BlockSpec((DSA_TQ, A_WIDTH), lambda b, j: (b * nqb + j, 0)),
        scratch_shapes=[pltpu.VMEM((nkb, DSA_KB, DSA_TQ), I32),
                        pltpu.VMEM((A_HEADS, SUBLANES, DSA_TQ), F32),
                        pltpu.VMEM((A_HEADS, SUBLANES, DSA_TQ), F32),
                        pltpu.VMEM((A_WIDTH, DSA_TQ), F32),
                        pltpu.VMEM((A_HEADS, DSA_KB, DSA_TQ), F32),
                        pltpu.VMEM((A_HEADS, DSA_KB, DSA_TQ), BF16),
                        pltpu.VMEM((nkb, DSA_KB, DSA_TQ), I16),
                        pltpu.VMEM((nkb, DSA_KB, DSA_TQ), I16)],
        compiler_params=pltpu.CompilerParams(vmem_limit_bytes=VMEM_LIMIT),
        name="dsa",
    )(qi_t, w_t, ki, qa_t, ka, v_blk, jnp.asarray(np.tril(np.ones((DSA_KB, DSA_KB), np.float32)), BF16))


def _hgrn_constants():
    c = CHUNK
    t = np.arange(c)[:, None]
    u = np.arange(c)[None, :]
    blocks = [(u <= t), (u > t)]
    masks = []
    for half in HG_LEVELS:
        mid = (t // (2 * half)) * (2 * half) + half - 1
        right = ((t // half) % 2) == 1
        blocks.append(right & (u > mid) & (u <= t))
        blocks.append((~right) & (u > t) & (u <= mid))
        tt, ss = t, u
        masks.append(((tt // (2 * half)) == (ss // (2 * half)))
                     & ((((tt // half) % 2) == 1) & (((ss // half) % 2) == 0)))
    masks.append(t == u)
    cm = np.concatenate(blocks, axis=0).astype(np.float32)
    cm3 = np.concatenate([cm, cm, cm], axis=1)
    masks4 = np.tile(np.stack(masks).astype(np.float32), (1, 1, B_HEADS))
    head_of = lambda n, per: np.arange(n) // per
    bd_k = (head_of(B_HEADS * c, c)[:, None] == head_of(B_HEADS * B_KEY_DIM, B_KEY_DIM)[None, :])
    bd_v = (head_of(B_HEADS * c, c)[:, None] == head_of(B_WIDTH, B_VAL_DIM)[None, :])
    bd_s = (head_of(B_WIDTH, B_VAL_DIM)[:, None] == head_of(B_HEADS * B_KEY_DIM, B_KEY_DIM)[None, :])
    return cm3, masks4, bd_k.astype(np.float32), bd_v.astype(np.float32), bd_s.astype(np.float32)


def _hgrn_kernel(qb_ref, fb_ref, ib_ref, gb_ref, lb_ref, hg_ref, cm_ref, mask_ref, bdk_ref, bdv_ref,
                 bds_ref, eye_ref, o_ref, state_ref, lv_ref):
    @pl.when(pl.program_id(1) == 0)
    def _():
        state_ref[...] = jnp.zeros(state_ref.shape, F32)

    lb = lb_ref[...]
    c = CHUNK
    nlev = len(HG_LEVELS)
    seq_refs = (qb_ref, fb_ref, ib_ref, gb_ref, o_ref, state_ref, lv_ref)

    def stack_heads(x, bd_ref):
        return (jnp.concatenate([x] * B_HEADS, axis=0) * bd_ref[...]).astype(BF16)

    def chunk_step(ci, carry):
        for bi in range(qb_ref.shape[0]):
            one_chunk(bi, pl.multiple_of(ci * c, c))
        return carry

    def one_chunk(bi, r0):
        qb_ref, fb_ref, ib_ref, gb_ref, o_ref, state_ref, lv_ref = [r.at[bi] for r in seq_refs]
        f = lb + (1.0 - lb) * _sigmoid(fb_ref[pl.ds(r0, c), :])
        lf = jnp.log(f)
        kin = 1.0 - f
        hi = lf.astype(BF16)
        r1 = lf - hi.astype(F32)
        mid = r1.astype(BF16)
        lo = (r1 - mid.astype(F32)).astype(BF16)
        e = jnp.exp(_dot(cm_ref[...], jnp.concatenate([hi, mid, lo], axis=0)))
        q = qb_ref[pl.ds(r0, c), :]
        q_in = (q * e[0:c]).astype(BF16)
        k_out = (kin * e[c:2 * c]).astype(BF16)
        e_last = e[c - 1:c, :]
        for i in range(nlev + 1):
            if i < nlev:
                q_i = q * e[(2 + 2 * i) * c:(3 + 2 * i) * c]
                k_i = kin * e[(3 + 2 * i) * c:(4 + 2 * i) * c]
            else:
                q_i, k_i = q, kin
            lv_ref[i] = _dot_nt(q_i.astype(BF16), stack_heads(k_i, bdk_ref))
        attn = mask_ref[0] * lv_ref[0]
        for i in range(1, nlev + 1):
            attn = attn + mask_ref[i] * lv_ref[i]
        v = ib_ref[pl.ds(r0, c), :]
        st = state_ref[...]
        o_all = _dot_nt(q_in, st.astype(BF16)) + _dot(attn.astype(BF16), stack_heads(v, bdv_ref))
        v_t = _dot_nt(eye_ref[...], v.astype(BF16)).astype(BF16)
        state_ref[...] = st * e_last + bds_ref[...] * _dot(v_t, k_out)
        for h in range(B_HEADS):
            vs = slice(h * B_VAL_DIM, (h + 1) * B_VAL_DIM)
            o = o_all[:, vs]
            ms = jnp.mean(o * o, axis=-1, keepdims=True)
            y = o * lax.rsqrt(ms + RMS_EPS) * hg_ref[...]
            g = gb_ref[pl.ds(r0, c), vs]
            o_ref[pl.ds(r0, c), vs] = y * (g * _sigmoid(g))

    lax.fori_loop(0, HG_STEP_CHUNKS, chunk_step, 0)


def _hgrn(qb, fb, ib, gb, lb_row, hg_row, batch, seq):
    t = qb.shape[0]
    tm = HG_STEP_CHUNKS * CHUNK
    per_b = seq // tm
    cm3, masks4, bd_k, bd_v, bd_s = _hgrn_constants()
    consts = (jnp.asarray(cm3, BF16), jnp.asarray(masks4, F32), jnp.asarray(bd_k, F32),
              jnp.asarray(bd_v, F32), jnp.asarray(bd_s, F32), jnp.eye(B_WIDTH, dtype=BF16))
    kd = B_HEADS * B_KEY_DIM
    grp = HG_GROUP if batch % HG_GROUP == 0 else 1
    tok_spec = lambda w: pl.BlockSpec((grp, tm, w), lambda b, i: (b, i, 0))
    full = lambda a: pl.BlockSpec(a.shape, lambda b, i: (0,) * a.ndim)
    seqs = [a.reshape(batch, seq, a.shape[1]) for a in (qb, fb, ib, gb)]
    out = pl.pallas_call(
        _hgrn_kernel,
        out_shape=jax.ShapeDtypeStruct((batch, seq, B_WIDTH), F32),
        grid=(batch // grp, per_b),
        in_specs=[tok_spec(kd), tok_spec(kd), tok_spec(B_WIDTH), tok_spec(B_WIDTH),
                  full(lb_row), full(hg_row)] + [full(a) for a in consts],
        out_specs=tok_spec(B_WIDTH),
        scratch_shapes=[pltpu.VMEM((grp, B_WIDTH, kd), F32),
                        pltpu.VMEM((grp, len(HG_LEVELS) + 1, CHUNK, B_HEADS * CHUNK), F32)],
        compiler_params=pltpu.CompilerParams(vmem_limit_bytes=VMEM_LIMIT),
        name="hgrn",
    )(*seqs, lb_row, hg_row, *consts)
    return out.reshape(t, B_WIDTH)


def _outproj_kernel(oa_ref, ob_ref, x_ref, g1_ref, wo_ref, n2_ref, sc_ref, sh_ref,
                    rw_ref, rb_ref, tri_ref,
                    x1_ref, h2_ref, idx_ref, gate_ref, rank_ref, cnt_ref, run_ref):
    @pl.when(pl.program_id(0) == 0)
    def _():
        run_ref[...] = jnp.zeros(run_ref.shape, F32)

    half = A_WIDTH
    mix = _dot(oa_ref[...].astype(BF16), wo_ref[0:half, :]) + _dot(ob_ref[...].astype(BF16), wo_ref[half:, :])
    x1 = x_ref[...] + g1_ref[0] * mix
    x1_ref[...] = x1
    ms = jnp.mean(x1 * x1, axis=-1, keepdims=True)
    h2 = x1 * lax.rsqrt(ms + RMS_EPS) * n2_ref[...]
    h2 = h2 * (1.0 + sc_ref[0]) + sh_ref[0]
    _store_token_tiles(h2_ref, h2)
    h_hi, h_lo = _split_bf16(h2)
    logits = _dot(jnp.concatenate([h_hi, h_hi, h_lo], axis=1).astype(BF16), rw_ref[...]) + rb_ref[...]
    tm = logits.shape[0]
    lane = lax.broadcasted_iota(I32, (tm, LANES), 1)
    work = logits
    vals, idxs = [], []
    for _ in range(TOP_K_EXPERTS):
        m = jnp.max(work, axis=-1, keepdims=True)
        ix = jnp.min(jnp.where(work == m, lane, LANES), axis=-1, keepdims=True)
        vals.append(m)
        idxs.append(ix)
        work = jnp.where(lane == ix, -jnp.inf, work)
    es = [jnp.exp(v - vals[0]) for v in vals]
    tot = es[0] + es[1] + es[2] + es[3]
    onehot = jnp.zeros((tm, LANES), F32)
    idx_out = jnp.zeros((tm, LANES), I32)
    gate_out = jnp.zeros((tm, LANES), F32)
    for k in range(TOP_K_EXPERTS):
        onehot = onehot + jnp.where(lane == idxs[k], 1.0, 0.0)
        idx_out = jnp.where(lane == k, idxs[k], idx_out)
        gate_out = jnp.where(lane == k, es[k] / tot, gate_out)
    idx_ref[...] = idx_out
    gate_ref[...] = gate_out
    before = _dot(tri_ref[...], onehot.astype(BF16)) + run_ref[...]
    rank_out = jnp.zeros((tm, LANES), F32)
    for k in range(TOP_K_EXPERTS):
        rk = jnp.sum(jnp.where(lane == idxs[k], before, 0.0), axis=-1, keepdims=True)
        rank_out = jnp.where(lane == k, rk, rank_out)
    rank_ref[...] = rank_out.astype(I32)
    run = run_ref[...] + jnp.sum(onehot, axis=0, keepdims=True)
    run_ref[...] = run
    cnt_ref[...] = run.astype(I32)


def _outproj(oa, ob, x2, g1, wo, n2_row, sc2, sh2, rw, rb, seq):
    t, d = x2.shape
    tm = 512
    per_b = seq // tm
    tri = jnp.asarray(np.tril(np.ones((tm, tm), np.float32), -1), BF16)
    row_spec = lambda w: pl.BlockSpec((tm, w), lambda i: (i, 0))
    mod_spec = pl.BlockSpec((1, 1, d), lambda i: (i // per_b, 0, 0))
    full = lambda a: pl.BlockSpec(a.shape, lambda i: (0,) * a.ndim)
    return pl.pallas_call(
        _outproj_kernel,
        out_shape=(jax.ShapeDtypeStruct((t, d), F32), jax.ShapeDtypeStruct((t, ROW_TILES, LANES), F32),
                   jax.ShapeDtypeStruct((t, LANES), I32), jax.ShapeDtypeStruct((t, LANES), F32),
                   jax.ShapeDtypeStruct((t, LANES), I32), jax.ShapeDtypeStruct((1, LANES), I32)),
        grid=(t // tm,),
        in_specs=[row_spec(A_WIDTH), row_spec(B_WIDTH), row_spec(d), mod_spec, full(wo),
                  full(n2_row), mod_spec, mod_spec, full(rw), full(rb), full(tri)],
        out_specs=(row_spec(d), pl.BlockSpec((tm, ROW_TILES, LANES), lambda i: (i, 0, 0)),
                   row_spec(LANES), row_spec(LANES), row_spec(LANES),
                   pl.BlockSpec((1, LANES), lambda i: (0, 0))),
        scratch_shapes=[pltpu.VMEM((1, LANES), F32)],
        compiler_params=pltpu.CompilerParams(dimension_semantics=("arbitrary",),
                                             vmem_limit_bytes=VMEM_LIMIT),
        name="outproj",
    )(oa, ob, x2, g1, wo, n2_row, sc2, sh2, rw, rb, tri)


DISPATCH_TOKENS = 256
WAIT_UNROLL = 16


def _drain(make_copy, n):
    def body(g, carry):
        for _ in range(WAIT_UNROLL):
            make_copy().wait()
        return carry

    lax.fori_loop(0, n // WAIT_UNROLL, body, 0)


def _dispatch_kernel(rows_ref, pad_end_ref, h2_ref, out_hbm, zero_ref, sem):
    n = DISPATCH_TOKENS * TOP_K_EXPERTS

    @pl.when(pl.program_id(0) == 0)
    def _():
        zero_ref[...] = jnp.zeros(zero_ref.shape, F32)

        def last_block(e):
            return out_hbm.at[pl.ds(pl.multiple_of(pad_end_ref[e + 1] - EXPERT_BLOCK, EXPERT_BLOCK),
                                    EXPERT_BLOCK)]

        def has_rows(e):
            return pad_end_ref[e + 1] > pad_end_ref[e]

        n_blocks = out_hbm.shape[0] // EXPERT_BLOCK
        min_blocks = n_blocks - N_EXPERTS
        tail = [(b, out_hbm.at[pl.ds(b * EXPERT_BLOCK, EXPERT_BLOCK)]) for b in range(min_blocks, n_blocks)]

        def unused(b):
            return b * EXPERT_BLOCK >= pad_end_ref[N_EXPERTS]

        for e in range(N_EXPERTS):
            @pl.when(has_rows(e))
            def _():
                pltpu.make_async_copy(zero_ref, last_block(e), sem).start()
        for b, dst in tail:
            @pl.when(unused(b))
            def _():
                pltpu.make_async_copy(zero_ref, dst, sem).start()
        for e in range(N_EXPERTS):
            @pl.when(has_rows(e))
            def _():
                pltpu.make_async_copy(zero_ref, last_block(e), sem).wait()
        for b, dst in tail:
            @pl.when(unused(b))
            def _():
                pltpu.make_async_copy(zero_ref, dst, sem).wait()

    def issue(t, carry):
        for k in range(TOP_K_EXPERTS):
            pltpu.make_async_copy(h2_ref.at[t], out_hbm.at[rows_ref[t * TOP_K_EXPERTS + k]],
                                  sem).start(priority=k % 2)
        return carry

    lax.fori_loop(0, DISPATCH_TOKENS, issue, 0)
    _drain(lambda: pltpu.make_async_copy(h2_ref.at[0], out_hbm.at[0], sem), n)


def _dispatch(rows_flat, pad_end0, h2, n_rows):
    t = h2.shape[0]
    n = DISPATCH_TOKENS * TOP_K_EXPERTS
    return pl.pallas_call(
        _dispatch_kernel,
        out_shape=jax.ShapeDtypeStruct((n_rows, ROW_TILES, LANES), F32),
        grid=(t // DISPATCH_TOKENS,),
        in_specs=[pl.BlockSpec((n,), lambda i: (i,), memory_space=pltpu.SMEM),
                  pl.BlockSpec(memory_space=pltpu.SMEM),
                  pl.BlockSpec((DISPATCH_TOKENS, ROW_TILES, LANES), lambda i: (i, 0, 0))],
        out_specs=pl.BlockSpec(memory_space=pl.ANY),
        scratch_shapes=[pltpu.VMEM((EXPERT_BLOCK, ROW_TILES, LANES), F32),
                        pltpu.SemaphoreType.DMA(())],
        compiler_params=pltpu.CompilerParams(dimension_semantics=("arbitrary",)),
        name="dispatch",
    )(rows_flat, pad_end0, h2)


def _experts_kernel(be_ref, nu_ref, run_ref, nxt_ref, x_ref, w1_hbm, b1_ref, w2_hbm, b2_ref, y_ref,
                    w1f_ref, w2f_ref, w1b_ref, w2b_ref, sems):
    i = pl.program_id(0)
    e = be_ref[i]
    prev = be_ref[jnp.maximum(i - 1, 0)]
    changed = jnp.logical_or(i == 0, e != prev)
    used = i < nu_ref[0]
    slot = run_ref[i] % 2

    def weight_copies(expert, s):
        return (pltpu.make_async_copy(w1_hbm.at[expert], w1f_ref.at[s], sems.at[0, s]),
                pltpu.make_async_copy(w2_hbm.at[expert], w2f_ref.at[s], sems.at[1, s]))

    @pl.when(i == 0)
    def _():
        for cp in weight_copies(e, slot):
            cp.start()

    @pl.when(jnp.logical_and(changed, used))
    def _():
        for cp in weight_copies(e, slot):
            cp.wait()

        @pl.when(nxt_ref[i] >= 0)
        def _():
            for cp in weight_copies(nxt_ref[i], 1 - slot):
                cp.start()

        w1b_ref[...] = w1f_ref[slot].astype(BF16)
        w2b_ref[...] = w2f_ref[slot].astype(BF16)

    @pl.when(used)
    def _():
        f = w2b_ref.shape[0]
        hg = _dot(_load_token_tiles(x_ref).astype(BF16), w1b_ref[...]) + b1_ref[0]
        glu = jnp.minimum(hg[:, :f], SWIGLU_LIMIT)
        lin = jnp.clip(hg[:, f:], -SWIGLU_LIMIT, SWIGLU_LIMIT)
        act = glu * _sigmoid(SWIGLU_ALPHA * glu) * (lin + 1.0)
        _store_token_tiles(y_ref, _dot(act.astype(BF16), w2b_ref[...]) + b2_ref[0])

    @pl.when(jnp.logical_not(used))
    def _():
        y_ref[...] = jnp.zeros(y_ref.shape, F32)


def _experts(block_expert, n_used, x_rows, w1, b1, w2, b2):
    n_rows = x_rows.shape[0]
    e, d, f2 = w1.shape
    f = w2.shape[1]
    nb = n_rows // EXPERT_BLOCK
    used = jnp.arange(nb, dtype=I32) < n_used[0]
    starts = jnp.logical_and(used, jnp.concatenate([jnp.ones((1,), bool), block_expert[1:] != block_expert[:-1]]))
    run_id = (jnp.cumsum(starts.astype(I32)) - 1).astype(I32)
    later = jnp.logical_and(block_expert[None, :] > block_expert[:, None], used[None, :])
    nxt = jnp.min(jnp.where(later, block_expert[None, :], N_EXPERTS), axis=1)
    nxt = jnp.where(nxt < N_EXPERTS, nxt, -1).astype(I32)
    tok_spec = pl.BlockSpec((EXPERT_BLOCK, ROW_TILES, LANES), lambda i, *_: (i, 0, 0))
    bias_spec = lambda w: pl.BlockSpec((1, 1, w), lambda i, be, *_: (be[i], 0, 0))
    return pl.pallas_call(
        _experts_kernel,
        out_shape=jax.ShapeDtypeStruct((n_rows, ROW_TILES, LANES), F32),
        grid_spec=pltpu.PrefetchScalarGridSpec(
            num_scalar_prefetch=4,
            grid=(nb,),
            in_specs=[tok_spec, pl.BlockSpec(memory_space=pl.ANY), bias_spec(f2),
                      pl.BlockSpec(memory_space=pl.ANY), bias_spec(d)],
            out_specs=tok_spec,
            scratch_shapes=[pltpu.VMEM((2, d, f2), F32), pltpu.VMEM((2, f, d), F32),
                            pltpu.VMEM((d, f2), BF16), pltpu.VMEM((f, d), BF16),
                            pltpu.SemaphoreType.DMA((2, 2))]),
        compiler_params=pltpu.CompilerParams(dimension_semantics=("arbitrary",),
                                             vmem_limit_bytes=VMEM_LIMIT),
        name="experts",
    )(block_expert, n_used, run_id, nxt, x_rows, w1, b1.reshape(e, 1, f2), w2, b2.reshape(e, 1, d))


COMBINE_TOKENS = 256


def _combine_kernel(rows_ref, next_rows_ref, y_hbm, gate_ref, x1_ref, g2_ref, fg_ref, o_ref, buf_ref, sems):
    n = COMBINE_TOKENS * TOP_K_EXPERTS
    i = pl.program_id(0)
    slot = i % 2

    def gather(rows, s):
        def issue(t, carry):
            for k in range(TOP_K_EXPERTS):
                pltpu.make_async_copy(y_hbm.at[rows[t * TOP_K_EXPERTS + k]], buf_ref.at[s, k, t],
                                      sems.at[s]).start(priority=k % 2)
            return carry

        lax.fori_loop(0, COMBINE_TOKENS, issue, 0)

    @pl.when(i == 0)
    def _():
        gather(rows_ref, 0)

    @pl.when(i + 1 < pl.num_programs(0))
    def _():
        gather(next_rows_ref, 1 - slot)

    _drain(lambda: pltpu.make_async_copy(y_hbm.at[0], buf_ref.at[slot, 0, 0], sems.at[slot]), n)

    gates = gate_ref[...]
    moe = gates[:, 0:1] * _load_token_tiles(buf_ref.at[slot, 0])
    for k in range(1, TOP_K_EXPERTS):
        moe = moe + gates[:, k:k + 1] * _load_token_tiles(buf_ref.at[slot, k])
    x2 = x1_ref[...] + g2_ref[0] * moe
    ms = jnp.mean(x2 * x2, axis=-1, keepdims=True)
    o_ref[...] = x2 * lax.rsqrt(ms + RMS_EPS) * fg_ref[...]


def _combine(rows_flat, y_rows, gates, x1, g2, fg_row, seq):
    t, d = x1.shape
    tm = COMBINE_TOKENS
    per_b = seq // tm
    n = tm * TOP_K_EXPERTS
    steps = t // tm
    return pl.pallas_call(
        _combine_kernel,
        out_shape=jax.ShapeDtypeStruct((t, d), F32),
        grid=(steps,),
        in_specs=[pl.BlockSpec((n,), lambda i: (i,), memory_space=pltpu.SMEM),
                  pl.BlockSpec((n,), lambda i: (jnp.minimum(i + 1, steps - 1),), memory_space=pltpu.SMEM),
                  pl.BlockSpec(memory_space=pl.ANY),
                  pl.BlockSpec((tm, LANES), lambda i: (i, 0)),
                  pl.BlockSpec((tm, d), lambda i: (i, 0)),
                  pl.BlockSpec((1, 1, d), lambda i: (i // per_b, 0, 0)),
                  pl.BlockSpec((1, d), lambda i: (0, 0))],
        out_specs=pl.BlockSpec((tm, d), lambda i: (i, 0)),
        scratch_shapes=[pltpu.VMEM((2, TOP_K_EXPERTS, tm, ROW_TILES, LANES), F32),
                        pltpu.SemaphoreType.DMA((2,))],
        compiler_params=pltpu.CompilerParams(dimension_semantics=("arbitrary",),
                                             vmem_limit_bytes=VMEM_LIMIT),
        name="combine",
    )(rows_flat, rows_flat, y_rows, gates, x1, g2, fg_row)


def kernel(x, c, positions, ada_w, ada_b, norm1_g, w_in, hg_norm_g, lb_logits, w_out, norm2_g,
           router_w, router_b, moe_w1, moe_b1, moe_w2, moe_b2, final_g):
    batch, seq, d = x.shape
    t = batch * seq
    layer = 0
    x2 = x.reshape(t, d)

    c_pad = jnp.concatenate([c, jnp.zeros((SUBLANES - batch, d), F32)], axis=0)
    mod = _adaln(c_pad, ada_w[layer], ada_b[layer][None, :])[:batch]
    shift1, scale1, gate1, shift2, scale2, gate2 = jnp.split(mod, 6, axis=-1)
    row3 = lambda m: m[:, None, :]

    inv_freq = ROPE_THETA ** (-(jnp.arange(0, ROT_DIM, 2, dtype=F32) / ROT_DIM))
    cos_t, sin_t = _trig(positions.reshape(1, t).astype(F32), inv_freq[:, None])

    wl = w_in[layer]
    sp = np.cumsum((A_WIDTH, A_WIDTH, A_WIDTH, IDX_HEADS * IDX_DIM, IDX_DIM, IDX_HEADS,
                    B_HEADS * B_KEY_DIM, B_HEADS * B_KEY_DIM, B_WIDTH))
    w_qa, w_ka, w_va, w_qi, w_ki, w_wi, w_qb, w_fb, w_ib, w_gb = jnp.split(wl, [int(v) for v in sp], axis=1)
    w_t = jnp.concatenate([w_qa, w_va, w_qi, w_wi, jnp.zeros((d, SUBLANES - IDX_HEADS), F32)], axis=1)
    w_t = w_t.astype(BF16).T
    w_r = jnp.concatenate([w_ka, w_ki, jnp.zeros((d, LANES - IDX_DIM), F32), w_qb, w_fb, w_ib, w_gb],
                          axis=1).astype(BF16)

    g1n = norm1_g[layer]
    ka, ki, qb, fb, ib, gb, qa_t, v_blk, qi_t, wi_t = _inproj(
        x2, g1n[None, :], row3(scale1), row3(shift1), w_r, w_t, cos_t, sin_t, seq)

    out_a = _dsa(qi_t, wi_t, ki, qa_t, ka, v_blk, batch, seq)

    lower = jnp.cumsum(jax.nn.softmax(lb_logits.astype(F32), axis=0), axis=0)[layer]
    out_b = _hgrn(qb, fb, ib, gb, lower[None, :], hg_norm_g[layer][None, :], batch, seq)

    rw = jnp.concatenate([router_w[layer], jnp.zeros((d, LANES - N_EXPERTS), F32)], axis=1)
    rw_hi = rw.astype(BF16)
    rw_lo = (rw - rw_hi.astype(F32)).astype(BF16)
    rw = jnp.concatenate([rw_hi, rw_lo, rw_hi], axis=0)
    rb = jnp.concatenate([router_b[layer], jnp.full((LANES - N_EXPERTS,), NEG_BIG, F32)])[None, :]
    x1, h2, idx, gates, rank, counts = _outproj(
        out_a, out_b, x2, row3(gate1), w_out[layer].astype(BF16), norm2_g[layer][None, :],
        row3(scale2), row3(shift2), rw, rb, seq)

    counts = counts[0, :N_EXPERTS]
    padded = (counts + EXPERT_BLOCK - 1) // EXPERT_BLOCK * EXPERT_BLOCK
    pad_end = jnp.cumsum(padded)
    pad_start = pad_end - padded
    n_assign = t * TOP_K_EXPERTS
    n_blocks = -(-n_assign // EXPERT_BLOCK) + N_EXPERTS
    n_rows = n_blocks * EXPERT_BLOCK
    block_start = jnp.arange(n_blocks, dtype=I32) * EXPERT_BLOCK
    block_expert = jnp.minimum(jnp.sum(pad_end[None, :] <= block_start[:, None], axis=1),
                               N_EXPERTS - 1).astype(I32)
    n_used = (pad_end[-1:] // EXPERT_BLOCK).astype(I32)
    idx4 = idx[:, :TOP_K_EXPERTS]
    start_of = jnp.sum(jnp.where(idx4[:, :, None] == jnp.arange(N_EXPERTS, dtype=I32), pad_start.astype(I32), 0),
                       axis=-1)
    rows_flat = (start_of + rank[:, :TOP_K_EXPERTS]).astype(I32).reshape(-1)

    pad_end0 = jnp.concatenate([jnp.zeros((1,), I32), pad_end.astype(I32)])
    x_rows = _dispatch(rows_flat, pad_end0, h2, n_rows)
    y_rows = _experts(block_expert, n_used, x_rows, moe_w1[layer], moe_b1[layer],
                      moe_w2[layer], moe_b2[layer])
    out = _combine(rows_flat, y_rows, gates, x1, row3(gate2), final_g[None, :], seq)
    return out.reshape(batch, seq, d)
```

```python
import functools

import numpy as np
import jax
import jax.numpy as jnp
from jax import lax
from jax.experimental import pallas as pl
from jax.experimental.pallas import tpu as pltpu

F32 = jnp.float32
BF16 = jnp.bfloat16
I32 = jnp.int32
I16 = jnp.int16
HALF_OFFSET = 2 ** 15
HIGHEST = lax.Precision.HIGHEST

D_MODEL = 1024
CHUNK = 64
A_HEADS = 8
A_HEAD_DIM = 64
A_WIDTH = A_HEADS * A_HEAD_DIM
IDX_HEADS = 4
IDX_DIM = 64
TOPK_MAX = 256
B_HEADS = 4
B_KEY_DIM = 64
B_VAL_DIM = 128
B_WIDTH = B_HEADS * B_VAL_DIM
ROPE_THETA = 500000.0
ROT_DIM = A_HEAD_DIM // 4
ROT_HALF = ROT_DIM // 2
N_EXPERTS = 32
TOP_K_EXPERTS = 4
SWIGLU_LIMIT = 7.0
SWIGLU_ALPHA = 1.702
EXPERT_BLOCK = 256
RMS_EPS = 1e-6

LANES = 128
SUBLANES = 8
VMEM_LIMIT = 56 * 1024 * 1024

NEG_BIG = -1e30
INT_MIN = -(2 ** 31)
INT_MAX = 2 ** 31 - 1
LOG2E = 1.4426950408889634
IDX_K = 4 * IDX_DIM

DSA_TQ = 256
DSA_KB = 256
HG_STEP_CHUNKS = 8
HG_GROUP = 4
HG_LEVELS = (32, 16, 8, 4, 2, 1)


def _dot(a, b, precision=None):
    return jnp.dot(a, b, preferred_element_type=F32, precision=precision)


def _dot_nt(a, b, precision=None):
    return lax.dot_general(a, b, (((1,), (1,)), ((), ())),
                           preferred_element_type=F32, precision=precision)


def _sigmoid(x):
    return 1.0 / (1.0 + jnp.exp(-x))


def _split_bf16(x):
    hi = x.astype(BF16).astype(F32)
    lo = (x - hi).astype(BF16).astype(F32)
    return hi, lo


ROW_TILES = D_MODEL // LANES


def _load_token_tiles(ref3):
    tiles = jnp.swapaxes(ref3[...], 0, 1)
    return jnp.concatenate([tiles[s] for s in range(ROW_TILES)], axis=1)


def _store_token_tiles(ref3, val):
    for s in range(ROW_TILES):
        ref3[:, s, :] = val[:, s * LANES:(s + 1) * LANES]


def _adaln_kernel(c_ref, w_ref, b_ref, o_ref):
    c = c_ref[...]
    o_ref[...] = _dot(c * _sigmoid(c), w_ref[...], HIGHEST) + b_ref[...]


def _adaln(c_pad, ada_w, ada_b):
    d = c_pad.shape[1]
    n = ada_w.shape[1]
    bn = 1024
    return pl.pallas_call(
        _adaln_kernel,
        out_shape=jax.ShapeDtypeStruct((c_pad.shape[0], n), F32),
        grid=(n // bn,),
        in_specs=[pl.BlockSpec((c_pad.shape[0], d), lambda j: (0, 0)),
                  pl.BlockSpec((d, bn), lambda j: (0, j)),
                  pl.BlockSpec((1, bn), lambda j: (0, j))],
        out_specs=pl.BlockSpec((c_pad.shape[0], bn), lambda j: (0, j)),
        name="adaln",
    )(c_pad, ada_w, ada_b)


def _trig_kernel(pos_ref, freq_ref, cos_ref, sin_ref):
    ang = pos_ref[...] * freq_ref[...]
    cos_ref[...] = jnp.cos(ang)
    sin_ref[...] = jnp.sin(ang)


def _trig(pos_row, freq_col):
    t = pos_row.shape[1]
    bt = 2048
    return pl.pallas_call(
        _trig_kernel,
        out_shape=(jax.ShapeDtypeStruct((ROT_HALF, t), F32),) * 2,
        grid=(t // bt,),
        in_specs=[pl.BlockSpec((1, bt), lambda i: (0, i)),
                  pl.BlockSpec((ROT_HALF, 1), lambda i: (0, 0))],
        out_specs=(pl.BlockSpec((ROT_HALF, bt), lambda i: (0, i)),) * 2,
        name="trig",
    )(pos_row, freq_col)


def _rope_rows(p, c, s):
    x1 = p[0:ROT_HALF]
    x2 = p[ROT_HALF:ROT_DIM]
    return jnp.concatenate([x1 * c - x2 * s, x2 * c + x1 * s, p[ROT_DIM:]], axis=0)


def _inproj_kernel(x_ref, g_ref, sc_ref, sh_ref, wr_ref, w_ref, cos_ref, sin_ref, cosr_ref, sinr_ref,
                   ec_ref, es_ref,
                   ka_ref, ki_ref, qb_ref, fb_ref, ib_ref, gb_ref, qa_ref, v_ref, qi_ref, wt_ref):
    x = x_ref[...]
    ms = jnp.mean(x * x, axis=-1, keepdims=True)
    h = x * lax.rsqrt(ms + RMS_EPS) * g_ref[...]
    h = h * (1.0 + sc_ref[0]) + sh_ref[0]
    hb = h.astype(BF16)
    _inproj_rows(hb, wr_ref, cosr_ref, sinr_ref, ec_ref, es_ref,
                 ka_ref, ki_ref, qb_ref, fb_ref, ib_ref, gb_ref)
    c = cos_ref[...]
    s = sin_ref[...]
    pq = _dot_nt(w_ref[0:A_WIDTH, :], hb)
    for hh in range(A_HEADS):
        r = _rope_rows(pq[hh * 64:(hh + 1) * 64], c, s) * (A_HEAD_DIM ** -0.5 * LOG2E)
        qa_ref[hh * 64:(hh + 1) * 64, :] = r.astype(BF16)
    pv = _dot_nt(w_ref[A_WIDTH:2 * A_WIDTH, :], hb)
    for cb in range(v_ref.shape[0]):
        v_ref[cb] = pv[:, cb * DSA_KB:(cb + 1) * DSA_KB].astype(BF16)
    pi = _dot_nt(w_ref[2 * A_WIDTH:2 * A_WIDTH + 256, :], hb)
    for hh in range(IDX_HEADS):
        q = _rope_rows(pi[hh * 64:(hh + 1) * 64], c, s) * (IDX_DIM ** -0.5)
        q_hi, q_lo = _split_bf16(q)
        qi_ref[hh * IDX_K:(hh + 1) * IDX_K, :] = jnp.concatenate(
            [q_hi, q_hi, q_lo, jnp.zeros_like(q_hi)], axis=0).astype(BF16)
    pw = _dot_nt(w_ref[2 * A_WIDTH + 256:2 * A_WIDTH + 264, :], hb)
    wt_ref[...] = pw * (IDX_HEADS ** -0.5)


def _rope_lane_tables():
    lane = np.arange(LANES) % A_HEAD_DIM
    i = np.arange(ROT_HALF)[:, None]
    first, second = lane[None, :] == i, lane[None, :] == i + ROT_HALF
    ec = (first | second).astype(np.float32)
    es = second.astype(np.float32) - first.astype(np.float32)
    ones = (lane >= ROT_DIM).astype(np.float32)[None, :]
    return ec, es, ones


def _inproj_rows(hb, w_ref, cosr_ref, sinr_ref, ec_ref, es_ref,
                 ka_ref, ki_ref, qb_ref, fb_ref, ib_ref, gb_ref):
    c = _dot(cosr_ref[...], ec_ref[0:ROT_HALF, :], HIGHEST) + ec_ref[ROT_HALF:ROT_HALF + 1, :]
    s = _dot(sinr_ref[...], es_ref[...], HIGHEST)
    lane = lax.broadcasted_iota(I32, c.shape, 1)
    first = (lane % A_HEAD_DIM) < ROT_HALF

    def rope(p):
        partner = jnp.where(first, pltpu.roll(p, LANES - ROT_HALF, 1), pltpu.roll(p, ROT_HALF, 1))
        return p * c + partner * s

    for j in range(A_WIDTH // LANES):
        p = _dot(hb, w_ref[:, j * LANES:(j + 1) * LANES])
        ka_ref[:, j * LANES:(j + 1) * LANES] = rope(p).astype(BF16)
    o = A_WIDTH
    k_hi, k_lo = _split_bf16(rope(_dot(hb, w_ref[:, o:o + LANES])))
    ki_ref[:, 0:LANES] = (k_hi + pltpu.roll(k_lo, IDX_DIM, 1)).astype(BF16)
    ki_ref[:, LANES:2 * LANES] = k_hi.astype(BF16)
    o += LANES
    qb_ref[...] = _dot(hb, w_ref[:, o:o + 256])
    o += 256
    fb_ref[...] = _dot(hb, w_ref[:, o:o + 256])
    o += 256
    ib_ref[...] = _dot(hb, w_ref[:, o:o + 512])
    o += 512
    gb_ref[...] = _dot(hb, w_ref[:, o:o + 512])


def _inproj(x2, g_row, sc_row, sh_row, w_r, w_t, cos_t, sin_t, seq):
    t, d = x2.shape
    tm = 512
    per_b = seq // tm
    ec, es, ones = _rope_lane_tables()
    ec = jnp.asarray(np.concatenate([ec, ones, np.zeros((SUBLANES - 1, LANES), np.float32)]))
    es = jnp.asarray(es)
    widths = (A_WIDTH, IDX_K, 256, 256, 512, 512)
    dts = (BF16, BF16, F32, F32, F32, F32)
    row_outs = tuple(jax.ShapeDtypeStruct((t, w), dt) for w, dt in zip(widths, dts))
    col_outs = (jax.ShapeDtypeStruct((A_WIDTH, t), BF16),
                jax.ShapeDtypeStruct((t // DSA_KB, A_WIDTH, DSA_KB), BF16),
                jax.ShapeDtypeStruct((IDX_HEADS * IDX_K, t), BF16),
                jax.ShapeDtypeStruct((SUBLANES, t), F32))
    full = lambda a: pl.BlockSpec(a.shape, lambda i: (0,) * a.ndim)
    mod_spec = pl.BlockSpec((1, 1, d), lambda i: (i // per_b, 0, 0))
    return pl.pallas_call(
        _inproj_kernel,
        out_shape=row_outs + col_outs,
        grid=(t // tm,),
        in_specs=[pl.BlockSpec((tm, d), lambda i: (i, 0)), full(g_row), mod_spec, mod_spec,
                  full(w_r), full(w_t),
                  pl.BlockSpec((ROT_HALF, tm), lambda i: (0, i)),
                  pl.BlockSpec((ROT_HALF, tm), lambda i: (0, i)),
                  pl.BlockSpec((tm, ROT_HALF), lambda i: (i, 0)),
                  pl.BlockSpec((tm, ROT_HALF), lambda i: (i, 0)),
                  full(ec), full(es)],
        out_specs=tuple(pl.BlockSpec((tm, w), lambda i: (i, 0)) for w in widths)
        + (pl.BlockSpec((A_WIDTH, tm), lambda i: (0, i)),
           pl.BlockSpec((tm // DSA_KB, A_WIDTH, DSA_KB), lambda i: (i, 0, 0)),
           pl.BlockSpec((IDX_HEADS * IDX_K, tm), lambda i: (0, i)),
           pl.BlockSpec((SUBLANES, tm), lambda i: (0, i))),
        compiler_params=pltpu.CompilerParams(vmem_limit_bytes=VMEM_LIMIT),
        name="inproj",
    )(x2, g_row, sc_row, sh_row, w_r, w_t, cos_t, sin_t, cos_t.T, sin_t.T, ec, es)


def _dsa_kernel(qi_ref, wt_ref, ki_ref, qa_ref, ka_ref, v_ref, tri_ref, o_ref,
                keys_ref, m_ref, l_ref, acc_ref, s_ref, p_ref, kh_ref, kl_ref):
    j = pl.program_id(1)
    nkb = j + 1
    tq = DSA_TQ
    kb_rows = DSA_KB
    row = lax.broadcasted_iota(I32, (kb_rows, tq), 0)
    col = lax.broadcasted_iota(I32, (kb_rows, tq), 1)
    q_chunk = (j * tq + col) // CHUNK

    def score_keys(kb):
        r0 = pl.multiple_of(kb * kb_rows, kb_rows)
        ki = ki_ref[pl.ds(r0, kb_rows), :]
        sc = jnp.zeros((kb_rows, tq), F32)
        for h in range(IDX_HEADS):
            lg = _dot(ki, qi_ref[h * IDX_K:(h + 1) * IDX_K, :])
            sc = sc + wt_ref[h:h + 1, :] * jnp.maximum(lg, 0.0)
        sc = jnp.where(sc == 0.0, 0.0, sc)
        bits = pltpu.bitcast(sc, I32)
        return jnp.where(bits < 0, bits ^ 0x7FFFFFFF, bits)

    def store_keys(kb, key):
        keys_ref[kb] = key
        kh_ref[kb] = (key >> 16).astype(I16)
        kl_ref[kb] = ((key & 0xFFFF) - HALF_OFFSET).astype(I16)

    def score_block(kb, carry):
        store_keys(kb, score_keys(kb))
        return carry

    lax.fori_loop(0, j, score_block, 0)
    adm = ((j * kb_rows + row) // CHUNK) <= q_chunk
    store_keys(j, jnp.where(adm, score_keys(j), INT_MIN))

    def count16(ref, pred_fn):
        rows16 = 2 * SUBLANES

        def body(kb, acc):
            hit = jnp.where(pred_fn(ref[kb]), jnp.int16(1), jnp.int16(0))
            for r in range(kb_rows // rows16):
                acc = acc + hit[r * rows16:(r + 1) * rows16]
            return acc
        acc = lax.fori_loop(0, nkb, body, jnp.zeros((rows16, tq), I16))
        return jnp.sum(acc.astype(I32), axis=0, keepdims=True)

    def search16(ref, need):
        def step(i, u):
            cand = u | lax.shift_left(jnp.int32(1), 15 - i)
            cand16 = (cand - HALF_OFFSET).astype(I16)
            cnt = count16(ref, lambda k: k >= cand16)
            return jnp.where(cnt >= need, cand, u)
        return lax.fori_loop(0, 16, step, jnp.zeros((1, tq), I32))

    u_hi = search16(kh_ref, TOPK_MAX)
    t_hi = (u_hi - HALF_OFFSET).astype(I16)
    above = count16(kh_ref, lambda k: k > t_hi)

    def mask_low(kb, carry):
        kl_ref[kb] = jnp.where(kh_ref[kb] == t_hi, kl_ref[kb], jnp.int16(-HALF_OFFSET))
        return carry

    lax.fori_loop(0, nkb, mask_low, 0)
    u_lo = search16(kl_ref, TOPK_MAX - above)
    t_lo = (u_lo - HALF_OFFSET).astype(I16)
    thr = (u_hi - HALF_OFFSET) * (2 * HALF_OFFSET) + u_lo
    n_gt = above + count16(kl_ref, lambda k: k > t_lo)
    need = (TOPK_MAX - n_gt).astype(F32)

    def strike_ties(kb, run):
        key = keys_ref[kb]
        tie = key == thr
        tie_f = jnp.where(tie, 1.0, 0.0)
        rank = _dot(tri_ref[...], tie_f.astype(BF16)) + run
        keys_ref[kb] = jnp.where(jnp.logical_and(tie, rank > need), INT_MIN, key)
        return run + jnp.sum(tie_f, axis=0, keepdims=True)

    lax.fori_loop(0, nkb, strike_ties, jnp.zeros((1, tq), F32))
    thr_sel = jnp.maximum(thr, INT_MIN + 1)

    m_ref[...] = jnp.full(m_ref.shape, NEG_BIG, F32)
    l_ref[...] = jnp.zeros(l_ref.shape, F32)
    acc_ref[...] = jnp.zeros(acc_ref.shape, F32)

    def attn_block(kb, carry):
        r0 = pl.multiple_of(kb * kb_rows, kb_rows)
        bias = jnp.where(keys_ref[kb] >= thr_sel, 0.0, NEG_BIG)
        heads = [slice(h * A_HEAD_DIM, (h + 1) * A_HEAD_DIM) for h in range(A_HEADS)]
        for h, hs in enumerate(heads):
            s_ref[h] = _dot(ka_ref[pl.ds(r0, kb_rows), hs], qa_ref[hs, :])
        alphas = []
        for h, hs in enumerate(heads):
            s = s_ref[h] + bias
            m_old = m_ref[h][0:1, :]
            m_new = jnp.maximum(m_old, jnp.max(s, axis=0, keepdims=True))
            alpha = jnp.exp2(m_old - m_new)
            p = jnp.exp2(s - m_new)
            l_new = alpha * l_ref[h][0:1, :] + jnp.sum(p, axis=0, keepdims=True)
            p_ref[h] = p.astype(BF16)
            m_ref[h] = jnp.broadcast_to(m_new, (SUBLANES, tq))
            l_ref[h] = jnp.broadcast_to(l_new, (SUBLANES, tq))
            alphas.append(alpha)
        for h, hs in enumerate(heads):
            acc_ref[hs, :] = alphas[h] * acc_ref[hs, :] + _dot(v_ref[kb, hs, :], p_ref[h])
        return carry

    lax.fori_loop(0, nkb, attn_block, 0)

    for h in range(A_HEADS):
        hs = slice(h * A_HEAD_DIM, (h + 1) * A_HEAD_DIM)
        acc_ref[hs, :] = acc_ref[hs, :] / l_ref[h][0:1, :]
    o_ref[...] = acc_ref[...].T


def _dsa(qi_t, w_t, ki, qa_t, ka, v_blk, batch, seq):
    t = ka.shape[0]
    nqb = seq // DSA_TQ
    nkb = seq // DSA_KB
    return pl.pallas_call(
        _dsa_kernel,
        out_shape=jax.ShapeDtypeStruct((t, A_WIDTH), F32),
        grid=(batch, nqb),
        in_specs=[pl.BlockSpec((IDX_HEADS * IDX_K, DSA_TQ), lambda b, j: (0, b * nqb + j)),
                  pl.BlockSpec((SUBLANES, DSA_TQ), lambda b, j: (0, b * nqb + j)),
                  pl.BlockSpec((seq, IDX_K), lambda b, j: (b, 0)),
                  pl.BlockSpec((A_WIDTH, DSA_TQ), lambda b, j: (0, b * nqb + j)),
                  pl.BlockSpec((seq, A_WIDTH), lambda b, j: (b, 0)),
                  pl.BlockSpec((nkb, A_WIDTH, DSA_KB), lambda b, j: (b, 0, 0)),
                  pl.BlockSpec((DSA_KB, DSA_KB), lambda b, j: (0, 0))],
        out_specs=pl.BlockSpec((DSA_TQ, A_WIDTH), lambda b, j: (b * nqb + j, 0)),
        scratch_shapes=[pltpu.VMEM((nkb, DSA_KB, DSA_TQ), I32),
                        pltpu.VMEM((A_HEADS, SUBLANES, DSA_TQ), F32),
                        pltpu.VMEM((A_HEADS, SUBLANES, DSA_TQ), F32),
                        pltpu.VMEM((A_WIDTH, DSA_TQ), F32),
                        pltpu.VMEM((A_HEADS, DSA_KB, DSA_TQ), F32),
                        pltpu.VMEM((A_HEADS, DSA_KB, DSA_TQ), BF16),
                        pltpu.VMEM((nkb, DSA_KB, DSA_TQ), I16),
                        pltpu.VMEM((nkb, DSA_KB, DSA_TQ), I16)],
        compiler_params=pltpu.CompilerParams(vmem_limit_bytes=VMEM_LIMIT),
        name="dsa",
    )(qi_t, w_t, ki, qa_t, ka, v_blk, jnp.asarray(np.tril(np.ones((DSA_KB, DSA_KB), np.float32)), BF16))


def _hgrn_constants():
    c = CHUNK
    t = np.arange(c)[:, None]
    u = np.arange(c)[None, :]
    blocks = [(u <= t), (u > t)]
    masks = []
    for half in HG_LEVELS:
        mid = (t // (2 * half)) * (2 * half) + half - 1
        right = ((t // half) % 2) == 1
        blocks.append(right & (u > mid) & (u <= t))
        blocks.append((~right) & (u > t) & (u <= mid))
        tt, ss = t, u
        masks.append(((tt // (2 * half)) == (ss // (2 * half)))
                     & ((((tt // half) % 2) == 1) & (((ss // half) % 2) == 0)))
    masks.append(t == u)
    cm = np.concatenate(blocks, axis=0).astype(np.float32)
    cm3 = np.concatenate([cm, cm, cm], axis=1)
    masks4 = np.tile(np.stack(masks).astype(np.float32), (1, 1, B_HEADS))
    head_of = lambda n, per: np.arange(n) // per
    bd_k = (head_of(B_HEADS * c, c)[:, None] == head_of(B_HEADS * B_KEY_DIM, B_KEY_DIM)[None, :])
    bd_v = (head_of(B_HEADS * c, c)[:, None] == head_of(B_WIDTH, B_VAL_DIM)[None, :])
    bd_s = (head_of(B_WIDTH, B_VAL_DIM)[:, None] == head_of(B_HEADS * B_KEY_DIM, B_KEY_DIM)[None, :])
    return cm3, masks4, bd_k.astype(np.float32), bd_v.astype(np.float32), bd_s.astype(np.float32)


def _hgrn_kernel(qb_ref, fb_ref, ib_ref, gb_ref, lb_ref, hg_ref, cm_ref, mask_ref, bdk_ref, bdv_ref,
                 bds_ref, eye_ref, o_ref, state_ref, lv_ref):
    @pl.when(pl.program_id(1) == 0)
    def _():
        state_ref[...] = jnp.zeros(state_ref.shape, F32)

    lb = lb_ref[...]
    c = CHUNK
    nlev = len(HG_LEVELS)
    seq_refs = (qb_ref, fb_ref, ib_ref, gb_ref, o_ref, state_ref, lv_ref)

    def stack_heads(x, bd_ref):
        return (jnp.concatenate([x] * B_HEADS, axis=0) * bd_ref[...]).astype(BF16)

    def chunk_step(ci, carry):
        for bi in range(qb_ref.shape[0]):
            one_chunk(bi, pl.multiple_of(ci * c, c))
        return carry

    def one_chunk(bi, r0):
        qb_ref, fb_ref, ib_ref, gb_ref, o_ref, state_ref, lv_ref = [r.at[bi] for r in seq_refs]
        f = lb + (1.0 - lb) * _sigmoid(fb_ref[pl.ds(r0, c), :])
        lf = jnp.log(f)
        kin = 1.0 - f
        hi = lf.astype(BF16)
        r1 = lf - hi.astype(F32)
        mid = r1.astype(BF16)
        lo = (r1 - mid.astype(F32)).astype(BF16)
        e = jnp.exp(_dot(cm_ref[...], jnp.concatenate([hi, mid, lo], axis=0)))
        q = qb_ref[pl.ds(r0, c), :]
        q_in = (q * e[0:c]).astype(BF16)
        k_out = (kin * e[c:2 * c]).astype(BF16)
        e_last = e[c - 1:c, :]
        for i in range(nlev + 1):
            if i < nlev:
                q_i = q * e[(2 + 2 * i) * c:(3 + 2 * i) * c]
                k_i = kin * e[(3 + 2 * i) * c:(4 + 2 * i) * c]
            else:
                q_i, k_i = q, kin
            lv_ref[i] = _dot_nt(q_i.astype(BF16), stack_heads(k_i, bdk_ref))
        attn = mask_ref[0] * lv_ref[0]
        for i in range(1, nlev + 1):
            attn = attn + mask_ref[i] * lv_ref[i]
        v = ib_ref[pl.ds(r0, c), :]
        st = state_ref[...]
        o_all = _dot_nt(q_in, st.astype(BF16)) + _dot(attn.astype(BF16), stack_heads(v, bdv_ref))
        v_t = _dot_nt(eye_ref[...], v.astype(BF16)).astype(BF16)
        state_ref[...] = st * e_last + bds_ref[...] * _dot(v_t, k_out)
        for h in range(B_HEADS):
            vs = slice(h * B_VAL_DIM, (h + 1) * B_VAL_DIM)
            o = o_all[:, vs]
            ms = jnp.mean(o * o, axis=-1, keepdims=True)
            y = o * lax.rsqrt(ms + RMS_EPS) * hg_ref[...]
            g = gb_ref[pl.ds(r0, c), vs]
            o_ref[pl.ds(r0, c), vs] = y * (g * _sigmoid(g))

    lax.fori_loop(0, HG_STEP_CHUNKS, chunk_step, 0)


def _hgrn(qb, fb, ib, gb, lb_row, hg_row, batch, seq):
    t = qb.shape[0]
    tm = HG_STEP_CHUNKS * CHUNK
    per_b = seq // tm
    cm3, masks4, bd_k, bd_v, bd_s = _hgrn_constants()
    consts = (jnp.asarray(cm3, BF16), jnp.asarray(masks4, F32), jnp.asarray(bd_k, F32),
              jnp.asarray(bd_v, F32), jnp.asarray(bd_s, F32), jnp.eye(B_WIDTH, dtype=BF16))
    kd = B_HEADS * B_KEY_DIM
    grp = HG_GROUP if batch % HG_GROUP == 0 else 1
    tok_spec = lambda w: pl.BlockSpec((grp, tm, w), lambda b, i: (b, i, 0))
    full = lambda a: pl.BlockSpec(a.shape, lambda b, i: (0,) * a.ndim)
    seqs = [a.reshape(batch, seq, a.shape[1]) for a in (qb, fb, ib, gb)]
    out = pl.pallas_call(
        _hgrn_kernel,
        out_shape=jax.ShapeDtypeStruct((batch, seq, B_WIDTH), F32),
        grid=(batch // grp, per_b),
        in_specs=[tok_spec(kd), tok_spec(kd), tok_spec(B_WIDTH), tok_spec(B_WIDTH),
                  full(lb_row), full(hg_row)] + [full(a) for a in consts],
        out_specs=tok_spec(B_WIDTH),
        scratch_shapes=[pltpu.VMEM((grp, B_WIDTH, kd), F32),
                        pltpu.VMEM((grp, len(HG_LEVELS) + 1, CHUNK, B_HEADS * CHUNK), F32)],
        compiler_params=pltpu.CompilerParams(vmem_limit_bytes=VMEM_LIMIT),
        name="hgrn",
    )(*seqs, lb_row, hg_row, *consts)
    return out.reshape(t, B_WIDTH)


def _outproj_kernel(oa_ref, ob_ref, x_ref, g1_ref, wo_ref, n2_ref, sc_ref, sh_ref,
                    rw_ref, rb_ref, tri_ref,
                    x1_ref, h2_ref, idx_ref, gate_ref, rank_ref, cnt_ref, run_ref):
    @pl.when(pl.program_id(0) == 0)
    def _():
        run_ref[...] = jnp.zeros(run_ref.shape, F32)

    half = A_WIDTH
    mix = _dot(oa_ref[...].astype(BF16), wo_ref[0:half, :]) + _dot(ob_ref[...].astype(BF16), wo_ref[half:, :])
    x1 = x_ref[...] + g1_ref[0] * mix
    x1_ref[...] = x1
    ms = jnp.mean(x1 * x1, axis=-1, keepdims=True)
    h2 = x1 * lax.rsqrt(ms + RMS_EPS) * n2_ref[...]
    h2 = h2 * (1.0 + sc_ref[0]) + sh_ref[0]
    _store_token_tiles(h2_ref, h2)
    h_hi, h_lo = _split_bf16(h2)
    logits = _dot(jnp.concatenate([h_hi, h_hi, h_lo], axis=1).astype(BF16), rw_ref[...]) + rb_ref[...]
    tm = logits.shape[0]
    lane = lax.broadcasted_iota(I32, (tm, LANES), 1)
    work = logits
    vals, idxs = [], []
    for _ in range(TOP_K_EXPERTS):
        m = jnp.max(work, axis=-1, keepdims=True)
        ix = jnp.min(jnp.where(work == m, lane, LANES), axis=-1, keepdims=True)
        vals.append(m)
        idxs.append(ix)
        work = jnp.where(lane == ix, -jnp.inf, work)
    es = [jnp.exp(v - vals[0]) for v in vals]
    tot = es[0] + es[1] + es[2] + es[3]
    onehot = jnp.zeros((tm, LANES), F32)
    idx_out = jnp.zeros((tm, LANES), I32)
    gate_out = jnp.zeros((tm, LANES), F32)
    for k in range(TOP_K_EXPERTS):
        onehot = onehot + jnp.where(lane == idxs[k], 1.0, 0.0)
        idx_out = jnp.where(lane == k, idxs[k], idx_out)
        gate_out = jnp.where(lane == k, es[k] / tot, gate_out)
    idx_ref[...] = idx_out
    gate_ref[...] = gate_out
    before = _dot(tri_ref[...], onehot.astype(BF16)) + run_ref[...]
    rank_out = jnp.zeros((tm, LANES), F32)
    for k in range(TOP_K_EXPERTS):
        rk = jnp.sum(jnp.where(lane == idxs[k], before, 0.0), axis=-1, keepdims=True)
        rank_out = jnp.where(lane == k, rk, rank_out)
    rank_ref[...] = rank_out.astype(I32)
    run = run_ref[...] + jnp.sum(onehot, axis=0, keepdims=True)
    run_ref[...] = run
    cnt_ref[...] = run.astype(I32)


def _outproj(oa, ob, x2, g1, wo, n2_row, sc2, sh2, rw, rb, seq):
    t, d = x2.shape
    tm = 512
    per_b = seq // tm
    tri = jnp.asarray(np.tril(np.ones((tm, tm), np.float32), -1), BF16)
    row_spec = lambda w: pl.BlockSpec((tm, w), lambda i: (i, 0))
    mod_spec = pl.BlockSpec((1, 1, d), lambda i: (i // per_b, 0, 0))
    full = lambda a: pl.BlockSpec(a.shape, lambda i: (0,) * a.ndim)
    return pl.pallas_call(
        _outproj_kernel,
        out_shape=(jax.ShapeDtypeStruct((t, d), F32), jax.ShapeDtypeStruct((t, ROW_TILES, LANES), F32),
                   jax.ShapeDtypeStruct((t, LANES), I32), jax.ShapeDtypeStruct((t, LANES), F32),
                   jax.ShapeDtypeStruct((t, LANES), I32), jax.ShapeDtypeStruct((1, LANES), I32)),
        grid=(t // tm,),
        in_specs=[row_spec(A_WIDTH), row_spec(B_WIDTH), row_spec(d), mod_spec, full(wo),
                  full(n2_row), mod_spec, mod_spec, full(rw), full(rb), full(tri)],
        out_specs=(row_spec(d), pl.BlockSpec((tm, ROW_TILES, LANES), lambda i: (i, 0, 0)),
                   row_spec(LANES), row_spec(LANES), row_spec(LANES),
                   pl.BlockSpec((1, LANES), lambda i: (0, 0))),
        scratch_shapes=[pltpu.VMEM((1, LANES), F32)],
        compiler_params=pltpu.CompilerParams(dimension_semantics=("arbitrary",),
                                             vmem_limit_bytes=VMEM_LIMIT),
        name="outproj",
    )(oa, ob, x2, g1, wo, n2_row, sc2, sh2, rw, rb, tri)


DISPATCH_TOKENS = 256
WAIT_UNROLL = 16


def _drain(make_copy, n):
    def body(g, carry):
        for _ in range(WAIT_UNROLL):
            make_copy().wait()
        return carry

    lax.fori_loop(0, n // WAIT_UNROLL, body, 0)


def _dispatch_kernel(rows_ref, pad_end_ref, h2_ref, out_hbm, zero_ref, sem):
    n = DISPATCH_TOKENS * TOP_K_EXPERTS

    @pl.when(pl.program_id(0) == 0)
    def _():
        zero_ref[...] = jnp.zeros(zero_ref.shape, F32)

        def last_block(e):
            return out_hbm.at[pl.ds(pl.multiple_of(pad_end_ref[e + 1] - EXPERT_BLOCK, EXPERT_BLOCK),
                                    EXPERT_BLOCK)]

        def has_rows(e):
            return pad_end_ref[e + 1] > pad_end_ref[e]

        n_blocks = out_hbm.shape[0] // EXPERT_BLOCK
        min_blocks = n_blocks - N_EXPERTS
        tail = [(b, out_hbm.at[pl.ds(b * EXPERT_BLOCK, EXPERT_BLOCK)]) for b in range(min_blocks, n_blocks)]

        def unused(b):
            return b * EXPERT_BLOCK >= pad_end_ref[N_EXPERTS]

        for e in range(N_EXPERTS):
            @pl.when(has_rows(e))
            def _():
                pltpu.make_async_copy(zero_ref, last_block(e), sem).start()
        for b, dst in tail:
            @pl.when(unused(b))
            def _():
                pltpu.make_async_copy(zero_ref, dst, sem).start()
        for e in range(N_EXPERTS):
            @pl.when(has_rows(e))
            def _():
                pltpu.make_async_copy(zero_ref, last_block(e), sem).wait()
        for b, dst in tail:
            @pl.when(unused(b))
            def _():
                pltpu.make_async_copy(zero_ref, dst, sem).wait()

    def issue(t, carry):
        for k in range(TOP_K_EXPERTS):
            pltpu.make_async_copy(h2_ref.at[t], out_hbm.at[rows_ref[t * TOP_K_EXPERTS + k]],
                                  sem).start(priority=k % 2)
        return carry

    lax.fori_loop(0, DISPATCH_TOKENS, issue, 0)
    _drain(lambda: pltpu.make_async_copy(h2_ref.at[0], out_hbm.at[0], sem), n)


def _dispatch(rows_flat, pad_end0, h2, n_rows):
    t = h2.shape[0]
    n = DISPATCH_TOKENS * TOP_K_EXPERTS
    return pl.pallas_call(
        _dispatch_kernel,
        out_shape=jax.ShapeDtypeStruct((n_rows, ROW_TILES, LANES), F32),
        grid=(t // DISPATCH_TOKENS,),
        in_specs=[pl.BlockSpec((n,), lambda i: (i,), memory_space=pltpu.SMEM),
                  pl.BlockSpec(memory_space=pltpu.SMEM),
                  pl.BlockSpec((DISPATCH_TOKENS, ROW_TILES, LANES), lambda i: (i, 0, 0))],
        out_specs=pl.BlockSpec(memory_space=pl.ANY),
        scratch_shapes=[pltpu.VMEM((EXPERT_BLOCK, ROW_TILES, LANES), F32),
                        pltpu.SemaphoreType.DMA(())],
        compiler_params=pltpu.CompilerParams(dimension_semantics=("arbitrary",)),
        name="dispatch",
    )(rows_flat, pad_end0, h2)


def _experts_kernel(be_ref, nu_ref, run_ref, nxt_ref, x_ref, w1_hbm, b1_ref, w2_hbm, b2_ref, y_ref,
                    w1f_ref, w2f_ref, w1b_ref, w2b_ref, sems):
    i = pl.program_id(0)
    e = be_ref[i]
    prev = be_ref[jnp.maximum(i - 1, 0)]
    changed = jnp.logical_or(i == 0, e != prev)
    used = i < nu_ref[0]
    slot = run_ref[i] % 2

    def weight_copies(expert, s):
        return (pltpu.make_async_copy(w1_hbm.at[expert], w1f_ref.at[s], sems.at[0, s]),
                pltpu.make_async_copy(w2_hbm.at[expert], w2f_ref.at[s], sems.at[1, s]))

    @pl.when(i == 0)
    def _():
        for cp in weight_copies(e, slot):
            cp.start()

    @pl.when(jnp.logical_and(changed, used))
    def _():
        for cp in weight_copies(e, slot):
            cp.wait()

        @pl.when(nxt_ref[i] >= 0)
        def _():
            for cp in weight_copies(nxt_ref[i], 1 - slot):
                cp.start()

        w1b_ref[...] = w1f_ref[slot].astype(BF16)
        w2b_ref[...] = w2f_ref[slot].astype(BF16)

    @pl.when(used)
    def _():
        f = w2b_ref.shape[0]
        hg = _dot(_load_token_tiles(x_ref).astype(BF16), w1b_ref[...]) + b1_ref[0]
        glu = jnp.minimum(hg[:, :f], SWIGLU_LIMIT)
        lin = jnp.clip(hg[:, f:], -SWIGLU_LIMIT, SWIGLU_LIMIT)
        act = glu * _sigmoid(SWIGLU_ALPHA * glu) * (lin + 1.0)
        _store_token_tiles(y_ref, _dot(act.astype(BF16), w2b_ref[...]) + b2_ref[0])

    @pl.when(jnp.logical_not(used))
    def _():
        y_ref[...] = jnp.zeros(y_ref.shape, F32)


def _experts(block_expert, n_used, x_rows, w1, b1, w2, b2):
    n_rows = x_rows.shape[0]
    e, d, f2 = w1.shape
    f = w2.shape[1]
    nb = n_rows // EXPERT_BLOCK
    used = jnp.arange(nb, dtype=I32) < n_used[0]
    starts = jnp.logical_and(used, jnp.concatenate([jnp.ones((1,), bool), block_expert[1:] != block_expert[:-1]]))
    run_id = (jnp.cumsum(starts.astype(I32)) - 1).astype(I32)
    later = jnp.logical_and(block_expert[None, :] > block_expert[:, None], used[None, :])
    nxt = jnp.min(jnp.where(later, block_expert[None, :], N_EXPERTS), axis=1)
    nxt = jnp.where(nxt < N_EXPERTS, nxt, -1).astype(I32)
    tok_spec = pl.BlockSpec((EXPERT_BLOCK, ROW_TILES, LANES), lambda i, *_: (i, 0, 0))
    bias_spec = lambda w: pl.BlockSpec((1, 1, w), lambda i, be, *_: (be[i], 0, 0))
    return pl.pallas_call(
        _experts_kernel,
        out_shape=jax.ShapeDtypeStruct((n_rows, ROW_TILES, LANES), F32),
        grid_spec=pltpu.PrefetchScalarGridSpec(
            num_scalar_prefetch=4,
            grid=(nb,),
            in_specs=[tok_spec, pl.BlockSpec(memory_space=pl.ANY), bias_spec(f2),
                      pl.BlockSpec(memory_space=pl.ANY), bias_spec(d)],
            out_specs=tok_spec,
            scratch_shapes=[pltpu.VMEM((2, d, f2), F32), pltpu.VMEM((2, f, d), F32),
                            pltpu.VMEM((d, f2), BF16), pltpu.VMEM((f, d), BF16),
                            pltpu.SemaphoreType.DMA((2, 2))]),
        compiler_params=pltpu.CompilerParams(dimension_semantics=("arbitrary",),
                                             vmem_limit_bytes=VMEM_LIMIT),
        name="experts",
    )(block_expert, n_used, run_id, nxt, x_rows, w1, b1.reshape(e, 1, f2), w2, b2.reshape(e, 1, d))


COMBINE_TOKENS = 256


def _combine_kernel(rows_ref, next_rows_ref, y_hbm, gate_ref, x1_ref, g2_ref, fg_ref, o_ref, buf_ref, sems):
    n = COMBINE_TOKENS * TOP_K_EXPERTS
    i = pl.program_id(0)
    slot = i % 2

    def gather(rows, s):
        def issue(t, carry):
            for k in range(TOP_K_EXPERTS):
                pltpu.make_async_copy(y_hbm.at[rows[t * TOP_K_EXPERTS + k]], buf_ref.at[s, k, t],
                                      sems.at[s]).start(priority=k % 2)
            return carry

        lax.fori_loop(0, COMBINE_TOKENS, issue, 0)

    @pl.when(i == 0)
    def _():
        gather(rows_ref, 0)

    @pl.when(i + 1 < pl.num_programs(0))
    def _():
        gather(next_rows_ref, 1 - slot)

    _drain(lambda: pltpu.make_async_copy(y_hbm.at[0], buf_ref.at[slot, 0, 0], sems.at[slot]), n)

    gates = gate_ref[...]
    moe = gates[:, 0:1] * _load_token_tiles(buf_ref.at[slot, 0])
    for k in range(1, TOP_K_EXPERTS):
        moe = moe + gates[:, k:k + 1] * _load_token_tiles(buf_ref.at[slot, k])
    x2 = x1_ref[...] + g2_ref[0] * moe
    ms = jnp.mean(x2 * x2, axis=-1, keepdims=True)
    o_ref[...] = x2 * lax.rsqrt(ms + RMS_EPS) * fg_ref[...]


def _combine(rows_flat, y_rows, gates, x1, g2, fg_row, seq):
    t, d = x1.shape
    tm = COMBINE_TOKENS
    per_b = seq // tm
    n = tm * TOP_K_EXPERTS
    steps = t // tm
    return pl.pallas_call(
        _combine_kernel,
        out_shape=jax.ShapeDtypeStruct((t, d), F32),
        grid=(steps,),
        in_specs=[pl.BlockSpec((n,), lambda i: (i,), memory_space=pltpu.SMEM),
                  pl.BlockSpec((n,), lambda i: (jnp.minimum(i + 1, steps - 1),), memory_space=pltpu.SMEM),
                  pl.BlockSpec(memory_space=pl.ANY),
                  pl.BlockSpec((tm, LANES), lambda i: (i, 0)),
                  pl.BlockSpec((tm, d), lambda i: (i, 0)),
                  pl.BlockSpec((1, 1, d), lambda i: (i // per_b, 0, 0)),
                  pl.BlockSpec((1, d), lambda i: (0, 0))],
        out_specs=pl.BlockSpec((tm, d), lambda i: (i, 0)),
        scratch_shapes=[pltpu.VMEM((2, TOP_K_EXPERTS, tm, ROW_TILES, LANES), F32),
                        pltpu.SemaphoreType.DMA((2,))],
        compiler_params=pltpu.CompilerParams(dimension_semantics=("arbitrary",),
                                             vmem_limit_bytes=VMEM_LIMIT),
        name="combine",
    )(rows_flat, rows_flat, y_rows, gates, x1, g2, fg_row)


def kernel(x, c, positions, ada_w, ada_b, norm1_g, w_in, hg_norm_g, lb_logits, w_out, norm2_g,
           router_w, router_b, moe_w1, moe_b1, moe_w2, moe_b2, final_g):
    batch, seq, d = x.shape
    t = batch * seq
    layer = 0
    x2 = x.reshape(t, d)

    c_pad = jnp.concatenate([c, jnp.zeros((SUBLANES - batch, d), F32)], axis=0)
    mod = _adaln(c_pad, ada_w[layer], ada_b[layer][None, :])[:batch]
    shift1, scale1, gate1, shift2, scale2, gate2 = jnp.split(mod, 6, axis=-1)
    row3 = lambda m: m[:, None, :]

    inv_freq = ROPE_THETA ** (-(jnp.arange(0, ROT_DIM, 2, dtype=F32) / ROT_DIM))
    cos_t, sin_t = _trig(positions.reshape(1, t).astype(F32), inv_freq[:, None])

    wl = w_in[layer]
    sp = np.cumsum((A_WIDTH, A_WIDTH, A_WIDTH, IDX_HEADS * IDX_DIM, IDX_DIM, IDX_HEADS,
                    B_HEADS * B_KEY_DIM, B_HEADS * B_KEY_DIM, B_WIDTH))
    w_qa, w_ka, w_va, w_qi, w_ki, w_wi, w_qb, w_fb, w_ib, w_gb = jnp.split(wl, [int(v) for v in sp], axis=1)
    w_t = jnp.concatenate([w_qa, w_va, w_qi, w_wi, jnp.zeros((d, SUBLANES - IDX_HEADS), F32)], axis=1)
    w_t = w_t.astype(BF16).T
    w_r = jnp.concatenate([w_ka, w_ki, jnp.zeros((d, LANES - IDX_DIM), F32), w_qb, w_fb, w_ib, w_gb],
                          axis=1).astype(BF16)

    g1n = norm1_g[layer]
    ka, ki, qb, fb, ib, gb, qa_t, v_blk, qi_t, wi_t = _inproj(
        x2, g1n[None, :], row3(scale1), row3(shift1), w_r, w_t, cos_t, sin_t, seq)

    out_a = _dsa(qi_t, wi_t, ki, qa_t, ka, v_blk, batch, seq)

    lower = jnp.cumsum(jax.nn.softmax(lb_logits.astype(F32), axis=0), axis=0)[layer]
    out_b = _hgrn(qb, fb, ib, gb, lower[None, :], hg_norm_g[layer][None, :], batch, seq)

    rw = jnp.concatenate([router_w[layer], jnp.zeros((d, LANES - N_EXPERTS), F32)], axis=1)
    rw_hi = rw.astype(BF16)
    rw_lo = (rw - rw_hi.astype(F32)).astype(BF16)
    rw = jnp.concatenate([rw_hi, rw_lo, rw_hi], axis=0)
    rb = jnp.concatenate([router_b[layer], jnp.full((LANES - N_EXPERTS,), NEG_BIG, F32)])[None, :]
    x1, h2, idx, gates, rank, counts = _outproj(
        out_a, out_b, x2, row3(gate1), w_out[layer].astype(BF16), norm2_g[layer][None, :],
        row3(scale2), row3(shift2), rw, rb, seq)

    counts = counts[0, :N_EXPERTS]
    padded = (counts + EXPERT_BLOCK - 1) // EXPERT_BLOCK * EXPERT_BLOCK
    pad_end = jnp.cumsum(padded)
    pad_start = pad_end - padded
    n_assign = t * TOP_K_EXPERTS
    n_blocks = -(-n_assign // EXPERT_BLOCK) + N_EXPERTS
    n_rows = n_blocks * EXPERT_BLOCK
    block_start = jnp.arange(n_blocks, dtype=I32) * EXPERT_BLOCK
    block_expert = jnp.minimum(jnp.sum(pad_end[None, :] <= block_start[:, None], axis=1),
                               N_EXPERTS - 1).astype(I32)
    n_used = (pad_end[-1:] // EXPERT_BLOCK).astype(I32)
    idx4 = idx[:, :TOP_K_EXPERTS]
    start_of = jnp.sum(jnp.where(idx4[:, :, None] == jnp.arange(N_EXPERTS, dtype=I32), pad_start.astype(I32), 0),
                       axis=-1)
    rows_flat = (start_of + rank[:, :TOP_K_EXPERTS]).astype(I32).reshape(-1)

    pad_end0 = jnp.concatenate([jnp.zeros((1,), I32), pad_end.astype(I32)])
    x_rows = _dispatch(rows_flat, pad_end0, h2, n_rows)
    y_rows = _experts(block_expert, n_used, x_rows, moe_w1[layer], moe_b1[layer],
                      moe_w2[layer], moe_b2[layer])
    out = _combine(rows_flat, y_rows, gates, x1, row3(gate2), final_g[None, :], seq)
    return out.reshape(batch, seq, d)
```

```python
import functools

import numpy as np
import jax
import jax.numpy as jnp
from jax import lax
from jax.experimental import pallas as pl
from jax.experimental.pallas import tpu as pltpu

F32 = jnp.float32
BF16 = jnp.bfloat16
I32 = jnp.int32
I16 = jnp.int16
HALF_OFFSET = 2 ** 15
HIGHEST = lax.Precision.HIGHEST

D_MODEL = 1024
CHUNK = 64
A_HEADS = 8
A_HEAD_DIM = 64
A_WIDTH = A_HEADS * A_HEAD_DIM
IDX_HEADS = 4
IDX_DIM = 64
TOPK_MAX = 256
B_HEADS = 4
B_KEY_DIM = 64
B_VAL_DIM = 128
B_WIDTH = B_HEADS * B_VAL_DIM
ROPE_THETA = 500000.0
ROT_DIM = A_HEAD_DIM // 4
ROT_HALF = ROT_DIM // 2
N_EXPERTS = 32
TOP_K_EXPERTS = 4
SWIGLU_LIMIT = 7.0
SWIGLU_ALPHA = 1.702
EXPERT_BLOCK = 256
RMS_EPS = 1e-6

LANES = 128
SUBLANES = 8
VMEM_LIMIT = 56 * 1024 * 1024

NEG_BIG = -1e30
INT_MIN = -(2 ** 31)
INT_MAX = 2 ** 31 - 1
LOG2E = 1.4426950408889634
IDX_K = 4 * IDX_DIM

DSA_TQ = 256
DSA_KB = 256
HG_STEP_CHUNKS = 8
HG_GROUP = 4
HG_LEVELS = (32, 16, 8, 4, 2, 1)


def _dot(a, b, precision=None):
    return jnp.dot(a, b, preferred_element_type=F32, precision=precision)


def _dot_nt(a, b, precision=None):
    return lax.dot_general(a, b, (((1,), (1,)), ((), ())),
                           preferred_element_type=F32, precision=precision)


def _sigmoid(x):
    return 1.0 / (1.0 + jnp.exp(-x))


def _split_bf16(x):
    hi = x.astype(BF16).astype(F32)
    lo = (x - hi).astype(BF16).astype(F32)
    return hi, lo


ROW_TILES = D_MODEL // LANES


def _load_token_tiles(ref3):
    tiles = jnp.swapaxes(ref3[...], 0, 1)
    return jnp.concatenate([tiles[s] for s in range(ROW_TILES)], axis=1)


def _store_token_tiles(ref3, val):
    for s in range(ROW_TILES):
        ref3[:, s, :] = val[:, s * LANES:(s + 1) * LANES]


def _adaln_kernel(c_ref, w_ref, b_ref, o_ref):
    c = c_ref[...]
    o_ref[...] = _dot(c * _sigmoid(c), w_ref[...], HIGHEST) + b_ref[...]


def _adaln(c_pad, ada_w, ada_b):
    d = c_pad.shape[1]
    n = ada_w.shape[1]
    bn = 1024
    return pl.pallas_call(
        _adaln_kernel,
        out_shape=jax.ShapeDtypeStruct((c_pad.shape[0], n), F32),
        grid=(n // bn,),
        in_specs=[pl.BlockSpec((c_pad.shape[0], d), lambda j: (0, 0)),
                  pl.BlockSpec((d, bn), lambda j: (0, j)),
                  pl.BlockSpec((1, bn), lambda j: (0, j))],
        out_specs=pl.BlockSpec((c_pad.shape[0], bn), lambda j: (0, j)),
        name="adaln",
    )(c_pad, ada_w, ada_b)


def _trig_kernel(pos_ref, freq_ref, cos_ref, sin_ref):
    ang = pos_ref[...] * freq_ref[...]
    cos_ref[...] = jnp.cos(ang)
    sin_ref[...] = jnp.sin(ang)


def _trig(pos_row, freq_col):
    t = pos_row.shape[1]
    bt = 2048
    return pl.pallas_call(
        _trig_kernel,
        out_shape=(jax.ShapeDtypeStruct((ROT_HALF, t), F32),) * 2,
        grid=(t // bt,),
        in_specs=[pl.BlockSpec((1, bt), lambda i: (0, i)),
                  pl.BlockSpec((ROT_HALF, 1), lambda i: (0, 0))],
        out_specs=(pl.BlockSpec((ROT_HALF, bt), lambda i: (0, i)),) * 2,
        name="trig",
    )(pos_row, freq_col)


def _rope_rows(p, c, s):
    x1 = p[0:ROT_HALF]
    x2 = p[ROT_HALF:ROT_DIM]
    return jnp.concatenate([x1 * c - x2 * s, x2 * c + x1 * s, p[ROT_DIM:]], axis=0)


def _inproj_kernel(x_ref, g_ref, sc_ref, sh_ref, wr_ref, w_ref, cos_ref, sin_ref, cosr_ref, sinr_ref,
                   ec_ref, es_ref,
                   ka_ref, ki_ref, qb_ref, fb_ref, ib_ref, gb_ref, qa_ref, v_ref, qi_ref, wt_ref):
    x = x_ref[...]
    ms = jnp.mean(x * x, axis=-1, keepdims=True)
    h = x * lax.rsqrt(ms + RMS_EPS) * g_ref[...]
    h = h * (1.0 + sc_ref[0]) + sh_ref[0]
    hb = h.astype(BF16)
    _inproj_rows(hb, wr_ref, cosr_ref, sinr_ref, ec_ref, es_ref,
                 ka_ref, ki_ref, qb_ref, fb_ref, ib_ref, gb_ref)
    c = cos_ref[...]
    s = sin_ref[...]
    pq = _dot_nt(w_ref[0:A_WIDTH, :], hb)
    for hh in range(A_HEADS):
        r = _rope_rows(pq[hh * 64:(hh + 1) * 64], c, s) * (A_HEAD_DIM ** -0.5 * LOG2E)
        qa_ref[hh * 64:(hh + 1) * 64, :] = r.astype(BF16)
    pv = _dot_nt(w_ref[A_WIDTH:2 * A_WIDTH, :], hb)
    for cb in range(v_ref.shape[0]):
        v_ref[cb] = pv[:, cb * DSA_KB:(cb + 1) * DSA_KB].astype(BF16)
    pi = _dot_nt(w_ref[2 * A_WIDTH:2 * A_WIDTH + 256, :], hb)
    for hh in range(IDX_HEADS):
        q = _rope_rows(pi[hh * 64:(hh + 1) * 64], c, s) * (IDX_DIM ** -0.5)
        q_hi, q_lo = _split_bf16(q)
        qi_ref[hh * IDX_K:(hh + 1) * IDX_K, :] = jnp.concatenate(
            [q_hi, q_hi, q_lo, jnp.zeros_like(q_hi)], axis=0).astype(BF16)
    pw = _dot_nt(w_ref[2 * A_WIDTH + 256:2 * A_WIDTH + 264, :], hb)
    wt_ref[...] = pw * (IDX_HEADS ** -0.5)


def _rope_lane_tables():
    lane = np.arange(LANES) % A_HEAD_DIM
    i = np.arange(ROT_HALF)[:, None]
    first, second = lane[None, :] == i, lane[None, :] == i + ROT_HALF
    ec = (first | second).astype(np.float32)
    es = second.astype(np.float32) - first.astype(np.float32)
    ones = (lane >= ROT_DIM).astype(np.float32)[None, :]
    return ec, es, ones


def _inproj_rows(hb, w_ref, cosr_ref, sinr_ref, ec_ref, es_ref,
                 ka_ref, ki_ref, qb_ref, fb_ref, ib_ref, gb_ref):
    c = _dot(cosr_ref[...], ec_ref[0:ROT_HALF, :], HIGHEST) + ec_ref[ROT_HALF:ROT_HALF + 1, :]
    s = _dot(sinr_ref[...], es_ref[...], HIGHEST)
    lane = lax.broadcasted_iota(I32, c.shape, 1)
    first = (lane % A_HEAD_DIM) < ROT_HALF

    def rope(p):
        partner = jnp.where(first, pltpu.roll(p, LANES - ROT_HALF, 1), pltpu.roll(p, ROT_HALF, 1))
        return p * c + partner * s

    for j in range(A_WIDTH // LANES):
        p = _dot(hb, w_ref[:, j * LANES:(j + 1) * LANES])
        ka_ref[:, j * LANES:(j + 1) * LANES] = rope(p).astype(BF16)
    o = A_WIDTH
    k_hi, k_lo = _split_bf16(rope(_dot(hb, w_ref[:, o:o + LANES])))
    ki_ref[:, 0:LANES] = (k_hi + pltpu.roll(k_lo, IDX_DIM, 1)).astype(BF16)
    ki_ref[:, LANES:2 * LANES] = k_hi.astype(BF16)
    o += LANES
    qb_ref[...] = _dot(hb, w_ref[:, o:o + 256])
    o += 256
    fb_ref[...] = _dot(hb, w_ref[:, o:o + 256])
    o += 256
    ib_ref[...] = _dot(hb, w_ref[:, o:o + 512])
    o += 512
    gb_ref[...] = _dot(hb, w_ref[:, o:o + 512])


def _inproj(x2, g_row, sc_row, sh_row, w_r, w_t, cos_t, sin_t, seq):
    t, d = x2.shape
    tm = 512
    per_b = seq // tm
    ec, es, ones = _rope_lane_tables()
    ec = jnp.asarray(np.concatenate([ec, ones, np.zeros((SUBLANES - 1, LANES), np.float32)]))
    es = jnp.asarray(es)
    widths = (A_WIDTH, IDX_K, 256, 256, 512, 512)
    dts = (BF16, BF16, F32, F32, F32, F32)
    row_outs = tuple(jax.ShapeDtypeStruct((t, w), dt) for w, dt in zip(widths, dts))
    col_outs = (jax.ShapeDtypeStruct((A_WIDTH, t), BF16),
                jax.ShapeDtypeStruct((t // DSA_KB, A_WIDTH, DSA_KB), BF16),
                jax.ShapeDtypeStruct((IDX_HEADS * IDX_K, t), BF16),
                jax.ShapeDtypeStruct((SUBLANES, t), F32))
    full = lambda a: pl.BlockSpec(a.shape, lambda i: (0,) * a.ndim)
    mod_spec = pl.BlockSpec((1, 1, d), lambda i: (i // per_b, 0, 0))
    return pl.pallas_call(
        _inproj_kernel,
        out_shape=row_outs + col_outs,
        grid=(t // tm,),
        in_specs=[pl.BlockSpec((tm, d), lambda i: (i, 0)), full(g_row), mod_spec, mod_spec,
                  full(w_r), full(w_t),
                  pl.BlockSpec((ROT_HALF, tm), lambda i: (0, i)),
                  pl.BlockSpec((ROT_HALF, tm), lambda i: (0, i)),
                  pl.BlockSpec((tm, ROT_HALF), lambda i: (i, 0)),
                  pl.BlockSpec((tm, ROT_HALF), lambda i: (i, 0)),
                  full(ec), full(es)],
        out_specs=tuple(pl.BlockSpec((tm, w), lambda i: (i, 0)) for w in widths)
        + (pl.BlockSpec((A_WIDTH, tm), lambda i: (0, i)),
           pl.BlockSpec((tm // DSA_KB, A_WIDTH, DSA_KB), lambda i: (i, 0, 0)),
           pl.BlockSpec((IDX_HEADS * IDX_K, tm), lambda i: (0, i)),
           pl.BlockSpec((SUBLANES, tm), lambda i: (0, i))),
        compiler_params=pltpu.CompilerParams(vmem_limit_bytes=VMEM_LIMIT),
        name="inproj",
    )(x2, g_row, sc_row, sh_row, w_r, w_t, cos_t, sin_t, cos_t.T, sin_t.T, ec, es)


def _dsa_kernel(qi_ref, wt_ref, ki_ref, qa_ref, ka_ref, v_ref, tri_ref, o_ref,
                keys_ref, m_ref, l_ref, acc_ref, s_ref, p_ref, kh_ref, kl_ref):
    j = pl.program_id(1)
    nkb = j + 1
    tq = DSA_TQ
    kb_rows = DSA_KB
    row = lax.broadcasted_iota(I32, (kb_rows, tq), 0)
    col = lax.broadcasted_iota(I32, (kb_rows, tq), 1)
    q_chunk = (j * tq + col) // CHUNK

    def score_keys(kb):
        r0 = pl.multiple_of(kb * kb_rows, kb_rows)
        ki = ki_ref[pl.ds(r0, kb_rows), :]
        sc = jnp.zeros((kb_rows, tq), F32)
        for h in range(IDX_HEADS):
            lg = _dot(ki, qi_ref[h * IDX_K:(h + 1) * IDX_K, :])
            sc = sc + wt_ref[h:h + 1, :] * jnp.maximum(lg, 0.0)
        sc = jnp.where(sc == 0.0, 0.0, sc)
        bits = pltpu.bitcast(sc, I32)
        return jnp.where(bits < 0, bits ^ 0x7FFFFFFF, bits)

    def store_keys(kb, key):
        keys_ref[kb] = key
        kh_ref[kb] = (key >> 16).astype(I16)
        kl_ref[kb] = ((key & 0xFFFF) - HALF_OFFSET).astype(I16)

    def score_block(kb, carry):
        store_keys(kb, score_keys(kb))
        return carry

    lax.fori_loop(0, j, score_block, 0)
    adm = ((j * kb_rows + row) // CHUNK) <= q_chunk
    store_keys(j, jnp.where(adm, score_keys(j), INT_MIN))

    def count16(ref, pred_fn):
        rows16 = 2 * SUBLANES

        def body(kb, acc):
            hit = jnp.where(pred_fn(ref[kb]), jnp.int16(1), jnp.int16(0))
            for r in range(kb_rows // rows16):
                acc = acc + hit[r * rows16:(r + 1) * rows16]
            return acc
        acc = lax.fori_loop(0, nkb, body, jnp.zeros((rows16, tq), I16))
        return jnp.sum(acc.astype(I32), axis=0, keepdims=True)

    def search16(ref, need):
        def step(i, u):
            cand = u | lax.shift_left(jnp.int32(1), 15 - i)
            cand16 = (cand - HALF_OFFSET).astype(I16)
            cnt = count16(ref, lambda k: k >= cand16)
            return jnp.where(cnt >= need, cand, u)
        return lax.fori_loop(0, 16, step, jnp.zeros((1, tq), I32))

    u_hi = search16(kh_ref, TOPK_MAX)
    t_hi = (u_hi - HALF_OFFSET).astype(I16)
    above = count16(kh_ref, lambda k: k > t_hi)

    def mask_low(kb, carry):
        kl_ref[kb] = jnp.where(kh_ref[kb] == t_hi, kl_ref[kb], jnp.int16(-HALF_OFFSET))
        return carry

    lax.fori_loop(0, nkb, mask_low, 0)
    u_lo = search16(kl_ref, TOPK_MAX - above)
    t_lo = (u_lo - HALF_OFFSET).astype(I16)
    thr = (u_hi - HALF_OFFSET) * (2 * HALF_OFFSET) + u_lo
    n_gt = above + count16(kl_ref, lambda k: k > t_lo)
    need = (TOPK_MAX - n_gt).astype(F32)

    def strike_ties(kb, run):
        key = keys_ref[kb]
        tie = key == thr
        tie_f = jnp.where(tie, 1.0, 0.0)
        rank = _dot(tri_ref[...], tie_f.astype(BF16)) + run
        keys_ref[kb] = jnp.where(jnp.logical_and(tie, rank > need), INT_MIN, key)
        return run + jnp.sum(tie_f, axis=0, keepdims=True)

    lax.fori_loop(0, nkb, strike_ties, jnp.zeros((1, tq), F32))
    thr_sel = jnp.maximum(thr, INT_MIN + 1)

    m_ref[...] = jnp.full(m_ref.shape, NEG_BIG, F32)
    l_ref[...] = jnp.zeros(l_ref.shape, F32)
    acc_ref[...] = jnp.zeros(acc_ref.shape, F32)

    def attn_block(kb, carry):
        r0 = pl.multiple_of(kb * kb_rows, kb_rows)
        bias = jnp.where(keys_ref[kb] >= thr_sel, 0.0, NEG_BIG)
        heads = [slice(h * A_HEAD_DIM, (h + 1) * A_HEAD_DIM) for h in range(A_HEADS)]
        for h, hs in enumerate(heads):
            s_ref[h] = _dot(ka_ref[pl.ds(r0, kb_rows), hs], qa_ref[hs, :])
        alphas = []
        for h, hs in enumerate(heads):
            s = s_ref[h] + bias
            m_old = m_ref[h][0:1, :]
            m_new = jnp.maximum(m_old, jnp.max(s, axis=0, keepdims=True))
            alpha = jnp.exp2(m_old - m_new)
            p = jnp.exp2(s - m_new)
            l_new = alpha * l_ref[h][0:1, :] + jnp.sum(p, axis=0, keepdims=True)
            p_ref[h] = p.astype(BF16)
            m_ref[h] = jnp.broadcast_to(m_new, (SUBLANES, tq))
            l_ref[h] = jnp.broadcast_to(l_new, (SUBLANES, tq))
            alphas.append(alpha)
        for h, hs in enumerate(heads):
            acc_ref[hs, :] = alphas[h] * acc_ref[hs, :] + _dot(v_ref[kb, hs, :], p_ref[h])
        return carry

    lax.fori_loop(0, nkb, attn_block, 0)

    for h in range(A_HEADS):
        hs = slice(h * A_HEAD_DIM, (h + 1) * A_HEAD_DIM)
        acc_ref[hs, :] = acc_ref[hs, :] / l_ref[h][0:1, :]
    o_ref[...] = acc_ref[...].T


def _dsa(qi_t, w_t, ki, qa_t, ka, v_blk, batch, seq):
    t = ka.shape[0]
    nqb = seq // DSA_TQ
    nkb = seq // DSA_KB
    return pl.pallas_call(
        _dsa_kernel,
        out_shape=jax.ShapeDtypeStruct((t, A_WIDTH), F32),
        grid=(batch, nqb),
        in_specs=[pl.BlockSpec((IDX_HEADS * IDX_K, DSA_TQ), lambda b, j: (0, b * nqb + j)),
                  pl.BlockSpec((SUBLANES, DSA_TQ), lambda b, j: (0, b * nqb + j)),
                  pl.BlockSpec((seq, IDX_K), lambda b, j: (b, 0)),
                  pl.BlockSpec((A_WIDTH, DSA_TQ), lambda b, j: (0, b * nqb + j)),
                  pl.BlockSpec((seq, A_WIDTH), lambda b, j: (b, 0)),
                  pl.BlockSpec((nkb, A_WIDTH, DSA_KB), lambda b, j: (b, 0, 0)),
                  pl.BlockSpec((DSA_KB, DSA_KB), lambda b, j: (0, 0))],
        out_specs=pl.BlockSpec((DSA_TQ, A_WIDTH), lambda b, j: (b * nqb + j, 0)),
        scratch_shapes=[pltpu.VMEM((nkb, DSA_KB, DSA_TQ), I32),
                        pltpu.VMEM((A_HEADS, SUBLANES, DSA_TQ), F32),
                        pltpu.VMEM((A_HEADS, SUBLANES, DSA_TQ), F32),
                        pltpu.VMEM((A_WIDTH, DSA_TQ), F32),
                        pltpu.VMEM((A_HEADS, DSA_KB, DSA_TQ), F32),
                        pltpu.VMEM((A_HEADS, DSA_KB, DSA_TQ), BF16),
                        pltpu.VMEM((nkb, DSA_KB, DSA_TQ), I16),
                        pltpu.VMEM((nkb, DSA_KB, DSA_TQ), I16)],
        compiler_params=pltpu.CompilerParams(vmem_limit_bytes=VMEM_LIMIT),
        name="dsa",
    )(qi_t, w_t, ki, qa_t, ka, v_blk, jnp.asarray(np.tril(np.ones((DSA_KB, DSA_KB), np.float32)), BF16))


def _hgrn_constants():
    c = CHUNK
    t = np.arange(c)[:, None]
    u = np.arange(c)[None, :]
    blocks = [(u <= t), (u > t)]
    masks = []
    for half in HG_LEVELS:
        mid = (t // (2 * half)) * (2 * half) + half - 1
        right = ((t // half) % 2) == 1
        blocks.append(right & (u > mid) & (u <= t))
        blocks.append((~right) & (u > t) & (u <= mid))
        tt, ss = t, u
        masks.append(((tt // (2 * half)) == (ss // (2 * half)))
                     & ((((tt // half) % 2) == 1) & (((ss // half) % 2) == 0)))
    masks.append(t == u)
    cm = np.concatenate(blocks, axis=0).astype(np.float32)
    cm3 = np.concatenate([cm, cm, cm], axis=1)
    masks4 = np.tile(np.stack(masks).astype(np.float32), (1, 1, B_HEADS))
    head_of = lambda n, per: np.arange(n) // per
    bd_k = (head_of(B_HEADS * c, c)[:, None] == head_of(B_HEADS * B_KEY_DIM, B_KEY_DIM)[None, :])
    bd_v = (head_of(B_HEADS * c, c)[:, None] == head_of(B_WIDTH, B_VAL_DIM)[None, :])
    bd_s = (head_of(B_WIDTH, B_VAL_DIM)[:, None] == head_of(B_HEADS * B_KEY_DIM, B_KEY_DIM)[None, :])
    return cm3, masks4, bd_k.astype(np.float32), bd_v.astype(np.float32), bd_s.astype(np.float32)


def _hgrn_kernel(qb_ref, fb_ref, ib_ref, gb_ref, lb_ref, hg_ref, cm_ref, mask_ref, bdk_ref, bdv_ref,
                 bds_ref, eye_ref, o_ref, state_ref, lv_ref):
    @pl.when(pl.program_id(1) == 0)
    def _():
        state_ref[...] = jnp.zeros(state_ref.shape, F32)

    lb = lb_ref[...]
    c = CHUNK
    nlev = len(HG_LEVELS)
    seq_refs = (qb_ref, fb_ref, ib_ref, gb_ref, o_ref, state_ref, lv_ref)

    def stack_heads(x, bd_ref):
        return (jnp.concatenate([x] * B_HEADS, axis=0) * bd_ref[...]).astype(BF16)

    def chunk_step(ci, carry):
        for bi in range(qb_ref.shape[0]):
            one_chunk(bi, pl.multiple_of(ci * c, c))
        return carry

    def one_chunk(bi, r0):
        qb_ref, fb_ref, ib_ref, gb_ref, o_ref, state_ref, lv_ref = [r.at[bi] for r in seq_refs]
        f = lb + (1.0 - lb) * _sigmoid(fb_ref[pl.ds(r0, c), :])
        lf = jnp.log(f)
        kin = 1.0 - f
        hi = lf.astype(BF16)
        r1 = lf - hi.astype(F32)
        mid = r1.astype(BF16)
        lo = (r1 - mid.astype(F32)).astype(BF16)
        e = jnp.exp(_dot(cm_ref[...], jnp.concatenate([hi, mid, lo], axis=0)))
        q = qb_ref[pl.ds(r0, c), :]
        q_in = (q * e[0:c]).astype(BF16)
        k_out = (kin * e[c:2 * c]).astype(BF16)
        e_last = e[c - 1:c, :]
        for i in range(nlev + 1):
            if i < nlev:
                q_i = q * e[(2 + 2 * i) * c:(3 + 2 * i) * c]
                k_i = kin * e[(3 + 2 * i) * c:(4 + 2 * i) * c]
            else:
                q_i, k_i = q, kin
            lv_ref[i] = _dot_nt(q_i.astype(BF16), stack_heads(k_i, bdk_ref))
        attn = mask_ref[0] * lv_ref[0]
        for i in range(1, nlev + 1):
            attn = attn + mask_ref[i] * lv_ref[i]
        v = ib_ref[pl.ds(r0, c), :]
        st = state_ref[...]
        o_all = _dot_nt(q_in, st.astype(BF16)) + _dot(attn.astype(BF16), stack_heads(v, bdv_ref))
        v_t = _dot_nt(eye_ref[...], v.astype(BF16)).astype(BF16)
        state_ref[...] = st * e_last + bds_ref[...] * _dot(v_t, k_out)
        for h in range(B_HEADS):
            vs = slice(h * B_VAL_DIM, (h + 1) * B_VAL_DIM)
            o = o_all[:, vs]
            ms = jnp.mean(o * o, axis=-1, keepdims=True)
            y = o * lax.rsqrt(ms + RMS_EPS) * hg_ref[...]
            g = gb_ref[pl.ds(r0, c), vs]
            o_ref[pl.ds(r0, c), vs] = y * (g * _sigmoid(g))

    lax.fori_loop(0, HG_STEP_CHUNKS, chunk_step, 0)


def _hgrn(qb, fb, ib, gb, lb_row, hg_row, batch, seq):
    t = qb.shape[0]
    tm = HG_STEP_CHUNKS * CHUNK
    per_b = seq // tm
    cm3, masks4, bd_k, bd_v, bd_s = _hgrn_constants()
    consts = (jnp.asarray(cm3, BF16), jnp.asarray(masks4, F32), jnp.asarray(bd_k, F32),
              jnp.asarray(bd_v, F32), jnp.asarray(bd_s, F32), jnp.eye(B_WIDTH, dtype=BF16))
    kd = B_HEADS * B_KEY_DIM
    grp = HG_GROUP if batch % HG_GROUP == 0 else 1
    tok_spec = lambda w: pl.BlockSpec((grp, tm, w), lambda b, i: (b, i, 0))
    full = lambda a: pl.BlockSpec(a.shape, lambda b, i: (0,) * a.ndim)
    seqs = [a.reshape(batch, seq, a.shape[1]) for a in (qb, fb, ib, gb)]
    out = pl.pallas_call(
        _hgrn_kernel,
        out_shape=jax.ShapeDtypeStruct((batch, seq, B_WIDTH), F32),
        grid=(batch // grp, per_b),
        in_specs=[tok_spec(kd), tok_spec(kd), tok_spec(B_WIDTH), tok_spec(B_WIDTH),
                  full(lb_row), full(hg_row)] + [full(a) for a in consts],
        out_specs=tok_spec(B_WIDTH),
        scratch_shapes=[pltpu.VMEM((grp, B_WIDTH, kd), F32),
                        pltpu.VMEM((grp, len(HG_LEVELS) + 1, CHUNK, B_HEADS * CHUNK), F32)],
        compiler_params=pltpu.CompilerParams(vmem_limit_bytes=VMEM_LIMIT),
        name="hgrn",
    )(*seqs, lb_row, hg_row, *consts)
    return out.reshape(t, B_WIDTH)


def _outproj_kernel(oa_ref, ob_ref, x_ref, g1_ref, wo_ref, n2_ref, sc_ref, sh_ref,
                    rw_ref, rb_ref, tri_ref,
                    x1_ref, h2_ref, idx_ref, gate_ref, rank_ref, cnt_ref, run_ref):
    @pl.when(pl.program_id(0) == 0)
    def _():
        run_ref[...] = jnp.zeros(run_ref.shape, F32)

    half = A_WIDTH
    mix = _dot(oa_ref[...].astype(BF16), wo_ref[0:half, :]) + _dot(ob_ref[...].astype(BF16), wo_ref[half:, :])
    x1 = x_ref[...] + g1_ref[0] * mix
    x1_ref[...] = x1
    ms = jnp.mean(x1 * x1, axis=-1, keepdims=True)
    h2 = x1 * lax.rsqrt(ms + RMS_EPS) * n2_ref[...]
    h2 = h2 * (1.0 + sc_ref[0]) + sh_ref[0]
    _store_token_tiles(h2_ref, h2)
    h_hi, h_lo = _split_bf16(h2)
    logits = _dot(jnp.concatenate([h_hi, h_hi, h_lo], axis=1).astype(BF16), rw_ref[...]) + rb_ref[...]
    tm = logits.shape[0]
    lane = lax.broadcasted_iota(I32, (tm, LANES), 1)
    work = logits
    vals, idxs = [], []
    for _ in range(TOP_K_EXPERTS):
        m = jnp.max(work, axis=-1, keepdims=True)
        ix = jnp.min(jnp.where(work == m, lane, LANES), axis=-1, keepdims=True)
        vals.append(m)
        idxs.append(ix)
        work = jnp.where(lane == ix, -jnp.inf, work)
    es = [jnp.exp(v - vals[0]) for v in vals]
    tot = es[0] + es[1] + es[2] + es[3]
    onehot = jnp.zeros((tm, LANES), F32)
    idx_out = jnp.zeros((tm, LANES), I32)
    gate_out = jnp.zeros((tm, LANES), F32)
    for k in range(TOP_K_EXPERTS):
        onehot = onehot + jnp.where(lane == idxs[k], 1.0, 0.0)
        idx_out = jnp.where(lane == k, idxs[k], idx_out)
        gate_out = jnp.where(lane == k, es[k] / tot, gate_out)
    idx_ref[...] = idx_out
    gate_ref[...] = gate_out
    before = _dot(tri_ref[...], onehot.astype(BF16)) + run_ref[...]
    rank_out = jnp.zeros((tm, LANES), F32)
    for k in range(TOP_K_EXPERTS):
        rk = jnp.sum(jnp.where(lane == idxs[k], before, 0.0), axis=-1, keepdims=True)
        rank_out = jnp.where(lane == k, rk, rank_out)
    rank_ref[...] = rank_out.astype(I32)
    run = run_ref[...] + jnp.sum(onehot, axis=0, keepdims=True)
    run_ref[...] = run
    cnt_ref[...] = run.astype(I32)


def _outproj(oa, ob, x2, g1, wo, n2_row, sc2, sh2, rw, rb, seq):
    t, d = x2.shape
    tm = 512
    per_b = seq // tm
    tri = jnp.asarray(np.tril(np.ones((tm, tm), np.float32), -1), BF16)
    row_spec = lambda w: pl.BlockSpec((tm, w), lambda i: (i, 0))
    mod_spec = pl.BlockSpec((1, 1, d), lambda i: (i // per_b, 0, 0))
    full = lambda a: pl.BlockSpec(a.shape, lambda i: (0,) * a.ndim)
    return pl.pallas_call(
        _outproj_kernel,
        out_shape=(jax.ShapeDtypeStruct((t, d), F32), jax.ShapeDtypeStruct((t, ROW_TILES, LANES), F32),
                   jax.ShapeDtypeStruct((t, LANES), I32), jax.ShapeDtypeStruct((t, LANES), F32),
                   jax.ShapeDtypeStruct((t, LANES), I32), jax.ShapeDtypeStruct((1, LANES), I32)),
        grid=(t // tm,),
        in_specs=[row_spec(A_WIDTH), row_spec(B_WIDTH), row_spec(d), mod_spec, full(wo),
                  full(n2_row), mod_spec, mod_spec, full(rw), full(rb), full(tri)],
        out_specs=(row_spec(d), pl.BlockSpec((tm, ROW_TILES, LANES), lambda i: (i, 0, 0)),
                   row_spec(LANES), row_spec(LANES), row_spec(LANES),
                   pl.BlockSpec((1, LANES), lambda i: (0, 0))),
        scratch_shapes=[pltpu.VMEM((1, LANES), F32)],
        compiler_params=pltpu.CompilerParams(dimension_semantics=("arbitrary",),
                                             vmem_limit_bytes=VMEM_LIMIT),
        name="outproj",
    )(oa, ob, x2, g1, wo, n2_row, sc2, sh2, rw, rb, tri)


DISPATCH_TOKENS = 256
WAIT_UNROLL = 16


def _drain(make_copy, n):
    def body(g, carry):
        for _ in range(WAIT_UNROLL):
            make_copy().wait()
        return carry

    lax.fori_loop(0, n // WAIT_UNROLL, body, 0)


def _dispatch_kernel(rows_ref, pad_end_ref, h2_ref, out_hbm, zero_ref, sem):
    n = DISPATCH_TOKENS * TOP_K_EXPERTS

    @pl.when(pl.program_id(0) == 0)
    def _():
        zero_ref[...] = jnp.zeros(zero_ref.shape, F32)

        def last_block(e):
            return out_hbm.at[pl.ds(pl.multiple_of(pad_end_ref[e + 1] - EXPERT_BLOCK, EXPERT_BLOCK),
                                    EXPERT_BLOCK)]

        def has_rows(e):
            return pad_end_ref[e + 1] > pad_end_ref[e]

        n_blocks = out_hbm.shape[0] // EXPERT_BLOCK
        min_blocks = n_blocks - N_EXPERTS
        tail = [(b, out_hbm.at[pl.ds(b * EXPERT_BLOCK, EXPERT_BLOCK)]) for b in range(min_blocks, n_blocks)]

        def unused(b):
            return b * EXPERT_BLOCK >= pad_end_ref[N_EXPERTS]

        for e in range(N_EXPERTS):
            @pl.when(has_rows(e))
            def _():
                pltpu.make_async_copy(zero_ref, last_block(e), sem).start()
        for b, dst in tail:
            @pl.when(unused(b))
            def _():
                pltpu.make_async_copy(zero_ref, dst, sem).start()
        for e in range(N_EXPERTS):
            @pl.when(has_rows(e))
            def _():
                pltpu.make_async_copy(zero_ref, last_block(e), sem).wait()
        for b, dst in tail:
            @pl.when(unused(b))
            def _():
                pltpu.make_async_copy(zero_ref, dst, sem).wait()

    def issue(t, carry):
        for k in range(TOP_K_EXPERTS):
            pltpu.make_async_copy(h2_ref.at[t], out_hbm.at[rows_ref[t * TOP_K_EXPERTS + k]],
                                  sem).start(priority=k % 2)
        return carry

    lax.fori_loop(0, DISPATCH_TOKENS, issue, 0)
    _drain(lambda: pltpu.make_async_copy(h2_ref.at[0], out_hbm.at[0], sem), n)


def _dispatch(rows_flat, pad_end0, h2, n_rows):
    t = h2.shape[0]
    n = DISPATCH_TOKENS * TOP_K_EXPERTS
    return pl.pallas_call(
        _dispatch_kernel,
        out_shape=jax.ShapeDtypeStruct((n_rows, ROW_TILES, LANES), F32),
        grid=(t // DISPATCH_TOKENS,),
        in_specs=[pl.BlockSpec((n,), lambda i: (i,), memory_space=pltpu.SMEM),
                  pl.BlockSpec(memory_space=pltpu.SMEM),
                  pl.BlockSpec((DISPATCH_TOKENS, ROW_TILES, LANES), lambda i: (i, 0, 0))],
        out_specs=pl.BlockSpec(memory_space=pl.ANY),
        scratch_shapes=[pltpu.VMEM((EXPERT_BLOCK, ROW_TILES, LANES), F32),
                        pltpu.SemaphoreType.DMA(())],
        compiler_params=pltpu.CompilerParams(dimension_semantics=("arbitrary",)),
        name="dispatch",
    )(rows_flat, pad_end0, h2)


def _experts_kernel(be_ref, nu_ref, run_ref, nxt_ref, x_ref, w1_hbm, b1_ref, w2_hbm, b2_ref, y_ref,
                    w1f_ref, w2f_ref, w1b_ref, w2b_ref, sems):
    i = pl.program_id(0)
    e = be_ref[i]
    prev = be_ref[jnp.maximum(i - 1, 0)]
    changed = jnp.logical_or(i == 0, e != prev)
    used = i < nu_ref[0]
    slot = run_ref[i] % 2

    def weight_copies(expert, s):
        return (pltpu.make_async_copy(w1_hbm.at[expert], w1f_ref.at[s], sems.at[0, s]),
                pltpu.make_async_copy(w2_hbm.at[expert], w2f_ref.at[s], sems.at[1, s]))

    @pl.when(i == 0)
    def _():
        for cp in weight_copies(e, slot):
            cp.start()

    @pl.when(jnp.logical_and(changed, used))
    def _():
        for cp in weight_copies(e, slot):
            cp.wait()

        @pl.when(nxt_ref[i] >= 0)
        def _():
            for cp in weight_copies(nxt_ref[i], 1 - slot):
                cp.start()

        w1b_ref[...] = w1f_ref[slot].astype(BF16)
        w2b_ref[...] = w2f_ref[slot].astype(BF16)

    @pl.when(used)
    def _():
        f = w2b_ref.shape[0]
        hg = _dot(_load_token_tiles(x_ref).astype(BF16), w1b_ref[...]) + b1_ref[0]
        glu = jnp.minimum(hg[:, :f], SWIGLU_LIMIT)
        lin = jnp.clip(hg[:, f:], -SWIGLU_LIMIT, SWIGLU_LIMIT)
        act = glu * _sigmoid(SWIGLU_ALPHA * glu) * (lin + 1.0)
        _store_token_tiles(y_ref, _dot(act.astype(BF16), w2b_ref[...]) + b2_ref[0])

    @pl.when(jnp.logical_not(used))
    def _():
        y_ref[...] = jnp.zeros(y_ref.shape, F32)


def _experts(block_expert, n_used, x_rows, w1, b1, w2, b2):
    n_rows = x_rows.shape[0]
    e, d, f2 = w1.shape
    f = w2.shape[1]
    nb = n_rows // EXPERT_BLOCK
    used = jnp.arange(nb, dtype=I32) < n_used[0]
    starts = jnp.logical_and(used, jnp.concatenate([jnp.ones((1,), bool), block_expert[1:] != block_expert[:-1]]))
    run_id = (jnp.cumsum(starts.astype(I32)) - 1).astype(I32)
    later = jnp.logical_and(block_expert[None, :] > block_expert[:, None], used[None, :])
    nxt = jnp.min(jnp.where(later, block_expert[None, :], N_EXPERTS), axis=1)
    nxt = jnp.where(nxt < N_EXPERTS, nxt, -1).astype(I32)
    tok_spec = pl.BlockSpec((EXPERT_BLOCK, ROW_TILES, LANES), lambda i, *_: (i, 0, 0))
    bias_spec = lambda w: pl.BlockSpec((1, 1, w), lambda i, be, *_: (be[i], 0, 0))
    return pl.pallas_call(
        _experts_kernel,
        out_shape=jax.ShapeDtypeStruct((n_rows, ROW_TILES, LANES), F32),
        grid_spec=pltpu.PrefetchScalarGridSpec(
            num_scalar_prefetch=4,
            grid=(nb,),
            in_specs=[tok_spec, pl.BlockSpec(memory_space=pl.ANY), bias_spec(f2),
                      pl.BlockSpec(memory_space=pl.ANY), bias_spec(d)],
            out_specs=tok_spec,
            scratch_shapes=[pltpu.VMEM((2, d, f2), F32), pltpu.VMEM((2, f, d), F32),
                            pltpu.VMEM((d, f2), BF16), pltpu.VMEM((f, d), BF16),
                            pltpu.SemaphoreType.DMA((2, 2))]),
        compiler_params=pltpu.CompilerParams(dimension_semantics=("arbitrary",),
                                             vmem_limit_bytes=VMEM_LIMIT),
        name="experts",
    )(block_expert, n_used, run_id, nxt, x_rows, w1, b1.reshape(e, 1, f2), w2, b2.reshape(e, 1, d))


COMBINE_TOKENS = 256


def _combine_kernel(rows_ref, next_rows_ref, y_hbm, gate_ref, x1_ref, g2_ref, fg_ref, o_ref, buf_ref, sems):
    n = COMBINE_TOKENS * TOP_K_EXPERTS
    i = pl.program_id(0)
    slot = i % 2

    half = COMBINE_TOKENS // 2
    has_next = i + 1 < pl.num_programs(0)

    def gather(rows, s, t0, t1):
        def issue(t, carry):
            for k in range(TOP_K_EXPERTS):
                pltpu.make_async_copy(y_hbm.at[rows[t * TOP_K_EXPERTS + k]], buf_ref.at[s, k, t],
                                      sems.at[s]).start(priority=k % 2)
            return carry

        lax.fori_loop(t0, t1, issue, 0)

    @pl.when(i == 0)
    def _():
        gather(rows_ref, 0, 0, COMBINE_TOKENS)

    @pl.when(has_next)
    def _():
        gather(next_rows_ref, 1 - slot, 0, half)

    _drain(lambda: pltpu.make_async_copy(y_hbm.at[0], buf_ref.at[slot, 0, 0], sems.at[slot]), n)

    gates = gate_ref[...]
    moe = gates[:, 0:1] * _load_token_tiles(buf_ref.at[slot, 0])
    for k in range(1, TOP_K_EXPERTS):
        moe = moe + gates[:, k:k + 1] * _load_token_tiles(buf_ref.at[slot, k])
    x2 = x1_ref[...] + g2_ref[0] * moe
    ms = jnp.mean(x2 * x2, axis=-1, keepdims=True)
    o_ref[...] = x2 * lax.rsqrt(ms + RMS_EPS) * fg_ref[...]

    @pl.when(has_next)
    def _():
        gather(next_rows_ref, 1 - slot, half, COMBINE_TOKENS)


def _combine(rows_flat, y_rows, gates, x1, g2, fg_row, seq):
    t, d = x1.shape
    tm = COMBINE_TOKENS
    per_b = seq // tm
    n = tm * TOP_K_EXPERTS
    steps = t // tm
    return pl.pallas_call(
        _combine_kernel,
        out_shape=jax.ShapeDtypeStruct((t, d), F32),
        grid=(steps,),
        in_specs=[pl.BlockSpec((n,), lambda i: (i,), memory_space=pltpu.SMEM),
                  pl.BlockSpec((n,), lambda i: (jnp.minimum(i + 1, steps - 1),), memory_space=pltpu.SMEM),
                  pl.BlockSpec(memory_space=pl.ANY),
                  pl.BlockSpec((tm, LANES), lambda i: (i, 0)),
                  pl.BlockSpec((tm, d), lambda i: (i, 0)),
                  pl.BlockSpec((1, 1, d), lambda i: (i // per_b, 0, 0)),
                  pl.BlockSpec((1, d), lambda i: (0, 0))],
        out_specs=pl.BlockSpec((tm, d), lambda i: (i, 0)),
        scratch_shapes=[pltpu.VMEM((2, TOP_K_EXPERTS, tm, ROW_TILES, LANES), F32),
                        pltpu.SemaphoreType.DMA((2,))],
        compiler_params=pltpu.CompilerParams(dimension_semantics=("arbitrary",),
                                             vmem_limit_bytes=VMEM_LIMIT),
        name="combine",
    )(rows_flat, rows_flat, y_rows, gates, x1, g2, fg_row)


def kernel(x, c, positions, ada_w, ada_b, norm1_g, w_in, hg_norm_g, lb_logits, w_out, norm2_g,
           router_w, router_b, moe_w1, moe_b1, moe_w2, moe_b2, final_g):
    batch, seq, d = x.shape
    t = batch * seq
    layer = 0
    x2 = x.reshape(t, d)

    c_pad = jnp.concatenate([c, jnp.zeros((SUBLANES - batch, d), F32)], axis=0)
    mod = _adaln(c_pad, ada_w[layer], ada_b[layer][None, :])[:batch]
    shift1, scale1, gate1, shift2, scale2, gate2 = jnp.split(mod, 6, axis=-1)
    row3 = lambda m: m[:, None, :]

    inv_freq = ROPE_THETA ** (-(jnp.arange(0, ROT_DIM, 2, dtype=F32) / ROT_DIM))
    cos_t, sin_t = _trig(positions.reshape(1, t).astype(F32), inv_freq[:, None])

    wl = w_in[layer]
    sp = np.cumsum((A_WIDTH, A_WIDTH, A_WIDTH, IDX_HEADS * IDX_DIM, IDX_DIM, IDX_HEADS,
                    B_HEADS * B_KEY_DIM, B_HEADS * B_KEY_DIM, B_WIDTH))
    w_qa, w_ka, w_va, w_qi, w_ki, w_wi, w_qb, w_fb, w_ib, w_gb = jnp.split(wl, [int(v) for v in sp], axis=1)
    w_t = jnp.concatenate([w_qa, w_va, w_qi, w_wi, jnp.zeros((d, SUBLANES - IDX_HEADS), F32)], axis=1)
    w_t = w_t.astype(BF16).T
    w_r = jnp.concatenate([w_ka, w_ki, jnp.zeros((d, LANES - IDX_DIM), F32), w_qb, w_fb, w_ib, w_gb],
                          axis=1).astype(BF16)

    g1n = norm1_g[layer]
    ka, ki, qb, fb, ib, gb, qa_t, v_blk, qi_t, wi_t = _inproj(
        x2, g1n[None, :], row3(scale1), row3(shift1), w_r, w_t, cos_t, sin_t, seq)

    out_a = _dsa(qi_t, wi_t, ki, qa_t, ka, v_blk, batch, seq)

    lower = jnp.cumsum(jax.nn.softmax(lb_logits.astype(F32), axis=0), axis=0)[layer]
    out_b = _hgrn(qb, fb, ib, gb, lower[None, :], hg_norm_g[layer][None, :], batch, seq)

    rw = jnp.concatenate([router_w[layer], jnp.zeros((d, LANES - N_EXPERTS), F32)], axis=1)
    rw_hi = rw.astype(BF16)
    rw_lo = (rw - rw_hi.astype(F32)).astype(BF16)
    rw = jnp.concatenate([rw_hi, rw_lo, rw_hi], axis=0)
    rb = jnp.concatenate([router_b[layer], jnp.full((LANES - N_EXPERTS,), NEG_BIG, F32)])[None, :]
    x1, h2, idx, gates, rank, counts = _outproj(
        out_a, out_b, x2, row3(gate1), w_out[layer].astype(BF16), norm2_g[layer][None, :],
        row3(scale2), row3(shift2), rw, rb, seq)

    counts = counts[0, :N_EXPERTS]
    padded = (counts + EXPERT_BLOCK - 1) // EXPERT_BLOCK * EXPERT_BLOCK
    pad_end = jnp.cumsum(padded)
    pad_start = pad_end - padded
    n_assign = t * TOP_K_EXPERTS
    n_blocks = -(-n_assign // EXPERT_BLOCK) + N_EXPERTS
    n_rows = n_blocks * EXPERT_BLOCK
    block_start = jnp.arange(n_blocks, dtype=I32) * EXPERT_BLOCK
    block_expert = jnp.minimum(jnp.sum(pad_end[None, :] <= block_start[:, None], axis=1),
                               N_EXPERTS - 1).astype(I32)
    n_used = (pad_end[-1:] // EXPERT_BLOCK).astype(I32)
    idx4 = idx[:, :TOP_K_EXPERTS]
    start_of = jnp.sum(jnp.where(idx4[:, :, None] == jnp.arange(N_EXPERTS, dtype=I32), pad_start.astype(I32), 0),
                       axis=-1)
    rows_flat = (start_of + rank[:, :TOP_K_EXPERTS]).astype(I32).reshape(-1)

    pad_end0 = jnp.concatenate([jnp.zeros((1,), I32), pad_end.astype(I32)])
    x_rows = _dispatch(rows_flat, pad_end0, h2, n_rows)
    y_rows = _experts(block_expert, n_used, x_rows, moe_w1[layer], moe_b1[layer],
                      moe_w2[layer], moe_b2[layer])
    out = _combine(rows_flat, y_rows, gates, x1, row3(gate2), final_g[None, :], seq)
    return out.reshape(batch, seq, d)
```
